```python
import math
import jax, jax.numpy as jnp
from jax import lax
import numpy as np

D_MODEL = 1024
BATCH = 16
SEQ = 2048
DEPTH = 1

HEAD_DIM = 64
NSA_HEADS = 8
NSA_KV_GROUPS = 2
NSA_REP = NSA_HEADS // NSA_KV_GROUPS
SB_HEADS = 8
CMP_BLOCK = 32
CMP_STRIDE = 16
CMP_HIDDEN = 2 * HEAD_DIM
SEL_BLOCK = 64
SEL_TOPK = 8
WINDOW = 512
Q_BLOCK = 128
SEL_Q_CHUNK = 64
D_FF = -(-8 * D_MODEL // (3 * 256)) * 256
RMS_EPS = 1e-6
NEG_INF = -1e30
FORCE_PRIORITY = 1e4
ATTN_SCALE = 1.0 / math.sqrt(HEAD_DIM)

NSA_Q_DIM = NSA_HEADS * HEAD_DIM
NSA_KV_DIM = 6 * NSA_KV_GROUPS * HEAD_DIM
NSA_GATE_DIM = 3 * NSA_HEADS
SB_QKV_DIM = 3 * SB_HEADS * HEAD_DIM
MERGE_DIM = 2 * D_MODEL
D_IN = NSA_Q_DIM + NSA_KV_DIM + NSA_GATE_DIM + SB_QKV_DIM + MERGE_DIM
IN_SPLITS = (NSA_Q_DIM, NSA_Q_DIM + NSA_KV_DIM, NSA_Q_DIM + NSA_KV_DIM + NSA_GATE_DIM,
             NSA_Q_DIM + NSA_KV_DIM + NSA_GATE_DIM + SB_QKV_DIM)

kernel_name = "hybrid_nsa_stickbreaking_gated_block"


def rmsnorm(x, g):
    xf = x.astype(jnp.float32)
    y = xf * lax.rsqrt(jnp.mean(xf * xf, axis=-1, keepdims=True) + RMS_EPS)
    return (y * g.astype(jnp.float32)).astype(x.dtype)


def modulate(h, shift, scale):
    return h * (1.0 + scale) + shift


def alibi_slopes():
    i = jnp.arange(1, NSA_HEADS + 1, dtype=jnp.float32)
    return jnp.exp2(-8.0 * i / NSA_HEADS).reshape(NSA_KV_GROUPS, NSA_REP)


def compress(k, pos_emb, w1, w2):
    n_cmp = (k.shape[2] - CMP_BLOCK) // CMP_STRIDE + 1
    idx = jnp.arange(n_cmp)[:, None] * CMP_STRIDE + jnp.arange(CMP_BLOCK)[None, :]
    blocks = k[:, :, idx] + pos_emb
    flat = blocks.reshape(*blocks.shape[:3], CMP_BLOCK * HEAD_DIM)
    return jax.nn.gelu(flat @ w1) @ w2


def nsa_compressed(q, kc, vc, slopes):
    S = q.shape[3]
    n = kc.shape[2]
    t = jnp.arange(S)
    end = jnp.arange(n) * CMP_STRIDE + CMP_BLOCK - 1
    dist = (t[:, None] - end[None, :]).astype(jnp.float32)
    valid = dist >= 0
    s = jnp.einsum('bgrqd,bgnd->bgrqn', q.astype(jnp.float32), kc.astype(jnp.float32)) * ATTN_SCALE
    s = s - slopes[:, :, None, None] * dist
    s = jnp.where(valid, s, NEG_INF)
    p = jax.nn.softmax(s, axis=-1)
    p = jnp.where(valid.any(-1)[:, None], p, 0.0)
    o = jnp.einsum('bgrqn,bgnd->bgrqd', p.astype(vc.dtype), vc)
    return o, p.sum(axis=2)


def nsa_select_indices(p_cmp, S):
    n_cmp = p_cmp.shape[-1]
    n_sel = S // SEL_BLOCK
    c_start = jnp.arange(n_cmp) * CMP_STRIDE
    c_end = c_start + CMP_BLOCK - 1
    s_start = jnp.arange(n_sel) * SEL_BLOCK
    s_end = s_start + SEL_BLOCK - 1
    overlap = ((c_start[:, None] <= s_end[None, :]) & (c_end[:, None] >= s_start[None, :])).astype(jnp.float32)
    p_slc = jnp.einsum('bgqn,nj->bgqj', p_cmp, overlap)
    cur = (jnp.arange(S) // SEL_BLOCK)[:, None]
    j = jnp.arange(n_sel)[None, :]
    valid = j <= cur
    forced = (j == 0) | (j == cur) | (j == cur - 1)
    prio = jnp.where(valid, p_slc, NEG_INF)
    prio = jnp.where(forced & valid, FORCE_PRIORITY, prio)
    _, idx = lax.top_k(prio, min(SEL_TOPK, n_sel))
    return idx


def nsa_selected(q, k, v, idx, slopes):
    B, G, R, S, hd = q.shape
    n = idx.shape[-1]
    kb = k.reshape(B, G, S // SEL_BLOCK, SEL_BLOCK, hd)
    vb = v.reshape(B, G, S // SEL_BLOCK, SEL_BLOCK, hd)
    nc = S // SEL_Q_CHUNK
    q_c = q.reshape(B, G, R, nc, SEL_Q_CHUNK, hd).transpose(3, 0, 1, 2, 4, 5)
    idx_c = idx.reshape(B, G, nc, SEL_Q_CHUNK, n).transpose(2, 0, 1, 3, 4)
    t_c = jnp.arange(S).reshape(nc, SEL_Q_CHUNK)
    bi = jnp.arange(B)[:, None, None, None]
    gi = jnp.arange(G)[None, :, None, None]

    def chunk(args):
        qq, ii, tt = args
        ks = kb[bi, gi, ii]
        vs = vb[bi, gi, ii]
        pos = ii[..., None] * SEL_BLOCK + jnp.arange(SEL_BLOCK)
        dist = (tt[:, None, None] - pos).astype(jnp.float32)
        s = jnp.einsum('bgrqd,bgqnld->bgrqnl', qq.astype(jnp.float32), ks.astype(jnp.float32)) * ATTN_SCALE
        s = s - slopes[:, :, None, None, None] * dist[:, :, None]
        s = jnp.where((dist >= 0)[:, :, None], s, NEG_INF)
        p = jax.nn.softmax(s, axis=(-2, -1))
        return jnp.einsum('bgrqnl,bgqnld->bgrqd', p.astype(v.dtype), vs)

    o = lax.map(chunk, (q_c, idx_c, t_c))
    return o.transpose(1, 2, 3, 0, 4, 5).reshape(B, G, R, S, hd)


def nsa_window(q, k, v, slopes):
    B, G, R, S, hd = q.shape
    kp = jnp.pad(k, ((0, 0), (0, 0), (WINDOW, 0), (0, 0)))
    vp = jnp.pad(v, ((0, 0), (0, 0), (WINDOW, 0), (0, 0)))
    nq = S // Q_BLOCK
    q_c = q.reshape(B, G, R, nq, Q_BLOCK, hd).transpose(3, 0, 1, 2, 4, 5)
    starts = jnp.arange(nq) * Q_BLOCK
    span = WINDOW + Q_BLOCK

    def blk(args):
        qq, q0 = args
        kk = lax.dynamic_slice_in_dim(kp, q0, span, axis=2)
        vv = lax.dynamic_slice_in_dim(vp, q0, span, axis=2)
        t = q0 + jnp.arange(Q_BLOCK)
        s_pos = q0 - WINDOW + jnp.arange(span)
        dist = t[:, None] - s_pos[None, :]
        mask = (dist >= 0) & (dist < WINDOW) & (s_pos[None, :] >= 0)
        s = jnp.einsum('bgrqd,bgkd->bgrqk', qq.astype(jnp.float32), kk.astype(jnp.float32)) * ATTN_SCALE
        s = s - slopes[:, :, None, None] * dist.astype(jnp.float32)
        s = jnp.where(mask, s, NEG_INF)
        p = jax.nn.softmax(s, axis=-1)
        return jnp.einsum('bgrqk,bgkd->bgrqd', p.astype(v.dtype), vv)

    o = lax.map(blk, (q_c, starts))
    return o.transpose(1, 2, 3, 0, 4, 5).reshape(B, G, R, S, hd)


def stick_breaking(q, k, v):
    B, H, S, hd = q.shape
    nq = S // Q_BLOCK
    q_c = q.reshape(B, H, nq, Q_BLOCK, hd).transpose(2, 0, 1, 3, 4)
    starts = jnp.arange(nq) * Q_BLOCK
    kf = k.astype(jnp.float32)
    s_pos = jnp.arange(S)

    def blk(args):
        qq, q0 = args
        t = q0 + jnp.arange(Q_BLOCK)
        mask = s_pos[None, :] < t[:, None]
        z = jnp.einsum('bhqd,bhkd->bhqk', qq.astype(jnp.float32), kf) * ATTN_SCALE
        log_keep = jnp.where(mask, -jax.nn.softplus(z), 0.0)
        later = lax.cumsum(log_keep, axis=3, reverse=True) - log_keep
        a = jnp.where(mask, jnp.exp(jax.nn.log_sigmoid(z) + later), 0.0)
        return jnp.einsum('bhqk,bhkd->bhqd', a.astype(v.dtype), v)

    o = lax.map(blk, (q_c, starts))
    return o.transpose(1, 2, 0, 3, 4).reshape(B, H, S, hd)


def hybrid_mixer(h, w_in, cmp_pos_k, cmp_w1_k, cmp_w2_k, cmp_pos_v, cmp_w1_v, cmp_w2_v,
                 w_proj_a, w_proj_b, w_out):
    B, S, _ = h.shape
    G, R, hd = NSA_KV_GROUPS, NSA_REP, HEAD_DIM
    proj = h @ w_in
    q_a, kv_a, gate_a, qkv_b, merge = jnp.split(proj, IN_SPLITS, axis=-1)

    q_a = q_a.reshape(B, S, G, R, hd).transpose(0, 2, 3, 1, 4)
    kv_a = kv_a.reshape(B, S, 6, G, hd).transpose(2, 0, 3, 1, 4)
    k_cmp, v_cmp, k_sel, v_sel, k_win, v_win = (kv_a[i] for i in range(6))
    g_a = jax.nn.sigmoid(gate_a.reshape(B, S, G, R, 3).transpose(0, 2, 3, 1, 4))
    slopes = alibi_slopes()
    kc = compress(k_cmp, cmp_pos_k, cmp_w1_k, cmp_w2_k)
    vc = compress(v_cmp, cmp_pos_v, cmp_w1_v, cmp_w2_v)
    o_cmp, p_cmp = nsa_compressed(q_a, kc, vc, slopes)
    idx = nsa_select_indices(p_cmp, S)
    o_sel = nsa_selected(q_a, k_sel, v_sel, idx, slopes)
    o_win = nsa_window(q_a, k_win, v_win, slopes)
    o_a = g_a[..., 0:1] * o_cmp + g_a[..., 1:2] * o_sel + g_a[..., 2:3] * o_win
    y_a = o_a.transpose(0, 3, 1, 2, 4).reshape(B, S, NSA_Q_DIM) @ w_proj_a

    qkv_b = qkv_b.reshape(B, S, 3, SB_HEADS, hd).transpose(2, 0, 3, 1, 4)
    o_b = stick_breaking(qkv_b[0], qkv_b[1], qkv_b[2])
    y_b = o_b.transpose(0, 2, 1, 3).reshape(B, S, SB_HEADS * hd) @ w_proj_b

    merge = merge.reshape(B, S, 2, D_MODEL)
    y = jax.nn.sigmoid(merge[:, :, 0]) * y_a + jax.nn.sigmoid(merge[:, :, 1]) * y_b
    return y @ w_out


def swiglu(h, w_gate, w_up, w_down):
    return (jax.nn.silu(h @ w_gate) * (h @ w_up)) @ w_down


def setup_inputs(seed: int = 0) -> dict:
    key = jax.random.key(seed)
    ks = jax.random.split(key, 20)

    def dense(k, shape, fan_in):
        return jax.random.normal(k, shape, jnp.float32) * (fan_in ** -0.5)

    def gain(k, shape):
        return 1.0 + 0.05 * jax.random.normal(k, shape, jnp.float32)

    L = DEPTH
    cmp_in = CMP_BLOCK * HEAD_DIM
    return {
        "x": jax.random.normal(ks[0], (BATCH, SEQ, D_MODEL), jnp.float32),
        "c": jax.random.normal(ks[1], (BATCH, D_MODEL), jnp.float32),
        "w_ada": dense(ks[2], (L, D_MODEL, 6 * D_MODEL), D_MODEL),
        "b_ada": 0.02 * jax.random.normal(ks[3], (L, 6 * D_MODEL), jnp.float32),
        "norm_mix_g": gain(ks[4], (L, D_MODEL)),
        "w_in": dense(ks[5], (L, D_MODEL, D_IN), D_MODEL),
        "cmp_pos_k": 0.02 * jax.random.normal(ks[6], (L, CMP_BLOCK, HEAD_DIM), jnp.float32),
        "cmp_w1_k": dense(ks[7], (L, cmp_in, CMP_HIDDEN), cmp_in),
        "cmp_w2_k": dense(ks[8], (L, CMP_HIDDEN, HEAD_DIM), CMP_HIDDEN),
        "cmp_pos_v": 0.02 * jax.random.normal(ks[9], (L, CMP_BLOCK, HEAD_DIM), jnp.float32),
        "cmp_w1_v": dense(ks[10], (L, cmp_in, CMP_HIDDEN), cmp_in),
        "cmp_w2_v": dense(ks[11], (L, CMP_HIDDEN, HEAD_DIM), CMP_HIDDEN),
        "w_proj_a": dense(ks[12], (L, NSA_Q_DIM, D_MODEL), NSA_Q_DIM),
        "w_proj_b": dense(ks[13], (L, SB_HEADS * HEAD_DIM, D_MODEL), SB_HEADS * HEAD_DIM),
        "w_out": dense(ks[14], (L, D_MODEL, D_MODEL), D_MODEL),
        "norm_ffn_g": gain(ks[15], (L, D_MODEL)),
        "w_ffn_gate": dense(ks[16], (L, D_MODEL, D_FF), D_MODEL),
        "w_ffn_up": dense(ks[17], (L, D_MODEL, D_FF), D_MODEL),
        "w_ffn_down": dense(ks[18], (L, D_FF, D_MODEL), D_FF),
        "norm_final_g": gain(ks[19], (D_MODEL,)),
    }


def reference(x, c, w_ada, b_ada, norm_mix_g, w_in, cmp_pos_k, cmp_w1_k, cmp_w2_k,
              cmp_pos_v, cmp_w1_v, cmp_w2_v, w_proj_a, w_proj_b, w_out, norm_ffn_g,
              w_ffn_gate, w_ffn_up, w_ffn_down, norm_final_g):
    for l in range(DEPTH):
        mod = jax.nn.silu(c) @ w_ada[l] + b_ada[l]
        sh1, sc1, g1, sh2, sc2, g2 = jnp.split(mod[:, None, :], 6, axis=-1)
        h = modulate(rmsnorm(x, norm_mix_g[l]), sh1, sc1)
        x = x + g1 * hybrid_mixer(h, w_in[l], cmp_pos_k[l], cmp_w1_k[l], cmp_w2_k[l],
                                  cmp_pos_v[l], cmp_w1_v[l], cmp_w2_v[l],
                                  w_proj_a[l], w_proj_b[l], w_out[l])
        h = modulate(rmsnorm(x, norm_ffn_g[l]), sh2, sc2)
        x = x + g2 * swiglu(h, w_ffn_gate[l], w_ffn_up[l], w_ffn_down[l])
    return rmsnorm(x, norm_final_g)
```

```python
import functools
import math

import jax
import jax.numpy as jnp
from jax import lax
from jax.experimental import pallas as pl
from jax.experimental.pallas import tpu as pltpu

F32 = jnp.float32
BF16 = jnp.bfloat16

HEAD_DIM = 64
NSA_HEADS = 8
NSA_GROUPS = 2
NSA_REP = NSA_HEADS // NSA_GROUPS
SB_HEADS = 8
CMP_BLOCK = 32
CMP_STRIDE = 16
CMP_HIDDEN = 2 * HEAD_DIM
SEL_BLOCK = 64
SEL_TOPK = 8
WINDOW = 512
RMS_EPS = 1e-6
NEG_INF = -1e30
FORCE_PRIORITY = 1e4
ATTN_SCALE = 1.0 / math.sqrt(HEAD_DIM)

LANES = 128
PAIR = 2 * HEAD_DIM
SEL_LANE0 = 32
GATE_PAD = 128

TM_IN = 512
TM_POST = 256
TQ_NSA = 128
SEL_CHUNK = 512
TQ_SB = 256
TK_SB = 256
FF_CHUNK = 256

VMEM_LIMIT = 56 * 1024 * 1024


def _cparams(*sem):
    return pltpu.CompilerParams(dimension_semantics=sem, vmem_limit_bytes=VMEM_LIMIT)


def _resident(shape):
    nd = len(shape)
    return pl.BlockSpec(shape, lambda *_: (0,) * nd, pipeline_mode=pl.Buffered(1))


def _dot(a, b):
    return jnp.dot(a, b, preferred_element_type=F32)


def _dot_nt(a, b):
    return lax.dot_general(a, b, (((1,), (1,)), ((), ())), preferred_element_type=F32)


def _split_bf16(v):
    hi = v.astype(BF16)
    lo = (v - hi.astype(F32)).astype(BF16)
    return hi, lo


def _rmsnorm(x, g):
    return x * lax.rsqrt(jnp.mean(x * x, axis=-1, keepdims=True) + RMS_EPS) * g


def _ada_kernel(c_ref, w_ref, b_ref, o_ref):
    c = c_ref[...]
    a = (c * jax.nn.sigmoid(c)).astype(BF16)
    o_ref[...] = _dot(a, w_ref[...].astype(BF16)) + b_ref[...]


def _ada(c, w, b):
    bsz, d = c.shape
    n = w.shape[1]
    tn = d
    return pl.pallas_call(
        _ada_kernel,
        grid=(n // tn,),
        in_specs=[pl.BlockSpec((bsz, d), lambda j: (0, 0)),
                  pl.BlockSpec((d, tn), lambda j: (0, j)),
                  pl.BlockSpec((1, tn), lambda j: (0, j))],
        out_specs=pl.BlockSpec((bsz, tn), lambda j: (0, j)),
        out_shape=jax.ShapeDtypeStruct((bsz, n), F32),
        compiler_params=_cparams("arbitrary"),
        name="ada",
    )(c, w, b.reshape(1, n))


def _in_proj_kernel(x_ref, mod_ref, g_ref, w_ref, qa_ref, kva_ref, qkvb_ref, mg_ref, ga_ref, *, splits):
    x = x_ref[0]
    h = _rmsnorm(x, g_ref[...]) * (1.0 + mod_ref[0, 1:2, :]) + mod_ref[0, 0:1, :]
    hb = h.astype(BF16)
    n_q, n_kv, n_g, n_b, n_m = splits
    col = 0
    for ref, width, act in ((qa_ref, n_q, None), (kva_ref, n_kv, None), (ga_ref, n_g, "sig"),
                            (qkvb_ref, n_b, None), (mg_ref, n_m, "sig")):
        for c0 in range(0, width, 512):
            cw = min(512, width - c0)
            r = _dot(hb, w_ref[:, col + c0:col + c0 + cw])
            if act == "sig":
                r = jax.nn.sigmoid(r)
            ref[0, :, c0:c0 + cw] = r.astype(ref.dtype)
        col += width


def _in_proj(x, mod3, g, w_all, splits):
    bsz, s, d = x.shape
    n_q, n_kv, n_g, n_b, n_m = splits
    tm = TM_IN
    out_shape = (jax.ShapeDtypeStruct((bsz, s, n_q), BF16),
                 jax.ShapeDtypeStruct((bsz, s, n_kv), BF16),
                 jax.ShapeDtypeStruct((bsz, s, n_b), BF16),
                 jax.ShapeDtypeStruct((bsz, s, n_m), BF16),
                 jax.ShapeDtypeStruct((bsz, s, n_g), F32))
    tok = lambda n: pl.BlockSpec((1, tm, n), lambda b, i: (b, i, 0))
    return pl.pallas_call(
        functools.partial(_in_proj_kernel, splits=splits),
        grid=(bsz, s // tm),
        in_specs=[tok(d),
                  pl.BlockSpec((1, 6, d), lambda b, i: (b, 0, 0)),
                  pl.BlockSpec((1, d), lambda b, i: (0, 0)),
                  _resident(w_all.shape)],
        out_specs=(tok(n_q), tok(n_kv), tok(n_b), tok(n_m), tok(n_g)),
        out_shape=out_shape,
        compiler_params=_cparams("parallel", "parallel"),
        name="in_proj",
    )(x, mod3, g.reshape(1, d), w_all)


def _gelu_tanh(x):
    return 0.5 * x * (1.0 + jnp.tanh(math.sqrt(2.0 / math.pi) * (x + 0.044715 * (x * x * x))))


def _compress_kernel(k16_ref, v16_ref, pk_ref, pv_ref, w1k_ref, w1v_ref, w2k_ref, w2v_ref, kc_ref, vc_ref):
    nc = k16_ref.shape[1]
    half = w1k_ref.shape[0] // 2

    def one(x_ref, p_ref, w1_ref, w2_ref, o_ref):
        xin = x_ref[0].astype(F32)
        top = (xin + p_ref[0:1, :]).astype(BF16)
        bot = (xin + p_ref[1:2, :]).astype(BF16)
        a = _dot(top, w1_ref[0:half, :])
        b = _dot(bot, w1_ref[half:, :])
        hid = a + pltpu.roll(b, nc - 1, 0)
        o_ref[0] = _dot(_gelu_tanh(hid).astype(BF16), w2_ref[...]).astype(o_ref.dtype)

    one(k16_ref, pk_ref, w1k_ref, w2k_ref, kc_ref)
    one(v16_ref, pv_ref, w1v_ref, w2v_ref, vc_ref)


def _compress(k16, v16, pk, pv, w1k, w1v, w2k, w2v):
    bsz, nc, wide = k16.shape
    blk = pl.BlockSpec((1, nc, wide), lambda b: (b, 0, 0))
    out = pl.BlockSpec((1, nc, LANES), lambda b: (b, 0, 0))
    full = lambda a: pl.BlockSpec(a.shape, lambda b: (0,) * a.ndim)
    return pl.pallas_call(
        _compress_kernel,
        grid=(bsz,),
        in_specs=[blk, blk, full(pk), full(pv), full(w1k), full(w1v), full(w2k), full(w2v)],
        out_specs=(out, out),
        out_shape=(jax.ShapeDtypeStruct((bsz, nc, LANES), BF16),) * 2,
        compiler_params=_cparams("parallel"),
        name="compress",
    )(k16, v16, pk, pv, w1k, w1v, w2k, w2v)


def _softmax_rows(s):
    m = jnp.max(s, axis=-1, keepdims=True)
    p = jnp.exp(s - m)
    return p, jnp.sum(p, axis=-1, keepdims=True)


def _nsa_kernel(qa_ref, kc_ref, vc_ref, ksel_ref, vsel_ref, kwin_ref, vwin_ref, kx_ref, kxc_ref, ovl_ref,
                ga_ref, o_ref, qaug_ref, oacc_ref, m_ref, l_ref, acc_ref):
    tq = TQ_NSA
    nh = NSA_HEADS
    qi = pl.program_id(1)
    q0 = pl.multiple_of(qi * tq, tq)
    nc = kc_ref.shape[1]
    n_cmp = nc - 1

    lane = lax.broadcasted_iota(jnp.int32, (tq, LANES), 1)
    row = lax.broadcasted_iota(jnp.int32, (tq, LANES), 0)
    lo = lane < HEAD_DIM
    t_col = q0 + row

    for g in range(NSA_GROUPS):
        for r in range(NSA_REP):
            h = g * NSA_REP + r
            slope = 2.0 ** (-(h + 1))
            qr = qa_ref[0, :, r * LANES:(r + 1) * LANES].astype(F32)
            qm = jnp.where(lo if g == 0 else jnp.logical_not(lo), qr, 0.0)
            ext = jnp.where(lane == 0, SEL_BLOCK * slope, jnp.where(lane == 1, slope, 0.0))
            qaug_ref[h * tq:(h + 1) * tq, 0:LANES] = qm.astype(BF16)
            qaug_ref[h * tq:(h + 1) * tq, LANES:2 * LANES] = ext.astype(BF16)

    ga = ga_ref[0]

    def gate(r, branch):
        c0 = r * 3 + branch
        c1 = NSA_REP * 3 + r * 3 + branch
        return jnp.where(lo, ga[:, c0:c0 + 1], ga[:, c1:c1 + 1])

    def emit(o_heads, branch, first):
        for r in range(NSA_REP):
            o0 = o_heads[r * tq:(r + 1) * tq]
            o1 = o_heads[(NSA_REP + r) * tq:(NSA_REP + r + 1) * tq]
            val = gate(r, branch) * jnp.where(lo, o0, o1)
            if first:
                oacc_ref[:, r * LANES:(r + 1) * LANES] = val
            else:
                oacc_ref[:, r * LANES:(r + 1) * LANES] += val

    qaug = qaug_ref[...]
    kc_aug = jnp.concatenate([kc_ref[0], kxc_ref[...]], axis=1)

    s = _dot_nt(qaug, kc_aug)
    n_l = lax.broadcasted_iota(jnp.int32, (tq, nc), 1)
    t_l = q0 + lax.broadcasted_iota(jnp.int32, (tq, nc), 0)
    valid = (t_l >= n_l * CMP_STRIDE + (CMP_BLOCK - 1)) & (n_l < n_cmp)
    any_valid = t_l[:, 0:1] >= (CMP_BLOCK - 1)
    ps = []
    for h in range(nh):
        p, l = _softmax_rows(jnp.where(valid, s[h * tq:(h + 1) * tq], NEG_INF))
        p = jnp.where(any_valid, p / l, 0.0)
        ps.append(p.astype(BF16))
    o_cmp = _dot(jnp.concatenate(ps, axis=0), vc_ref[0])
    emit(o_cmp, 0, True)

    st = _dot_nt(kc_aug, qaug)
    n_s = lax.broadcasted_iota(jnp.int32, (nc, tq), 0)
    t_s = q0 + lax.broadcasted_iota(jnp.int32, (nc, tq), 1)
    valid_t = (t_s >= n_s * CMP_STRIDE + (CMP_BLOCK - 1)) & (n_s < n_cmp)
    any_valid_t = t_s[0:1, :] >= (CMP_BLOCK - 1)
    nsel_rows = 32
    j_s = lax.broadcasted_iota(jnp.int32, (nsel_rows, tq), 0)
    cur = (q0 + lax.broadcasted_iota(jnp.int32, (nsel_rows, tq), 1)) // SEL_BLOCK
    valid_j = j_s <= cur
    forced = ((j_s == 0) | (j_s == cur) | (j_s == cur - 1)) & valid_j
    for g in range(NSA_GROUPS):
        psum = None
        for r in range(NSA_REP):
            h = g * NSA_REP + r
            sm = jnp.where(valid_t, st[:, h * tq:(h + 1) * tq], NEG_INF)
            m = jnp.max(sm, axis=0, keepdims=True)
            e = jnp.exp(sm - m)
            p = jnp.where(any_valid_t, e / jnp.sum(e, axis=0, keepdims=True), 0.0)
            psum = p if psum is None else psum + p
        p_hi, p_lo = _split_bf16(psum)
        p_slc = _dot(ovl_ref[...], p_hi) + _dot(ovl_ref[...], p_lo)
        prio = p_slc[SEL_LANE0:SEL_LANE0 + nsel_rows]
        prio = jnp.where(valid_j, prio, NEG_INF)
        prio = jnp.where(forced, FORCE_PRIORITY, prio)
        chosen = jnp.zeros((nsel_rows, tq), jnp.bool_)
        for _ in range(SEL_TOPK):
            best = jnp.max(prio, axis=0, keepdims=True)
            first = jnp.min(jnp.where(prio == best, j_s, nsel_rows), axis=0, keepdims=True)
            pick = j_s == first
            chosen = chosen | pick
            prio = jnp.where(pick, -3e38, prio)
        bias_t = jnp.where(chosen, 0.0, NEG_INF)
        bias_t = jnp.concatenate([jnp.zeros((SEL_LANE0, tq), F32), bias_t,
                                  jnp.zeros((LANES - SEL_LANE0 - nsel_rows, tq), F32)], axis=0)
        bias = bias_t.T
        for r in range(NSA_REP):
            h = g * NSA_REP + r
            slope = 2.0 ** (-(h + 1))
            ext = jnp.where(lane == 0, SEL_BLOCK * slope, jnp.where(lane == 1, slope, bias))
            oacc_rows = slice(h * tq, (h + 1) * tq)
            acc_ref[oacc_rows, :] = ext

    def window(k_start, span, mask_fn):
        k_aug = jnp.concatenate([kwin_ref[0, pl.ds(k_start, span), :], kx_ref[pl.ds(k_start, span), :]], axis=1)
        sw = _dot_nt(qaug, k_aug)
        ps, ls = [], []
        for h in range(nh):
            p, l = _softmax_rows(mask_fn(sw[h * tq:(h + 1) * tq]))
            ps.append(p.astype(BF16))
            ls.append(l)
        o = _dot(jnp.concatenate(ps, axis=0), vwin_ref[0, pl.ds(k_start, span), :])
        emit(o / jnp.concatenate(ls, axis=0), 2, False)

    c_l = lane
    r_l = row

    @pl.when(q0 >= WINDOW)
    def _():
        def mask_fn(sh):
            far = jnp.where(c_l > r_l, sh[:, 0:LANES], NEG_INF)
            diag = jnp.where(c_l <= r_l, sh[:, WINDOW:WINDOW + LANES], NEG_INF)
            return jnp.concatenate([far, sh[:, LANES:WINDOW], diag], axis=1)
        window(pl.multiple_of(q0 - WINDOW, tq), WINDOW + tq, mask_fn)

    @pl.when(q0 < WINDOW)
    def _():
        kpos = lax.broadcasted_iota(jnp.int32, (tq, WINDOW), 1)
        tpos = q0 + lax.broadcasted_iota(jnp.int32, (tq, WINDOW), 0)
        window(0, WINDOW, lambda sh: jnp.where(kpos <= tpos, sh, NEG_INF))

    qaug_ref[:, LANES:2 * LANES] = acc_ref[...].astype(BF16)
    qsel = qaug_ref[...]
    m_ref[...] = jnp.full(m_ref.shape, NEG_INF, F32)
    l_ref[...] = jnp.zeros(l_ref.shape, F32)
    acc_ref[...] = jnp.zeros(acc_ref.shape, F32)

    def sel_chunk(c, causal):
        k_start = pl.multiple_of(c * SEL_CHUNK, SEL_CHUNK)
        k_aug = jnp.concatenate([ksel_ref[0, pl.ds(k_start, SEL_CHUNK), :],
                                 kx_ref[pl.ds(k_start, SEL_CHUNK), :]], axis=1)
        ss = _dot_nt(qsel, k_aug)
        if causal:
            kpos = k_start + lax.broadcasted_iota(jnp.int32, (tq, SEL_CHUNK), 1)
            tpos = q0 + lax.broadcasted_iota(jnp.int32, (tq, SEL_CHUNK), 0)
            keep = kpos <= tpos
        ps = []
        for h in range(nh):
            rows = slice(h * tq, (h + 1) * tq)
            sh = ss[rows]
            if causal:
                sh = jnp.where(keep, sh, NEG_INF)
            m_old = m_ref[rows]
            m_new = jnp.maximum(m_old, jnp.max(sh, axis=-1, keepdims=True))
            alpha = jnp.exp(m_old - m_new)
            p = jnp.exp(sh - m_new)
            l_ref[rows] = alpha * l_ref[rows] + jnp.sum(p, axis=-1, keepdims=True)
            acc_ref[rows] = alpha * acc_ref[rows]
            m_ref[rows] = m_new
            ps.append(p.astype(BF16))
        acc_ref[...] += _dot(jnp.concatenate(ps, axis=0), vsel_ref[0, pl.ds(k_start, SEL_CHUNK), :])

    n_full = q0 // SEL_CHUNK

    def body(c, carry):
        sel_chunk(c, False)
        return carry

    lax.fori_loop(0, n_full, body, 0)
    sel_chunk(n_full, True)
    emit(acc_ref[...] / l_ref[...], 1, False)

    o_ref[0] = oacc_ref[...].astype(o_ref.dtype)


def _nsa(qa, kva, kc, vc, ga, kx, kxc, ovl):
    bsz, s, nq = qa.shape
    nc = kc.shape[1]
    tq = TQ_NSA
    kv_blk = lambda i: pl.BlockSpec((1, s, LANES), lambda b, q, i=i: (b, 0, i))
    cmp_blk = pl.BlockSpec((1, nc, LANES), lambda b, q: (b, 0, 0))
    const = lambda a: pl.BlockSpec(a.shape, lambda b, q: (0,) * a.ndim)
    return pl.pallas_call(
        _nsa_kernel,
        grid=(bsz, s // tq),
        in_specs=[pl.BlockSpec((1, tq, nq), lambda b, q: (b, q, 0)),
                  cmp_blk, cmp_blk, kv_blk(2), kv_blk(3), kv_blk(4), kv_blk(5),
                  const(kx), const(kxc), const(ovl),
                  pl.BlockSpec((1, tq, GATE_PAD), lambda b, q: (b, q, 0))],
        out_specs=pl.BlockSpec((1, tq, nq), lambda b, q: (b, q, 0)),
        out_shape=jax.ShapeDtypeStruct((bsz, s, nq), BF16),
        scratch_shapes=[pltpu.VMEM((NSA_HEADS * tq, 2 * LANES), BF16),
                        pltpu.VMEM((tq, nq), F32),
                        pltpu.VMEM((NSA_HEADS * tq, 1), F32),
                        pltpu.VMEM((NSA_HEADS * tq, 1), F32),
                        pltpu.VMEM((NSA_HEADS * tq, LANES), F32)],
        compiler_params=_cparams("parallel", "arbitrary"),
        name="nsa",
    )(qa, kc, vc, kva, kva, kva, kva, kx, kxc, ovl, ga)


def _sb_kernel(q_ref, k_ref, v_ref, tri_ref, o_ref, carry_ref, acc_ref):
    tq, tk = TQ_SB, TK_SB
    qi = pl.program_id(2)
    q0 = pl.multiple_of(qi * tq, tq)
    lane = lax.broadcasted_iota(jnp.int32, (tq, LANES), 1)
    lo = lane < HEAD_DIM
    q2 = q_ref[0].astype(F32)
    qs = jnp.concatenate([jnp.where(lo, q2, 0.0), jnp.where(lo, 0.0, q2)], axis=0).astype(BF16)
    tri = tri_ref[...]

    carry_ref[...] = jnp.zeros(carry_ref.shape, F32)
    acc_ref[...] = jnp.zeros(acc_ref.shape, F32)

    def tile(kt, masked):
        k_start = pl.multiple_of(kt * tk, tk)
        z = _dot_nt(qs, k_ref[0, pl.ds(k_start, tk), :])
        sp = jnp.maximum(z, 0.0) + jnp.log1p(jnp.exp(-jnp.abs(z)))
        lk = -sp
        if masked:
            kpos = lax.broadcasted_iota(jnp.int32, (tq, tk), 1)
            tpos = lax.broadcasted_iota(jnp.int32, (tq, tk), 0)
            keep = jnp.concatenate([kpos < tpos] * 2, axis=0)
            lk = jnp.where(keep, lk, 0.0)
        hi, lo_part = _split_bf16(lk)
        later = _dot(jnp.concatenate([hi, lo_part], axis=1), tri)
        a = jnp.exp(z - sp + later + carry_ref[...])
        if masked:
            a = jnp.where(keep, a, 0.0)
        acc_ref[...] += _dot(a.astype(BF16), v_ref[0, pl.ds(k_start, tk), :])
        carry_ref[...] += later[:, 0:1] + lk[:, 0:1]

    tile(qi, True)

    def body(i, c):
        tile(qi - 1 - i, False)
        return c

    lax.fori_loop(0, qi, body, 0)
    o_ref[0] = jnp.where(lo, acc_ref[0:tq], acc_ref[tq:2 * tq]).astype(o_ref.dtype)


def _sb(qkvb, tri):
    bsz, s, n3 = qkvb.shape
    npair = SB_HEADS // 2
    tq = TQ_SB
    return pl.pallas_call(
        _sb_kernel,
        grid=(bsz, npair, s // tq),
        in_specs=[pl.BlockSpec((1, tq, LANES), lambda b, p, q: (b, q, p)),
                  pl.BlockSpec((1, s, LANES), lambda b, p, q: (b, 0, npair + p)),
                  pl.BlockSpec((1, s, LANES), lambda b, p, q: (b, 0, 2 * npair + p)),
                  pl.BlockSpec(tri.shape, lambda b, p, q: (0, 0))],
        out_specs=pl.BlockSpec((1, tq, LANES), lambda b, p, q: (b, q, p)),
        out_shape=jax.ShapeDtypeStruct((bsz, s, npair * LANES), BF16),
        scratch_shapes=[pltpu.VMEM((2 * tq, 1), F32),
                        pltpu.VMEM((2 * tq, LANES), F32)],
        compiler_params=_cparams("parallel", "parallel", "arbitrary"),
        name="sb",
    )(qkvb, qkvb, qkvb, tri)


def _post_kernel(x_ref, oa_ref, ob_ref, mg_ref, mod_ref, wpa_ref, wpb_ref, wo_ref, g2_ref,
                 wg_ref, wu_ref, wd_ref, gf_ref, out_ref):
    d = x_ref.shape[2]
    x = x_ref[0]
    ya = _dot(oa_ref[0], wpa_ref[...])
    yb = _dot(ob_ref[0], wpb_ref[...])
    y = mg_ref[0, :, 0:d].astype(F32) * ya + mg_ref[0, :, d:2 * d].astype(F32) * yb
    x1 = x + mod_ref[0, 2:3, :] * _dot(y.astype(BF16), wo_ref[...])
    h2 = (_rmsnorm(x1, g2_ref[...]) * (1.0 + mod_ref[0, 4:5, :]) + mod_ref[0, 3:4, :]).astype(BF16)
    ffn = None
    for c in range(wg_ref.shape[0]):
        gte = _dot(h2, wg_ref[c])
        up = _dot(h2, wu_ref[c])
        act = (gte * jax.nn.sigmoid(gte) * up).astype(BF16)
        part = _dot(act, wd_ref[c])
        ffn = part if ffn is None else ffn + part
    x2 = x1 + mod_ref[0, 5:6, :] * ffn
    out_ref[0] = _rmsnorm(x2, gf_ref[...])


def _post(x, oa, ob, mg, mod3, wpa, wpb, wo, g2, wg3, wu3, wd3, gf):
    bsz, s, d = x.shape
    tm = TM_POST
    tok = lambda n: pl.BlockSpec((1, tm, n), lambda b, i: (b, i, 0))
    return pl.pallas_call(
        _post_kernel,
        grid=(bsz, s // tm),
        in_specs=[tok(d), tok(oa.shape[2]), tok(ob.shape[2]), tok(mg.shape[2]),
                  pl.BlockSpec((1, 6, d), lambda b, i: (b, 0, 0)),
                  _resident(wpa.shape), _resident(wpb.shape), _resident(wo.shape),
                  pl.BlockSpec((1, d), lambda b, i: (0, 0)),
                  _resident(wg3.shape), _resident(wu3.shape), _resident(wd3.shape),
                  pl.BlockSpec((1, d), lambda b, i: (0, 0))],
        out_specs=tok(d),
        out_shape=jax.ShapeDtypeStruct((bsz, s, d), F32),
        compiler_params=_cparams("parallel", "parallel"),
        name="post",
    )(x, oa, ob, mg, mod3, wpa, wpb, wo, g2.reshape(1, d), wg3, wu3, wd3, gf.reshape(1, d))


def _key_extras(pos):
    lane = jnp.arange(LANES)[None, :]
    a = (pos // SEL_BLOCK)[:, None]
    b = (pos % SEL_BLOCK)[:, None]
    ext = jnp.where(lane == 0, a, jnp.where(lane == 1, b, 0)).astype(F32)
    return ext, a


def kernel(x, c, w_ada, b_ada, norm_mix_g, w_in, cmp_pos_k, cmp_w1_k, cmp_w2_k, cmp_pos_v, cmp_w1_v, cmp_w2_v,
           w_proj_a, w_proj_b, w_out, norm_ffn_g, w_ffn_gate, w_ffn_up, w_ffn_down, norm_final_g):
    bsz, s, d = x.shape
    depth = w_ada.shape[0]
    assert s % SEL_CHUNK == 0 and s // SEL_BLOCK <= 32 and s % TM_IN == 0
    n_q = NSA_HEADS * HEAD_DIM
    n_kv = 6 * NSA_GROUPS * HEAD_DIM
    n_gate = 3 * NSA_HEADS
    n_b = 3 * SB_HEADS * HEAD_DIM
    n_m = 2 * d
    nc = s // CMP_STRIDE
    d_ff = w_ffn_gate.shape[2]
    assert d_ff % FF_CHUNK == 0

    pos = jnp.arange(s)
    ext, blk = _key_extras(pos)
    lane = jnp.arange(LANES)[None, :]
    kx = jnp.where(lane == SEL_LANE0 + blk, 1.0, ext).astype(BF16)
    kxc = _key_extras(jnp.arange(nc) * CMP_STRIDE + CMP_BLOCK - 1)[0].astype(BF16)
    jrow = jnp.arange(LANES)[:, None] - SEL_LANE0
    ncol = jnp.arange(nc)[None, :]
    ovl = ((ncol * CMP_STRIDE <= jrow * SEL_BLOCK + SEL_BLOCK - 1)
           & (ncol * CMP_STRIDE + CMP_BLOCK - 1 >= jrow * SEL_BLOCK)
           & (jrow >= 0) & (jrow < s // SEL_BLOCK) & (ncol < nc - 1)).astype(BF16)
    jj = jnp.arange(TK_SB)
    u = (jj[:, None] > jj[None, :]).astype(BF16)
    tri = jnp.concatenate([u, u], axis=0)

    for l in range(depth):
        w = w_in[l]
        wq = (w[:, :n_q] * ATTN_SCALE).reshape(d, NSA_GROUPS, NSA_REP, HEAD_DIM).transpose(0, 2, 1, 3).reshape(d, n_q)
        o1 = n_q + n_kv
        o2 = o1 + n_gate
        o3 = o2 + n_b
        wgate = jnp.pad(w[:, o1:o2], ((0, 0), (0, GATE_PAD - n_gate)))
        wb = jnp.concatenate([w[:, o2:o2 + n_b // 3] * ATTN_SCALE, w[:, o2 + n_b // 3:o3]], axis=1)
        w_all = jnp.concatenate([wq, w[:, n_q:o1], wgate, wb, w[:, o3:]], axis=1).astype(BF16)
        splits = (n_q, n_kv, GATE_PAD, n_b, n_m)

        def cmp_weights(pos_emb, w1, w2):
            half = CMP_BLOCK // 2
            w1r = w1.reshape(2, half, HEAD_DIM, CMP_HIDDEN)
            z = jnp.zeros_like(w1r)
            g0 = jnp.concatenate([w1r, z], axis=3)
            g1 = jnp.concatenate([z, w1r], axis=3)
            w1p = jnp.stack([g0, g1], axis=2).reshape(2 * half * 2 * HEAD_DIM, 2 * CMP_HIDDEN)
            zz = jnp.zeros_like(w2)
            w2p = jnp.concatenate([jnp.concatenate([w2, zz], axis=1), jnp.concatenate([zz, w2], axis=1)], axis=0)
            pe = pos_emb.reshape(2, half, 1, HEAD_DIM)
            pe = jnp.broadcast_to(pe, (2, half, 2, HEAD_DIM)).reshape(2, half * 2 * HEAD_DIM)
            return pe, w1p.astype(BF16), w2p.astype(BF16)

        pk, w1k, w2k = cmp_weights(cmp_pos_k[l], cmp_w1_k[l], cmp_w2_k[l])
        pv, w1v, w2v = cmp_weights(cmp_pos_v[l], cmp_w1_v[l], cmp_w2_v[l])
        wpa = w_proj_a[l].reshape(NSA_GROUPS, NSA_REP, HEAD_DIM, d).transpose(1, 0, 2, 3).reshape(n_q, d).astype(BF16)
        wpb = w_proj_b[l].astype(BF16)
        wo = w_out[l].astype(BF16)
        nchunk = d_ff // FF_CHUNK
        wg3 = w_ffn_gate[l].reshape(d, nchunk, FF_CHUNK).transpose(1, 0, 2).astype(BF16)
        wu3 = w_ffn_up[l].reshape(d, nchunk, FF_CHUNK).transpose(1, 0, 2).astype(BF16)
        wd3 = w_ffn_down[l].reshape(nchunk, FF_CHUNK, d).astype(BF16)

        mod3 = _ada(c, w_ada[l], b_ada[l]).reshape(bsz, 6, d)
        qa, kva, qkvb, mg, ga = _in_proj(x, mod3, norm_mix_g[l], w_all, splits)
        k16 = kva[:, :, 0:LANES].reshape(bsz, nc, CMP_STRIDE * LANES)
        v16 = kva[:, :, LANES:2 * LANES].reshape(bsz, nc, CMP_STRIDE * LANES)
        kc, vc = _compress(k16, v16, pk, pv, w1k, w1v, w2k, w2v)
        oa = _nsa(qa, kva, kc, vc, ga, kx, kxc, ovl)
        ob = _sb(qkvb, tri)
        gf = norm_final_g if l == depth - 1 else jnp.ones_like(norm_final_g)
        x = _post(x, oa, ob, mg, mod3, wpa, wpb, wo, norm_ffn_g[l], wg3, wu3, wd3, gf)
        assert depth == 1, "final norm is fused into the last layer's post kernel"
    return x
```

```python
import functools
import math

import jax
import jax.numpy as jnp
from jax import lax
from jax.experimental import pallas as pl
from jax.experimental.pallas import tpu as pltpu

F32 = jnp.float32
BF16 = jnp.bfloat16

HEAD_DIM = 64
NSA_HEADS = 8
NSA_GROUPS = 2
NSA_REP = NSA_HEADS // NSA_GROUPS
SB_HEADS = 8
CMP_BLOCK = 32
CMP_STRIDE = 16
CMP_HIDDEN = 2 * HEAD_DIM
SEL_BLOCK = 64
SEL_TOPK = 8
WINDOW = 512
RMS_EPS = 1e-6
NEG_INF = -1e30
FORCE_PRIORITY = 1e4
ATTN_SCALE = 1.0 / math.sqrt(HEAD_DIM)
LOG2E = math.log2(math.e)

LANES = 128
PAIR = 2 * HEAD_DIM
SEL_LANE0 = 32
GATE_PAD = 128

TM_IN = 512
TM_POST = 256
TQ_NSA = 128
SEL_CHUNK = 512
TQ_SB = 256
TK_SB = 256
FF_CHUNK = 256
SB_DEAD_LOG2 = -150.0

VMEM_LIMIT = 56 * 1024 * 1024


def _cparams(*sem):
    return pltpu.CompilerParams(dimension_semantics=sem, vmem_limit_bytes=VMEM_LIMIT)


def _resident(shape):
    nd = len(shape)
    return pl.BlockSpec(shape, lambda *_: (0,) * nd, pipeline_mode=pl.Buffered(1))


def _dot(a, b):
    return jnp.dot(a, b, preferred_element_type=F32)


def _dot_nt(a, b):
    return lax.dot_general(a, b, (((1,), (1,)), ((), ())), preferred_element_type=F32)


def _split_bf16(v):
    hi = v.astype(BF16)
    lo = (v - hi.astype(F32)).astype(BF16)
    return hi, lo


def _rmsnorm(x, g):
    return x * lax.rsqrt(jnp.mean(x * x, axis=-1, keepdims=True) + RMS_EPS) * g


def _ada_kernel(c_ref, w_ref, b_ref, o_ref):
    c = c_ref[...]
    a = (c * jax.nn.sigmoid(c)).astype(BF16)
    o_ref[...] = _dot(a, w_ref[...].astype(BF16)) + b_ref[...]


def _ada(c, w, b):
    bsz, d = c.shape
    n = w.shape[1]
    tn = d
    return pl.pallas_call(
        _ada_kernel,
        grid=(n // tn,),
        in_specs=[pl.BlockSpec((bsz, d), lambda j: (0, 0)),
                  pl.BlockSpec((d, tn), lambda j: (0, j)),
                  pl.BlockSpec((1, tn), lambda j: (0, j))],
        out_specs=pl.BlockSpec((bsz, tn), lambda j: (0, j)),
        out_shape=jax.ShapeDtypeStruct((bsz, n), F32),
        compiler_params=_cparams("arbitrary"),
        name="ada",
    )(c, w, b.reshape(1, n))


def _in_proj_kernel(x_ref, mod_ref, g_ref, w_ref, qa_ref, kva_ref, qkvb_ref, mg_ref, ga_ref, *, splits):
    x = x_ref[0]
    h = _rmsnorm(x, g_ref[...]) * (1.0 + mod_ref[0, 1:2, :]) + mod_ref[0, 0:1, :]
    hb = h.astype(BF16)
    n_q, n_kv, n_g, n_b, n_m = splits
    col = 0
    for ref, width, act in ((qa_ref, n_q, None), (kva_ref, n_kv, None), (ga_ref, n_g, "sig"),
                            (qkvb_ref, n_b, None), (mg_ref, n_m, "sig")):
        for c0 in range(0, width, 512):
            cw = min(512, width - c0)
            r = _dot(hb, w_ref[:, col + c0:col + c0 + cw])
            if act == "sig":
                r = jax.nn.sigmoid(r)
            ref[0, :, c0:c0 + cw] = r.astype(ref.dtype)
        col += width


def _in_proj(x, mod3, g, w_all, splits):
    bsz, s, d = x.shape
    n_q, n_kv, n_g, n_b, n_m = splits
    tm = TM_IN
    out_shape = (jax.ShapeDtypeStruct((bsz, s, n_q), BF16),
                 jax.ShapeDtypeStruct((bsz, s, n_kv), BF16),
                 jax.ShapeDtypeStruct((bsz, s, n_b), BF16),
                 jax.ShapeDtypeStruct((bsz, s, n_m), BF16),
                 jax.ShapeDtypeStruct((bsz, s, n_g), F32))
    tok = lambda n: pl.BlockSpec((1, tm, n), lambda b, i: (b, i, 0))
    return pl.pallas_call(
        functools.partial(_in_proj_kernel, splits=splits),
        grid=(bsz, s // tm),
        in_specs=[tok(d),
                  pl.BlockSpec((1, 6, d), lambda b, i: (b, 0, 0)),
                  pl.BlockSpec((1, d), lambda b, i: (0, 0)),
                  _resident(w_all.shape)],
        out_specs=(tok(n_q), tok(n_kv), tok(n_b), tok(n_m), tok(n_g)),
        out_shape=out_shape,
        compiler_params=_cparams("parallel", "parallel"),
        name="in_proj",
    )(x, mod3, g.reshape(1, d), w_all)


def _gelu_tanh(x):
    return 0.5 * x * (1.0 + jnp.tanh(math.sqrt(2.0 / math.pi) * (x + 0.044715 * (x * x * x))))


def _compress_kernel(k16_ref, v16_ref, pk_ref, pv_ref, w1k_ref, w1v_ref, w2k_ref, w2v_ref, kc_ref, vc_ref):
    nc = k16_ref.shape[1]
    half = w1k_ref.shape[0] // 2

    def one(x_ref, p_ref, w1_ref, w2_ref, o_ref):
        xin = x_ref[0].astype(F32)
        top = (xin + p_ref[0:1, :]).astype(BF16)
        bot = (xin + p_ref[1:2, :]).astype(BF16)
        a = _dot(top, w1_ref[0:half, :])
        b = _dot(bot, w1_ref[half:, :])
        hid = a + pltpu.roll(b, nc - 1, 0)
        o_ref[0] = _dot(_gelu_tanh(hid).astype(BF16), w2_ref[...]).astype(o_ref.dtype)

    one(k16_ref, pk_ref, w1k_ref, w2k_ref, kc_ref)
    one(v16_ref, pv_ref, w1v_ref, w2v_ref, vc_ref)


def _compress(k16, v16, pk, pv, w1k, w1v, w2k, w2v):
    bsz, nc, wide = k16.shape
    blk = pl.BlockSpec((1, nc, wide), lambda b: (b, 0, 0))
    out = pl.BlockSpec((1, nc, LANES), lambda b: (b, 0, 0))
    full = lambda a: pl.BlockSpec(a.shape, lambda b: (0,) * a.ndim)
    return pl.pallas_call(
        _compress_kernel,
        grid=(bsz,),
        in_specs=[blk, blk, full(pk), full(pv), full(w1k), full(w1v), full(w2k), full(w2v)],
        out_specs=(out, out),
        out_shape=(jax.ShapeDtypeStruct((bsz, nc, LANES), BF16),) * 2,
        compiler_params=_cparams("parallel"),
        name="compress",
    )(k16, v16, pk, pv, w1k, w1v, w2k, w2v)


def _softmax_rows(s):
    m = jnp.max(s, axis=-1, keepdims=True)
    p = jnp.exp(s - m)
    return p, jnp.sum(p, axis=-1, keepdims=True)


def _nsa_kernel(qa_ref, kc_ref, vc_ref, ksel_ref, vsel_ref, kwin_ref, vwin_ref, kx_ref, kxc_ref, ovl_ref,
                ga_ref, o_ref, qaug_ref, oacc_ref, m_ref, l_ref, acc_ref):
    tq = TQ_NSA
    nh = NSA_HEADS
    qi = pl.program_id(1)
    q0 = pl.multiple_of(qi * tq, tq)
    nc = kc_ref.shape[1]
    n_cmp = nc - 1

    lane = lax.broadcasted_iota(jnp.int32, (tq, LANES), 1)
    row = lax.broadcasted_iota(jnp.int32, (tq, LANES), 0)
    lo = lane < HEAD_DIM
    t_col = q0 + row

    for g in range(NSA_GROUPS):
        for r in range(NSA_REP):
            h = g * NSA_REP + r
            slope = 2.0 ** (-(h + 1))
            qr = qa_ref[0, :, r * LANES:(r + 1) * LANES].astype(F32)
            qm = jnp.where(lo if g == 0 else jnp.logical_not(lo), qr, 0.0)
            ext = jnp.where(lane == 0, SEL_BLOCK * slope, jnp.where(lane == 1, slope, 0.0))
            qaug_ref[h * tq:(h + 1) * tq, 0:LANES] = qm.astype(BF16)
            qaug_ref[h * tq:(h + 1) * tq, LANES:2 * LANES] = ext.astype(BF16)

    ga = ga_ref[0]

    def gate(r, branch):
        c0 = r * 3 + branch
        c1 = NSA_REP * 3 + r * 3 + branch
        return jnp.where(lo, ga[:, c0:c0 + 1], ga[:, c1:c1 + 1])

    def emit(o_heads, branch, first):
        for r in range(NSA_REP):
            o0 = o_heads[r * tq:(r + 1) * tq]
            o1 = o_heads[(NSA_REP + r) * tq:(NSA_REP + r + 1) * tq]
            val = gate(r, branch) * jnp.where(lo, o0, o1)
            if first:
                oacc_ref[:, r * LANES:(r + 1) * LANES] = val
            else:
                oacc_ref[:, r * LANES:(r + 1) * LANES] += val

    qaug = qaug_ref[...]
    kc_aug = jnp.concatenate([kc_ref[0], kxc_ref[...]], axis=1)

    s = _dot_nt(qaug, kc_aug)
    n_l = lax.broadcasted_iota(jnp.int32, (tq, nc), 1)
    t_l = q0 + lax.broadcasted_iota(jnp.int32, (tq, nc), 0)
    valid = (t_l >= n_l * CMP_STRIDE + (CMP_BLOCK - 1)) & (n_l < n_cmp)
    any_valid = t_l[:, 0:1] >= (CMP_BLOCK - 1)
    ps = []
    for h in range(nh):
        p, l = _softmax_rows(jnp.where(valid, s[h * tq:(h + 1) * tq], NEG_INF))
        p = jnp.where(any_valid, p / l, 0.0)
        ps.append(p.astype(BF16))
    o_cmp = _dot(jnp.concatenate(ps, axis=0), vc_ref[0])
    emit(o_cmp, 0, True)

    st = _dot_nt(kc_aug, qaug)
    n_s = lax.broadcasted_iota(jnp.int32, (nc, tq), 0)
    t_s = q0 + lax.broadcasted_iota(jnp.int32, (nc, tq), 1)
    valid_t = (t_s >= n_s * CMP_STRIDE + (CMP_BLOCK - 1)) & (n_s < n_cmp)
    any_valid_t = t_s[0:1, :] >= (CMP_BLOCK - 1)
    nsel_rows = 32
    j_s = lax.broadcasted_iota(jnp.int32, (nsel_rows, tq), 0)
    cur = (q0 + lax.broadcasted_iota(jnp.int32, (nsel_rows, tq), 1)) // SEL_BLOCK
    valid_j = j_s <= cur
    forced = ((j_s == 0) | (j_s == cur) | (j_s == cur - 1)) & valid_j
    for g in range(NSA_GROUPS):
        psum = None
        for r in range(NSA_REP):
            h = g * NSA_REP + r
            sm = jnp.where(valid_t, st[:, h * tq:(h + 1) * tq], NEG_INF)
            m = jnp.max(sm, axis=0, keepdims=True)
            e = jnp.exp(sm - m)
            p = jnp.where(any_valid_t, e / jnp.sum(e, axis=0, keepdims=True), 0.0)
            psum = p if psum is None else psum + p
        p_hi, p_lo = _split_bf16(psum)
        p_slc = _dot(ovl_ref[...], p_hi) + _dot(ovl_ref[...], p_lo)
        prio = p_slc[SEL_LANE0:SEL_LANE0 + nsel_rows]
        prio = jnp.where(valid_j, prio, NEG_INF)
        prio = jnp.where(forced, FORCE_PRIORITY, prio)
        chosen = jnp.zeros((nsel_rows, tq), jnp.bool_)
        for _ in range(SEL_TOPK):
            best = jnp.max(prio, axis=0, keepdims=True)
            first = jnp.min(jnp.where(prio == best, j_s, nsel_rows), axis=0, keepdims=True)
            pick = j_s == first
            chosen = chosen | pick
            prio = jnp.where(pick, -3e38, prio)
        bias_t = jnp.where(chosen, 0.0, NEG_INF)
        bias_t = jnp.concatenate([jnp.zeros((SEL_LANE0, tq), F32), bias_t,
                                  jnp.zeros((LANES - SEL_LANE0 - nsel_rows, tq), F32)], axis=0)
        bias = bias_t.T
        for r in range(NSA_REP):
            h = g * NSA_REP + r
            slope = 2.0 ** (-(h + 1))
            ext = jnp.where(lane == 0, SEL_BLOCK * slope, jnp.where(lane == 1, slope, bias))
            oacc_rows = slice(h * tq, (h + 1) * tq)
            acc_ref[oacc_rows, :] = ext

    def window(k_start, span, mask_fn):
        k_aug = jnp.concatenate([kwin_ref[0, pl.ds(k_start, span), :], kx_ref[pl.ds(k_start, span), :]], axis=1)
        sw = _dot_nt(qaug, k_aug)
        ps, ls = [], []
        for h in range(nh):
            p, l = _softmax_rows(mask_fn(sw[h * tq:(h + 1) * tq]))
            ps.append(p.astype(BF16))
            ls.append(l)
        o = _dot(jnp.concatenate(ps, axis=0), vwin_ref[0, pl.ds(k_start, span), :])
        emit(o / jnp.concatenate(ls, axis=0), 2, False)

    c_l = lane
    r_l = row

    @pl.when(q0 >= WINDOW)
    def _():
        def mask_fn(sh):
            far = jnp.where(c_l > r_l, sh[:, 0:LANES], NEG_INF)
            diag = jnp.where(c_l <= r_l, sh[:, WINDOW:WINDOW + LANES], NEG_INF)
            return jnp.concatenate([far, sh[:, LANES:WINDOW], diag], axis=1)
        window(pl.multiple_of(q0 - WINDOW, tq), WINDOW + tq, mask_fn)

    @pl.when(q0 < WINDOW)
    def _():
        kpos = lax.broadcasted_iota(jnp.int32, (tq, WINDOW), 1)
        tpos = q0 + lax.broadcasted_iota(jnp.int32, (tq, WINDOW), 0)
        window(0, WINDOW, lambda sh: jnp.where(kpos <= tpos, sh, NEG_INF))

    qaug_ref[:, LANES:2 * LANES] = acc_ref[...].astype(BF16)
    qsel = qaug_ref[...]
    m_ref[...] = jnp.full(m_ref.shape, NEG_INF, F32)
    l_ref[...] = jnp.zeros(l_ref.shape, F32)
    acc_ref[...] = jnp.zeros(acc_ref.shape, F32)

    def sel_chunk(c, causal):
        k_start = pl.multiple_of(c * SEL_CHUNK, SEL_CHUNK)
        k_aug = jnp.concatenate([ksel_ref[0, pl.ds(k_start, SEL_CHUNK), :],
                                 kx_ref[pl.ds(k_start, SEL_CHUNK), :]], axis=1)
        ss = _dot_nt(qsel, k_aug)
        if causal:
            kpos = k_start + lax.broadcasted_iota(jnp.int32, (tq, SEL_CHUNK), 1)
            tpos = q0 + lax.broadcasted_iota(jnp.int32, (tq, SEL_CHUNK), 0)
            keep = kpos <= tpos
        ps = []
        for h in range(nh):
            rows = slice(h * tq, (h + 1) * tq)
            sh = ss[rows]
            if causal:
                sh = jnp.where(keep, sh, NEG_INF)
            m_old = m_ref[rows]
            m_new = jnp.maximum(m_old, jnp.max(sh, axis=-1, keepdims=True))
            alpha = jnp.exp(m_old - m_new)
            p = jnp.exp(sh - m_new)
            l_ref[rows] = alpha * l_ref[rows] + jnp.sum(p, axis=-1, keepdims=True)
            acc_ref[rows] = alpha * acc_ref[rows]
            m_ref[rows] = m_new
            ps.append(p.astype(BF16))
        acc_ref[...] += _dot(jnp.concatenate(ps, axis=0), vsel_ref[0, pl.ds(k_start, SEL_CHUNK), :])

    n_full = q0 // SEL_CHUNK

    def body(c, carry):
        sel_chunk(c, False)
        return carry

    lax.fori_loop(0, n_full, body, 0)
    sel_chunk(n_full, True)
    emit(acc_ref[...] / l_ref[...], 1, False)

    o_ref[0] = oacc_ref[...].astype(o_ref.dtype)


def _nsa(qa, kva, kc, vc, ga, kx, kxc, ovl):
    bsz, s, nq = qa.shape
    nc = kc.shape[1]
    tq = TQ_NSA
    kv_blk = lambda i: pl.BlockSpec((1, s, LANES), lambda b, q, i=i: (b, 0, i))
    cmp_blk = pl.BlockSpec((1, nc, LANES), lambda b, q: (b, 0, 0))
    const = lambda a: pl.BlockSpec(a.shape, lambda b, q: (0,) * a.ndim)
    return pl.pallas_call(
        _nsa_kernel,
        grid=(bsz, s // tq),
        in_specs=[pl.BlockSpec((1, tq, nq), lambda b, q: (b, q, 0)),
                  cmp_blk, cmp_blk, kv_blk(2), kv_blk(3), kv_blk(4), kv_blk(5),
                  const(kx), const(kxc), const(ovl),
                  pl.BlockSpec((1, tq, GATE_PAD), lambda b, q: (b, q, 0))],
        out_specs=pl.BlockSpec((1, tq, nq), lambda b, q: (b, q, 0)),
        out_shape=jax.ShapeDtypeStruct((bsz, s, nq), BF16),
        scratch_shapes=[pltpu.VMEM((NSA_HEADS * tq, 2 * LANES), BF16),
                        pltpu.VMEM((tq, nq), F32),
                        pltpu.VMEM((NSA_HEADS * tq, 1), F32),
                        pltpu.VMEM((NSA_HEADS * tq, 1), F32),
                        pltpu.VMEM((NSA_HEADS * tq, LANES), F32)],
        compiler_params=_cparams("parallel", "arbitrary"),
        name="nsa",
    )(qa, kc, vc, kva, kva, kva, kva, kx, kxc, ovl, ga)


def _sb_kernel(q_ref, k_ref, v_ref, tri_ref, o_ref, carry_ref, acc_ref):
    tq, tk = TQ_SB, TK_SB
    qi = pl.program_id(2)
    lane = lax.broadcasted_iota(jnp.int32, (tq, LANES), 1)
    lo = lane < HEAD_DIM
    q2 = q_ref[0].astype(F32)
    qs = jnp.concatenate([jnp.where(lo, q2, 0.0), jnp.where(lo, 0.0, q2)], axis=0).astype(BF16)
    ntri = tri_ref[...]

    carry_ref[...] = jnp.zeros(carry_ref.shape, F32)
    acc_ref[...] = jnp.zeros(acc_ref.shape, F32)

    def tile(kt, masked):
        k_start = pl.multiple_of(kt * tk, tk)
        z = _dot_nt(qs, k_ref[0, pl.ds(k_start, tk), :])
        t = jnp.log2(1.0 + jnp.exp2(-jnp.abs(z)))
        sp = jnp.maximum(z, 0.0) + t
        ls = jnp.minimum(z, 0.0) - t
        if masked:
            kpos = lax.broadcasted_iota(jnp.int32, (tq, tk), 1)
            tpos = lax.broadcasted_iota(jnp.int32, (tq, tk), 0)
            keep = jnp.concatenate([kpos < tpos] * 2, axis=0)
            sp = jnp.where(keep, sp, 0.0)
        later = _dot(sp.astype(BF16), ntri)
        a = jnp.exp2(ls + later)
        if masked:
            a = jnp.where(keep, a, 0.0)
        pv = _dot(a.astype(BF16), v_ref[0, pl.ds(k_start, tk), :])
        carry = carry_ref[...]
        acc_ref[...] += jnp.exp2(carry) * pv
        carry_ref[...] = carry - jnp.sum(sp, axis=-1, keepdims=True)

    tile(qi, True)

    def live():
        return jnp.max(carry_ref[...]) > SB_DEAD_LOG2

    def cond(st):
        return (st[0] < qi) & st[1]

    def body(st):
        tile(qi - 1 - st[0], False)
        return st[0] + 1, live()

    lax.while_loop(cond, body, (jnp.int32(0), live()))
    o_ref[0] = jnp.where(lo, acc_ref[0:tq], acc_ref[tq:2 * tq]).astype(o_ref.dtype)


def _sb(qkvb, tri):
    bsz, s, n3 = qkvb.shape
    npair = SB_HEADS // 2
    tq = TQ_SB
    return pl.pallas_call(
        _sb_kernel,
        grid=(bsz, npair, s // tq),
        in_specs=[pl.BlockSpec((1, tq, LANES), lambda b, p, q: (b, q, p)),
                  pl.BlockSpec((1, s, LANES), lambda b, p, q: (b, 0, npair + p)),
                  pl.BlockSpec((1, s, LANES), lambda b, p, q: (b, 0, 2 * npair + p)),
                  pl.BlockSpec(tri.shape, lambda b, p, q: (0, 0))],
        out_specs=pl.BlockSpec((1, tq, LANES), lambda b, p, q: (b, q, p)),
        out_shape=jax.ShapeDtypeStruct((bsz, s, npair * LANES), BF16),
        scratch_shapes=[pltpu.VMEM((2 * tq, 1), F32),
                        pltpu.VMEM((2 * tq, LANES), F32)],
        compiler_params=_cparams("parallel", "parallel", "arbitrary"),
        name="sb",
    )(qkvb, qkvb, qkvb, tri)


def _post_kernel(x_ref, oa_ref, ob_ref, mg_ref, mod_ref, wpa_ref, wpb_ref, wo_ref, g2_ref,
                 wg_ref, wu_ref, wd_ref, gf_ref, out_ref):
    d = x_ref.shape[2]
    x = x_ref[0]
    ya = _dot(oa_ref[0], wpa_ref[...])
    yb = _dot(ob_ref[0], wpb_ref[...])
    y = mg_ref[0, :, 0:d].astype(F32) * ya + mg_ref[0, :, d:2 * d].astype(F32) * yb
    x1 = x + mod_ref[0, 2:3, :] * _dot(y.astype(BF16), wo_ref[...])
    h2 = (_rmsnorm(x1, g2_ref[...]) * (1.0 + mod_ref[0, 4:5, :]) + mod_ref[0, 3:4, :]).astype(BF16)
    ffn = None
    for c in range(wg_ref.shape[0]):
        gte = _dot(h2, wg_ref[c])
        up = _dot(h2, wu_ref[c])
        act = (gte * jax.nn.sigmoid(gte) * up).astype(BF16)
        part = _dot(act, wd_ref[c])
        ffn = part if ffn is None else ffn + part
    x2 = x1 + mod_ref[0, 5:6, :] * ffn
    out_ref[0] = _rmsnorm(x2, gf_ref[...])


def _post(x, oa, ob, mg, mod3, wpa, wpb, wo, g2, wg3, wu3, wd3, gf):
    bsz, s, d = x.shape
    tm = TM_POST
    tok = lambda n: pl.BlockSpec((1, tm, n), lambda b, i: (b, i, 0))
    return pl.pallas_call(
        _post_kernel,
        grid=(bsz, s // tm),
        in_specs=[tok(d), tok(oa.shape[2]), tok(ob.shape[2]), tok(mg.shape[2]),
                  pl.BlockSpec((1, 6, d), lambda b, i: (b, 0, 0)),
                  _resident(wpa.shape), _resident(wpb.shape), _resident(wo.shape),
                  pl.BlockSpec((1, d), lambda b, i: (0, 0)),
                  _resident(wg3.shape), _resident(wu3.shape), _resident(wd3.shape),
                  pl.BlockSpec((1, d), lambda b, i: (0, 0))],
        out_specs=tok(d),
        out_shape=jax.ShapeDtypeStruct((bsz, s, d), F32),
        compiler_params=_cparams("parallel", "parallel"),
        name="post",
    )(x, oa, ob, mg, mod3, wpa, wpb, wo, g2.reshape(1, d), wg3, wu3, wd3, gf.reshape(1, d))


def _key_extras(pos):
    lane = jnp.arange(LANES)[None, :]
    a = (pos // SEL_BLOCK)[:, None]
    b = (pos % SEL_BLOCK)[:, None]
    ext = jnp.where(lane == 0, a, jnp.where(lane == 1, b, 0)).astype(F32)
    return ext, a


def kernel(x, c, w_ada, b_ada, norm_mix_g, w_in, cmp_pos_k, cmp_w1_k, cmp_w2_k, cmp_pos_v, cmp_w1_v, cmp_w2_v,
           w_proj_a, w_proj_b, w_out, norm_ffn_g, w_ffn_gate, w_ffn_up, w_ffn_down, norm_final_g):
    bsz, s, d = x.shape
    depth = w_ada.shape[0]
    assert s % SEL_CHUNK == 0 and s // SEL_BLOCK <= 32 and s % TM_IN == 0
    n_q = NSA_HEADS * HEAD_DIM
    n_kv = 6 * NSA_GROUPS * HEAD_DIM
    n_gate = 3 * NSA_HEADS
    n_b = 3 * SB_HEADS * HEAD_DIM
    n_m = 2 * d
    nc = s // CMP_STRIDE
    d_ff = w_ffn_gate.shape[2]
    assert d_ff % FF_CHUNK == 0

    pos = jnp.arange(s)
    ext, blk = _key_extras(pos)
    lane = jnp.arange(LANES)[None, :]
    kx = jnp.where(lane == SEL_LANE0 + blk, 1.0, ext).astype(BF16)
    kxc = _key_extras(jnp.arange(nc) * CMP_STRIDE + CMP_BLOCK - 1)[0].astype(BF16)
    jrow = jnp.arange(LANES)[:, None] - SEL_LANE0
    ncol = jnp.arange(nc)[None, :]
    ovl = ((ncol * CMP_STRIDE <= jrow * SEL_BLOCK + SEL_BLOCK - 1)
           & (ncol * CMP_STRIDE + CMP_BLOCK - 1 >= jrow * SEL_BLOCK)
           & (jrow >= 0) & (jrow < s // SEL_BLOCK) & (ncol < nc - 1)).astype(BF16)
    jj = jnp.arange(TK_SB)
    tri = -(jj[:, None] > jj[None, :]).astype(BF16)

    for l in range(depth):
        w = w_in[l]
        wq = (w[:, :n_q] * ATTN_SCALE).reshape(d, NSA_GROUPS, NSA_REP, HEAD_DIM).transpose(0, 2, 1, 3).reshape(d, n_q)
        o1 = n_q + n_kv
        o2 = o1 + n_gate
        o3 = o2 + n_b
        wgate = jnp.pad(w[:, o1:o2], ((0, 0), (0, GATE_PAD - n_gate)))
        wb = jnp.concatenate([w[:, o2:o2 + n_b // 3] * (ATTN_SCALE * LOG2E), w[:, o2 + n_b // 3:o3]], axis=1)
        w_all = jnp.concatenate([wq, w[:, n_q:o1], wgate, wb, w[:, o3:]], axis=1).astype(BF16)
        splits = (n_q, n_kv, GATE_PAD, n_b, n_m)

        def cmp_weights(pos_emb, w1, w2):
            half = CMP_BLOCK // 2
            w1r = w1.reshape(2, half, HEAD_DIM, CMP_HIDDEN)
            z = jnp.zeros_like(w1r)
            g0 = jnp.concatenate([w1r, z], axis=3)
            g1 = jnp.concatenate([z, w1r], axis=3)
            w1p = jnp.stack([g0, g1], axis=2).reshape(2 * half * 2 * HEAD_DIM, 2 * CMP_HIDDEN)
            zz = jnp.zeros_like(w2)
            w2p = jnp.concatenate([jnp.concatenate([w2, zz], axis=1), jnp.concatenate([zz, w2], axis=1)], axis=0)
            pe = pos_emb.reshape(2, half, 1, HEAD_DIM)
            pe = jnp.broadcast_to(pe, (2, half, 2, HEAD_DIM)).reshape(2, half * 2 * HEAD_DIM)
            return pe, w1p.astype(BF16), w2p.astype(BF16)

        pk, w1k, w2k = cmp_weights(cmp_pos_k[l], cmp_w1_k[l], cmp_w2_k[l])
        pv, w1v, w2v = cmp_weights(cmp_pos_v[l], cmp_w1_v[l], cmp_w2_v[l])
        wpa = w_proj_a[l].reshape(NSA_GROUPS, NSA_REP, HEAD_DIM, d).transpose(1, 0, 2, 3).reshape(n_q, d).astype(BF16)
        wpb = w_proj_b[l].astype(BF16)
        wo = w_out[l].astype(BF16)
        nchunk = d_ff // FF_CHUNK
        wg3 = w_ffn_gate[l].reshape(d, nchunk, FF_CHUNK).transpose(1, 0, 2).astype(BF16)
        wu3 = w_ffn_up[l].reshape(d, nchunk, FF_CHUNK).transpose(1, 0, 2).astype(BF16)
        wd3 = w_ffn_down[l].reshape(nchunk, FF_CHUNK, d).astype(BF16)

        mod3 = _ada(c, w_ada[l], b_ada[l]).reshape(bsz, 6, d)
        qa, kva, qkvb, mg, ga = _in_proj(x, mod3, norm_mix_g[l], w_all, splits)
        k16 = kva[:, :, 0:LANES].reshape(bsz, nc, CMP_STRIDE * LANES)
        v16 = kva[:, :, LANES:2 * LANES].reshape(bsz, nc, CMP_STRIDE * LANES)
        kc, vc = _compress(k16, v16, pk, pv, w1k, w1v, w2k, w2v)
        oa = _nsa(qa, kva, kc, vc, ga, kx, kxc, ovl)
        ob = _sb(qkvb, tri)
        gf = norm_final_g if l == depth - 1 else jnp.ones_like(norm_final_g)
        x = _post(x, oa, ob, mg, mod3, wpa, wpb, wo, norm_ffn_g[l], wg3, wu3, wd3, gf)
        assert depth == 1, "final norm is fused into the last layer's post kernel"
    return x
```

```python
import functools
import math

import jax
import jax.numpy as jnp
import numpy as np
from jax import lax
from jax.experimental import pallas as pl
from jax.experimental.pallas import tpu as pltpu

F32 = jnp.float32
BF16 = jnp.bfloat16

HEAD_DIM = 64
NSA_HEADS = 8
NSA_GROUPS = 2
NSA_REP = NSA_HEADS // NSA_GROUPS
SB_HEADS = 8
CMP_BLOCK = 32
CMP_STRIDE = 16
CMP_HIDDEN = 2 * HEAD_DIM
SEL_BLOCK = 64
SEL_TOPK = 8
WINDOW = 512
RMS_EPS = 1e-6
NEG_INF = -1e30
FORCE_PRIORITY = 1e4
ATTN_SCALE = 1.0 / math.sqrt(HEAD_DIM)
LOG2E = math.log2(math.e)

LANES = 128
PAIR = 2 * HEAD_DIM
SEL_LANE0 = 32
GATE_PAD = 128

TM_IN = 512
TM_POST = 256
TQ_NSA = 128
SEL_CHUNK = 512
TQ_SB = 256
TK_SB = 256
FF_CHUNK = 256
SB_DEAD_LOG2 = -150.0

VMEM_LIMIT = 56 * 1024 * 1024


def _cparams(*sem):
    return pltpu.CompilerParams(dimension_semantics=sem, vmem_limit_bytes=VMEM_LIMIT)


def _resident(shape):
    nd = len(shape)
    return pl.BlockSpec(shape, lambda *_: (0,) * nd, pipeline_mode=pl.Buffered(1))


def _dot(a, b):
    return jnp.dot(a, b, preferred_element_type=F32)


def _dot_nt(a, b):
    return lax.dot_general(a, b, (((1,), (1,)), ((), ())), preferred_element_type=F32)


def _split_bf16(v):
    hi = v.astype(BF16)
    lo = (v - hi.astype(F32)).astype(BF16)
    return hi, lo


def _rmsnorm(x, g):
    return x * lax.rsqrt(jnp.mean(x * x, axis=-1, keepdims=True) + RMS_EPS) * g


def _ada_kernel(c_ref, w_ref, b_ref, o_ref):
    c = c_ref[...]
    a = (c * jax.nn.sigmoid(c)).astype(BF16)
    o_ref[...] = _dot(a, w_ref[...].astype(BF16)) + b_ref[...]


def _ada(c, w, b):
    bsz, d = c.shape
    n = w.shape[1]
    tn = d
    return pl.pallas_call(
        _ada_kernel,
        grid=(n // tn,),
        in_specs=[pl.BlockSpec((bsz, d), lambda j: (0, 0)),
                  pl.BlockSpec((d, tn), lambda j: (0, j)),
                  pl.BlockSpec((1, tn), lambda j: (0, j))],
        out_specs=pl.BlockSpec((bsz, tn), lambda j: (0, j)),
        out_shape=jax.ShapeDtypeStruct((bsz, n), F32),
        compiler_params=_cparams("arbitrary"),
        name="ada",
    )(c, w, b.reshape(1, n))


def _in_proj_kernel(x_ref, mod_ref, g_ref, w_ref, qa_ref, kva_ref, qkvb_ref, mg_ref, ga_ref, *, splits):
    x = x_ref[0]
    h = _rmsnorm(x, g_ref[...]) * (1.0 + mod_ref[0, 1:2, :]) + mod_ref[0, 0:1, :]
    hb = h.astype(BF16)
    n_q, n_kv, n_g, n_b, n_m = splits
    col = 0
    for ref, width, act in ((qa_ref, n_q, None), (kva_ref, n_kv, None), (ga_ref, n_g, "sig"),
                            (qkvb_ref, n_b, None), (mg_ref, n_m, "sig")):
        for c0 in range(0, width, 512):
            cw = min(512, width - c0)
            r = _dot(hb, w_ref[:, col + c0:col + c0 + cw])
            if act == "sig":
                r = jax.nn.sigmoid(r)
            ref[0, :, c0:c0 + cw] = r.astype(ref.dtype)
        col += width


def _in_proj(x, mod3, g, w_all, splits):
    bsz, s, d = x.shape
    n_q, n_kv, n_g, n_b, n_m = splits
    tm = TM_IN
    out_shape = (jax.ShapeDtypeStruct((bsz, s, n_q), BF16),
                 jax.ShapeDtypeStruct((bsz, s, n_kv), BF16),
                 jax.ShapeDtypeStruct((bsz, s, n_b), BF16),
                 jax.ShapeDtypeStruct((bsz, s, n_m), BF16),
                 jax.ShapeDtypeStruct((bsz, s, n_g), F32))
    tok = lambda n: pl.BlockSpec((1, tm, n), lambda b, i: (b, i, 0))
    return pl.pallas_call(
        functools.partial(_in_proj_kernel, splits=splits),
        grid=(bsz, s // tm),
        in_specs=[tok(d),
                  pl.BlockSpec((1, 6, d), lambda b, i: (b, 0, 0)),
                  pl.BlockSpec((1, d), lambda b, i: (0, 0)),
                  _resident(w_all.shape)],
        out_specs=(tok(n_q), tok(n_kv), tok(n_b), tok(n_m), tok(n_g)),
        out_shape=out_shape,
        compiler_params=_cparams("parallel", "parallel"),
        name="in_proj",
    )(x, mod3, g.reshape(1, d), w_all)


def _gelu_tanh(x):
    return 0.5 * x * (1.0 + jnp.tanh(math.sqrt(2.0 / math.pi) * (x + 0.044715 * (x * x * x))))


def _compress_kernel(k16_ref, v16_ref, pk_ref, pv_ref, w1k_ref, w1v_ref, w2k_ref, w2v_ref, kc_ref, vc_ref):
    nc = k16_ref.shape[1]
    half = w1k_ref.shape[0] // 2

    def one(x_ref, p_ref, w1_ref, w2_ref, o_ref):
        xin = x_ref[0].astype(F32)
        top = (xin + p_ref[0:1, :]).astype(BF16)
        bot = (xin + p_ref[1:2, :]).astype(BF16)
        a = _dot(top, w1_ref[0:half, :])
        b = _dot(bot, w1_ref[half:, :])
        hid = a + pltpu.roll(b, nc - 1, 0)
        o_ref[0] = _dot(_gelu_tanh(hid).astype(BF16), w2_ref[...]).astype(o_ref.dtype)

    one(k16_ref, pk_ref, w1k_ref, w2k_ref, kc_ref)
    one(v16_ref, pv_ref, w1v_ref, w2v_ref, vc_ref)


def _compress(k16, v16, pk, pv, w1k, w1v, w2k, w2v):
    bsz, nc, wide = k16.shape
    blk = pl.BlockSpec((1, nc, wide), lambda b: (b, 0, 0))
    out = pl.BlockSpec((1, nc, LANES), lambda b: (b, 0, 0))
    full = lambda a: pl.BlockSpec(a.shape, lambda b: (0,) * a.ndim)
    return pl.pallas_call(
        _compress_kernel,
        grid=(bsz,),
        in_specs=[blk, blk, full(pk), full(pv), full(w1k), full(w1v), full(w2k), full(w2v)],
        out_specs=(out, out),
        out_shape=(jax.ShapeDtypeStruct((bsz, nc, LANES), BF16),) * 2,
        compiler_params=_cparams("parallel"),
        name="compress",
    )(k16, v16, pk, pv, w1k, w1v, w2k, w2v)


def _nsa_kernel(qa_ref, kc_ref, vc_ref, ksel_ref, vsel_ref, kwin_ref, vwin_ref, kx_ref, kxc_ref, ovl_ref, qx_ref,
                ga_ref, o_ref, qaug_ref, oacc_ref, s_ref, mx_ref, acc_ref):
    tq = TQ_NSA
    nh = NSA_HEADS
    rows = nh * tq
    qi = pl.program_id(1)
    q0 = pl.multiple_of(qi * tq, tq)
    nc = kc_ref.shape[1]
    n_cmp = nc - 1

    lane = lax.broadcasted_iota(jnp.int32, (tq, LANES), 1)
    row = lax.broadcasted_iota(jnp.int32, (tq, LANES), 0)
    lo = lane < HEAD_DIM

    for g in range(NSA_GROUPS):
        for r in range(NSA_REP):
            h = g * NSA_REP + r
            qr = qa_ref[0, :, r * LANES:(r + 1) * LANES].astype(F32)
            qm = jnp.where(lo if g == 0 else jnp.logical_not(lo), qr, 0.0)
            qaug_ref[h * tq:(h + 1) * tq, 0:LANES] = qm.astype(BF16)
            qaug_ref[h * tq:(h + 1) * tq, LANES:2 * LANES] = jnp.broadcast_to(qx_ref[h:h + 1, :], (tq, LANES)).astype(BF16)

    ga = ga_ref[0]

    def emit(o_heads, branch, first):
        for r in range(NSA_REP):
            c0 = r * 3 + branch
            c1 = NSA_REP * 3 + r * 3 + branch
            gt = jnp.where(lo, ga[:, c0:c0 + 1], ga[:, c1:c1 + 1])
            val = gt * jnp.where(lo, o_heads[r * tq:(r + 1) * tq], o_heads[(NSA_REP + r) * tq:(NSA_REP + r + 1) * tq])
            if first:
                oacc_ref[:, r * LANES:(r + 1) * LANES] = val
            else:
                oacc_ref[:, r * LANES:(r + 1) * LANES] += val

    def softmax_pv(s3, v):
        w = s3.shape[-1]
        p = jnp.exp2(s3 - jnp.max(s3, axis=-1, keepdims=True)).astype(BF16).reshape(rows, w)
        pv = _dot(p, jnp.concatenate([v, jnp.ones((w, LANES), BF16)], axis=1))
        return pv[:, 0:LANES] / pv[:, LANES:2 * LANES]

    qaug = qaug_ref[...]
    kc_aug = jnp.concatenate([kc_ref[0], kxc_ref[...]], axis=1)

    s = _dot_nt(qaug, kc_aug).reshape(nh, tq, nc)
    n_l = lax.broadcasted_iota(jnp.int32, (tq, nc), 1)
    t_l = q0 + lax.broadcasted_iota(jnp.int32, (tq, nc), 0)
    valid = (t_l >= n_l * CMP_STRIDE + (CMP_BLOCK - 1)) & (n_l < n_cmp)
    o_cmp = softmax_pv(jnp.where(valid[None], s, NEG_INF), vc_ref[0])
    has_cmp = jnp.concatenate([q0 + row >= CMP_BLOCK - 1] * nh, axis=0)
    emit(jnp.where(has_cmp, o_cmp, 0.0), 0, True)

    st = _dot_nt(kc_aug, qaug)
    n_s = lax.broadcasted_iota(jnp.int32, (nc, rows), 0)
    t_s = q0 + (lax.broadcasted_iota(jnp.int32, (nc, rows), 1) & (tq - 1))
    sm = jnp.where((t_s >= n_s * CMP_STRIDE + (CMP_BLOCK - 1)) & (n_s < n_cmp), st, NEG_INF)
    e = jnp.exp2(sm - jnp.max(sm, axis=0, keepdims=True))
    p_t = jnp.where(t_s >= CMP_BLOCK - 1, e / jnp.sum(e, axis=0, keepdims=True), 0.0)
    nsel_rows = 32
    j_s = lax.broadcasted_iota(jnp.int32, (nsel_rows, tq), 0)
    cur = (q0 + lax.broadcasted_iota(jnp.int32, (nsel_rows, tq), 1)) // SEL_BLOCK
    valid_j = j_s <= cur
    forced = ((j_s == 0) | (j_s == cur) | (j_s == cur - 1)) & valid_j
    for g in range(NSA_GROUPS):
        base = g * NSA_REP * tq
        psum = (p_t[:, base:base + tq] + p_t[:, base + tq:base + 2 * tq]
                + p_t[:, base + 2 * tq:base + 3 * tq] + p_t[:, base + 3 * tq:base + 4 * tq])
        p_hi, p_lo = _split_bf16(psum)
        p_slc = _dot(ovl_ref[...], p_hi) + _dot(ovl_ref[...], p_lo)
        prio = p_slc[SEL_LANE0:SEL_LANE0 + nsel_rows]
        prio = jnp.where(valid_j, prio, NEG_INF)
        prio = jnp.where(forced, FORCE_PRIORITY, prio)
        chosen = jnp.zeros((nsel_rows, tq), jnp.bool_)
        for _ in range(SEL_TOPK):
            best = jnp.max(prio, axis=0, keepdims=True)
            first = jnp.min(jnp.where(prio == best, j_s, nsel_rows), axis=0, keepdims=True)
            pick = j_s == first
            chosen = chosen | pick
            prio = jnp.where(pick, -3e38, prio)
        bias_t = jnp.where(chosen, 0.0, NEG_INF)
        bias_t = jnp.concatenate([jnp.zeros((SEL_LANE0, tq), F32), bias_t,
                                  jnp.zeros((LANES - SEL_LANE0 - nsel_rows, tq), F32)], axis=0)
        bias = bias_t.T
        for r in range(NSA_REP):
            h = g * NSA_REP + r
            acc_ref[h * tq:(h + 1) * tq, 0:LANES] = bias + qx_ref[h:h + 1, :]

    span = WINDOW + tq
    w_start = pl.multiple_of(jnp.maximum(q0 - WINDOW, 0), tq)
    k_aug = jnp.concatenate([kwin_ref[0, pl.ds(w_start, span), :], kx_ref[pl.ds(w_start, span), :]], axis=1)
    dist = (q0 + lax.broadcasted_iota(jnp.int32, (tq, span), 0)) - (w_start + lax.broadcasted_iota(jnp.int32, (tq, span), 1))
    band = (dist >= 0) & (dist < WINDOW)
    s3 = jnp.where(band[None], _dot_nt(qaug, k_aug).reshape(nh, tq, span), NEG_INF)
    emit(softmax_pv(s3, vwin_ref[0, pl.ds(w_start, span), :]), 2, False)

    qaug_ref[:, LANES:2 * LANES] = acc_ref[:, 0:LANES].astype(BF16)
    qsel = qaug_ref[...]
    last = q0 // SEL_CHUNK

    def lane_max(ss):
        m = ss[:, 0:LANES]
        for c0 in range(LANES, SEL_CHUNK, LANES):
            m = jnp.maximum(m, ss[:, c0:c0 + LANES])
        return m

    def scores(c):
        k_start = pl.multiple_of(c * SEL_CHUNK, SEL_CHUNK)
        k_aug = jnp.concatenate([ksel_ref[0, pl.ds(k_start, SEL_CHUNK), :],
                                 kx_ref[pl.ds(k_start, SEL_CHUNK), :]], axis=1)
        return _dot_nt(qsel, k_aug)

    kpos = last * SEL_CHUNK + lax.broadcasted_iota(jnp.int32, (tq, SEL_CHUNK), 1)
    tpos = q0 + lax.broadcasted_iota(jnp.int32, (tq, SEL_CHUNK), 0)
    ss = jnp.where((kpos <= tpos)[None], scores(last).reshape(nh, tq, SEL_CHUNK), NEG_INF).reshape(rows, SEL_CHUNK)
    s_ref[last] = ss
    mx_ref[...] = lane_max(ss)

    def pass_a(c, carry):
        ss = scores(c)
        s_ref[c] = ss
        mx_ref[...] = jnp.maximum(mx_ref[...], lane_max(ss))
        return carry

    lax.fori_loop(0, last, pass_a, 0)
    mx_ref[...] = jnp.broadcast_to(jnp.max(mx_ref[...], axis=-1, keepdims=True), mx_ref.shape)
    acc_ref[...] = jnp.zeros(acc_ref.shape, F32)

    def pass_b(c, carry):
        k_start = pl.multiple_of(c * SEL_CHUNK, SEL_CHUNK)
        m = mx_ref[...]
        p = jnp.exp2(s_ref[c] - jnp.concatenate([m] * (SEL_CHUNK // LANES), axis=1)).astype(BF16)
        v_aug = jnp.concatenate([vsel_ref[0, pl.ds(k_start, SEL_CHUNK), :], jnp.ones((SEL_CHUNK, LANES), BF16)], axis=1)
        acc_ref[...] += _dot(p, v_aug)
        return carry

    lax.fori_loop(0, last + 1, pass_b, 0)
    emit(acc_ref[:, 0:LANES] / acc_ref[:, LANES:2 * LANES], 1, False)

    o_ref[0] = oacc_ref[...].astype(o_ref.dtype)


def _nsa(qa, kva, kc, vc, ga, kx, kxc, ovl, qx):
    bsz, s, nq = qa.shape
    nc = kc.shape[1]
    tq = TQ_NSA
    rows = NSA_HEADS * tq
    kv_blk = lambda i: pl.BlockSpec((1, s, LANES), lambda b, q, i=i: (b, 0, i))
    cmp_blk = pl.BlockSpec((1, nc, LANES), lambda b, q: (b, 0, 0))
    const = lambda a: pl.BlockSpec(a.shape, lambda b, q: (0,) * a.ndim)
    return pl.pallas_call(
        _nsa_kernel,
        grid=(bsz, s // tq),
        in_specs=[pl.BlockSpec((1, tq, nq), lambda b, q: (b, q, 0)),
                  cmp_blk, cmp_blk, kv_blk(2), kv_blk(3), kv_blk(4), kv_blk(5),
                  const(kx), const(kxc), const(ovl), const(qx),
                  pl.BlockSpec((1, tq, GATE_PAD), lambda b, q: (b, q, 0))],
        out_specs=pl.BlockSpec((1, tq, nq), lambda b, q: (b, q, 0)),
        out_shape=jax.ShapeDtypeStruct((bsz, s, nq), BF16),
        scratch_shapes=[pltpu.VMEM((rows, 2 * LANES), BF16),
                        pltpu.VMEM((tq, nq), F32),
                        pltpu.VMEM((s // SEL_CHUNK, rows, SEL_CHUNK), F32),
                        pltpu.VMEM((rows, LANES), F32),
                        pltpu.VMEM((rows, 2 * LANES), F32)],
        compiler_params=_cparams("parallel", "arbitrary"),
        name="nsa",
    )(qa, kc, vc, kva, kva, kva, kva, kx, kxc, ovl, qx, ga)


def _sb_kernel(q_ref, k_ref, v_ref, tri_ref, o_ref, carry_ref, acc_ref):
    tq, tk = TQ_SB, TK_SB
    qi = pl.program_id(2)
    lane = lax.broadcasted_iota(jnp.int32, (tq, LANES), 1)
    lo = lane < HEAD_DIM
    q2 = q_ref[0].astype(F32)
    qs = jnp.concatenate([jnp.where(lo, q2, 0.0), jnp.where(lo, 0.0, q2)], axis=0).astype(BF16)
    ntri = tri_ref[...]

    carry_ref[...] = jnp.zeros(carry_ref.shape, F32)
    acc_ref[...] = jnp.zeros(acc_ref.shape, F32)

    def tile(kt, masked):
        k_start = pl.multiple_of(kt * tk, tk)
        z = _dot_nt(qs, k_ref[0, pl.ds(k_start, tk), :])
        t = jnp.log2(1.0 + jnp.exp2(-jnp.abs(z)))
        sp = jnp.maximum(z, 0.0) + t
        ls = jnp.minimum(z, 0.0) - t
        if masked:
            kpos = lax.broadcasted_iota(jnp.int32, (tq, tk), 1)
            tpos = lax.broadcasted_iota(jnp.int32, (tq, tk), 0)
            keep = jnp.concatenate([kpos < tpos] * 2, axis=0)
            sp = jnp.where(keep, sp, 0.0)
        later = _dot(sp.astype(BF16), ntri)
        a = jnp.exp2(ls + later)
        if masked:
            a = jnp.where(keep, a, 0.0)
        pv = _dot(a.astype(BF16), v_ref[0, pl.ds(k_start, tk), :])
        carry = carry_ref[...]
        acc_ref[...] += jnp.exp2(carry) * pv
        carry_ref[...] = carry - jnp.sum(sp, axis=-1, keepdims=True)

    tile(qi, True)

    def live():
        return jnp.max(carry_ref[...]) > SB_DEAD_LOG2

    def cond(st):
        return (st[0] < qi) & st[1]

    def body(st):
        tile(qi - 1 - st[0], False)
        return st[0] + 1, live()

    lax.while_loop(cond, body, (jnp.int32(0), live()))
    o_ref[0] = jnp.where(lo, acc_ref[0:tq], acc_ref[tq:2 * tq]).astype(o_ref.dtype)


def _sb(qkvb, tri):
    bsz, s, n3 = qkvb.shape
    npair = SB_HEADS // 2
    tq = TQ_SB
    return pl.pallas_call(
        _sb_kernel,
        grid=(bsz, npair, s // tq),
        in_specs=[pl.BlockSpec((1, tq, LANES), lambda b, p, q: (b, q, p)),
                  pl.BlockSpec((1, s, LANES), lambda b, p, q: (b, 0, npair + p)),
                  pl.BlockSpec((1, s, LANES), lambda b, p, q: (b, 0, 2 * npair + p)),
                  pl.BlockSpec(tri.shape, lambda b, p, q: (0, 0))],
        out_specs=pl.BlockSpec((1, tq, LANES), lambda b, p, q: (b, q, p)),
        out_shape=jax.ShapeDtypeStruct((bsz, s, npair * LANES), BF16),
        scratch_shapes=[pltpu.VMEM((2 * tq, 1), F32),
                        pltpu.VMEM((2 * tq, LANES), F32)],
        compiler_params=_cparams("parallel", "parallel", "arbitrary"),
        name="sb",
    )(qkvb, qkvb, qkvb, tri)


def _post_kernel(x_ref, oa_ref, ob_ref, mg_ref, mod_ref, wpa_ref, wpb_ref, wo_ref, g2_ref,
                 wg_ref, wu_ref, wd_ref, gf_ref, out_ref):
    d = x_ref.shape[2]
    x = x_ref[0]
    ya = _dot(oa_ref[0], wpa_ref[...])
    yb = _dot(ob_ref[0], wpb_ref[...])
    y = mg_ref[0, :, 0:d].astype(F32) * ya + mg_ref[0, :, d:2 * d].astype(F32) * yb
    x1 = x + mod_ref[0, 2:3, :] * _dot(y.astype(BF16), wo_ref[...])
    h2 = (_rmsnorm(x1, g2_ref[...]) * (1.0 + mod_ref[0, 4:5, :]) + mod_ref[0, 3:4, :]).astype(BF16)
    ffn = None
    for c in range(wg_ref.shape[0]):
        gte = _dot(h2, wg_ref[c])
        up = _dot(h2, wu_ref[c])
        act = (gte * jax.nn.sigmoid(gte) * up).astype(BF16)
        part = _dot(act, wd_ref[c])
        ffn = part if ffn is None else ffn + part
    x2 = x1 + mod_ref[0, 5:6, :] * ffn
    out_ref[0] = _rmsnorm(x2, gf_ref[...])


def _post(x, oa, ob, mg, mod3, wpa, wpb, wo, g2, wg3, wu3, wd3, gf):
    bsz, s, d = x.shape
    tm = TM_POST
    tok = lambda n: pl.BlockSpec((1, tm, n), lambda b, i: (b, i, 0))
    return pl.pallas_call(
        _post_kernel,
        grid=(bsz, s // tm),
        in_specs=[tok(d), tok(oa.shape[2]), tok(ob.shape[2]), tok(mg.shape[2]),
                  pl.BlockSpec((1, 6, d), lambda b, i: (b, 0, 0)),
                  _resident(wpa.shape), _resident(wpb.shape), _resident(wo.shape),
                  pl.BlockSpec((1, d), lambda b, i: (0, 0)),
                  _resident(wg3.shape), _resident(wu3.shape), _resident(wd3.shape),
                  pl.BlockSpec((1, d), lambda b, i: (0, 0))],
        out_specs=tok(d),
        out_shape=jax.ShapeDtypeStruct((bsz, s, d), F32),
        compiler_params=_cparams("parallel", "parallel"),
        name="post",
    )(x, oa, ob, mg, mod3, wpa, wpb, wo, g2.reshape(1, d), wg3, wu3, wd3, gf.reshape(1, d))


def _bf16_pieces(x, n=3):
    out, rem = [], float(x)
    for _ in range(n):
        p = float(np.asarray(rem, np.float32).astype(jnp.bfloat16).astype(np.float32))
        out.append(p)
        rem -= p
    return out


N_PIECES = 3


def _key_extras(pos):
    lane = jnp.arange(LANES)[None, :]
    a = (pos // SEL_BLOCK)[:, None]
    b = (pos % SEL_BLOCK)[:, None]
    ext = jnp.where(lane < N_PIECES, a, jnp.where(lane < 2 * N_PIECES, b, 0)).astype(F32)
    return ext, a


def _query_extras():
    qx = np.zeros((NSA_HEADS, LANES), np.float32)
    for h in range(NSA_HEADS):
        slope = 2.0 ** (-(h + 1))
        for i, p in enumerate(_bf16_pieces(LOG2E, N_PIECES)):
            qx[h, i] = SEL_BLOCK * slope * p
            qx[h, N_PIECES + i] = slope * p
    return jnp.asarray(qx)


def kernel(x, c, w_ada, b_ada, norm_mix_g, w_in, cmp_pos_k, cmp_w1_k, cmp_w2_k, cmp_pos_v, cmp_w1_v, cmp_w2_v,
           w_proj_a, w_proj_b, w_out, norm_ffn_g, w_ffn_gate, w_ffn_up, w_ffn_down, norm_final_g):
    bsz, s, d = x.shape
    depth = w_ada.shape[0]
    assert s % SEL_CHUNK == 0 and s // SEL_BLOCK <= 32 and s % TM_IN == 0 and s >= WINDOW + TQ_NSA
    n_q = NSA_HEADS * HEAD_DIM
    n_kv = 6 * NSA_GROUPS * HEAD_DIM
    n_gate = 3 * NSA_HEADS
    n_b = 3 * SB_HEADS * HEAD_DIM
    n_m = 2 * d
    nc = s // CMP_STRIDE
    d_ff = w_ffn_gate.shape[2]
    assert d_ff % FF_CHUNK == 0

    pos = jnp.arange(s)
    ext, blk = _key_extras(pos)
    lane = jnp.arange(LANES)[None, :]
    kx = jnp.where(lane == SEL_LANE0 + blk, 1.0, ext).astype(BF16)
    kxc = _key_extras(jnp.arange(nc) * CMP_STRIDE + CMP_BLOCK - 1)[0].astype(BF16)
    jrow = jnp.arange(LANES)[:, None] - SEL_LANE0
    ncol = jnp.arange(nc)[None, :]
    ovl = ((ncol * CMP_STRIDE <= jrow * SEL_BLOCK + SEL_BLOCK - 1)
           & (ncol * CMP_STRIDE + CMP_BLOCK - 1 >= jrow * SEL_BLOCK)
           & (jrow >= 0) & (jrow < s // SEL_BLOCK) & (ncol < nc - 1)).astype(BF16)
    qx = _query_extras()
    jj = jnp.arange(TK_SB)
    tri = -(jj[:, None] > jj[None, :]).astype(BF16)

    for l in range(depth):
        w = w_in[l]
        wq = (w[:, :n_q] * (ATTN_SCALE * LOG2E)).reshape(d, NSA_GROUPS, NSA_REP, HEAD_DIM).transpose(0, 2, 1, 3).reshape(d, n_q)
        o1 = n_q + n_kv
        o2 = o1 + n_gate
        o3 = o2 + n_b
        wgate = jnp.pad(w[:, o1:o2], ((0, 0), (0, GATE_PAD - n_gate)))
        wb = jnp.concatenate([w[:, o2:o2 + n_b // 3] * (ATTN_SCALE * LOG2E), w[:, o2 + n_b // 3:o3]], axis=1)
        w_all = jnp.concatenate([wq, w[:, n_q:o1], wgate, wb, w[:, o3:]], axis=1).astype(BF16)
        splits = (n_q, n_kv, GATE_PAD, n_b, n_m)

        def cmp_weights(pos_emb, w1, w2):
            half = CMP_BLOCK // 2
            w1r = w1.reshape(2, half, HEAD_DIM, CMP_HIDDEN)
            z = jnp.zeros_like(w1r)
            g0 = jnp.concatenate([w1r, z], axis=3)
            g1 = jnp.concatenate([z, w1r], axis=3)
            w1p = jnp.stack([g0, g1], axis=2).reshape(2 * half * 2 * HEAD_DIM, 2 * CMP_HIDDEN)
            zz = jnp.zeros_like(w2)
            w2p = jnp.concatenate([jnp.concatenate([w2, zz], axis=1), jnp.concatenate([zz, w2], axis=1)], axis=0)
            pe = pos_emb.reshape(2, half, 1, HEAD_DIM)
            pe = jnp.broadcast_to(pe, (2, half, 2, HEAD_DIM)).reshape(2, half * 2 * HEAD_DIM)
            return pe, w1p.astype(BF16), w2p.astype(BF16)

        pk, w1k, w2k = cmp_weights(cmp_pos_k[l], cmp_w1_k[l], cmp_w2_k[l])
        pv, w1v, w2v = cmp_weights(cmp_pos_v[l], cmp_w1_v[l], cmp_w2_v[l])
        wpa = w_proj_a[l].reshape(NSA_GROUPS, NSA_REP, HEAD_DIM, d).transpose(1, 0, 2, 3).reshape(n_q, d).astype(BF16)
        wpb = w_proj_b[l].astype(BF16)
        wo = w_out[l].astype(BF16)
        nchunk = d_ff // FF_CHUNK
        wg3 = w_ffn_gate[l].reshape(d, nchunk, FF_CHUNK).transpose(1, 0, 2).astype(BF16)
        wu3 = w_ffn_up[l].reshape(d, nchunk, FF_CHUNK).transpose(1, 0, 2).astype(BF16)
        wd3 = w_ffn_down[l].reshape(nchunk, FF_CHUNK, d).astype(BF16)

        mod3 = _ada(c, w_ada[l], b_ada[l]).reshape(bsz, 6, d)
        qa, kva, qkvb, mg, ga = _in_proj(x, mod3, norm_mix_g[l], w_all, splits)
        k16 = kva[:, :, 0:LANES].reshape(bsz, nc, CMP_STRIDE * LANES)
        v16 = kva[:, :, LANES:2 * LANES].reshape(bsz, nc, CMP_STRIDE * LANES)
        kc, vc = _compress(k16, v16, pk, pv, w1k, w1v, w2k, w2v)
        oa = _nsa(qa, kva, kc, vc, ga, kx, kxc, ovl, qx)
        ob = _sb(qkvb, tri)
        gf = norm_final_g if l == depth - 1 else jnp.ones_like(norm_final_g)
        x = _post(x, oa, ob, mg, mod3, wpa, wpb, wo, norm_ffn_g[l], wg3, wu3, wd3, gf)
        assert depth == 1, "final norm is fused into the last layer's post kernel"
    return x
```

```python
import functools
import math

import jax
import jax.numpy as jnp
import numpy as np
from jax import lax
from jax.experimental import pallas as pl
from jax.experimental.pallas import tpu as pltpu

F32 = jnp.float32
BF16 = jnp.bfloat16

HEAD_DIM = 64
NSA_HEADS = 8
NSA_GROUPS = 2
NSA_REP = NSA_HEADS // NSA_GROUPS
SB_HEADS = 8
CMP_BLOCK = 32
CMP_STRIDE = 16
CMP_HIDDEN = 2 * HEAD_DIM
SEL_BLOCK = 64
SEL_TOPK = 8
WINDOW = 512
RMS_EPS = 1e-6
NEG_INF = -1e30
FORCE_PRIORITY = 1e4
ATTN_SCALE = 1.0 / math.sqrt(HEAD_DIM)
LOG2E = math.log2(math.e)

LANES = 128
PAIR = 2 * HEAD_DIM
SEL_LANE0 = 32
GATE_PAD = 128

TM_IN = 512
TM_POST = 256
TQ_NSA = 128
TQ_SELECT = 256
NSA_CHAINS = 2
SEL_CHUNK = 512
TQ_SB = 256
TK_SB = 256
FF_CHUNK = 256
SB_DEAD_LOG2 = -150.0

VMEM_LIMIT = 56 * 1024 * 1024


def _cparams(*sem):
    return pltpu.CompilerParams(dimension_semantics=sem, vmem_limit_bytes=VMEM_LIMIT)


def _resident(shape):
    nd = len(shape)
    return pl.BlockSpec(shape, lambda *_: (0,) * nd, pipeline_mode=pl.Buffered(1))


def _dot(a, b):
    return jnp.dot(a, b, preferred_element_type=F32)


def _dot_nt(a, b):
    return lax.dot_general(a, b, (((1,), (1,)), ((), ())), preferred_element_type=F32)


def _split_bf16(v):
    hi = v.astype(BF16)
    lo = (v - hi.astype(F32)).astype(BF16)
    return hi, lo


def _rmsnorm(x, g):
    return x * lax.rsqrt(jnp.mean(x * x, axis=-1, keepdims=True) + RMS_EPS) * g


def _ada_kernel(c_ref, w_ref, b_ref, o_ref):
    c = c_ref[...]
    a = (c * jax.nn.sigmoid(c)).astype(BF16)
    o_ref[...] = _dot(a, w_ref[...].astype(BF16)) + b_ref[...]


def _ada(c, w, b):
    bsz, d = c.shape
    n = w.shape[1]
    tn = d
    return pl.pallas_call(
        _ada_kernel,
        grid=(n // tn,),
        in_specs=[pl.BlockSpec((bsz, d), lambda j: (0, 0)),
                  pl.BlockSpec((d, tn), lambda j: (0, j)),
                  pl.BlockSpec((1, tn), lambda j: (0, j))],
        out_specs=pl.BlockSpec((bsz, tn), lambda j: (0, j)),
        out_shape=jax.ShapeDtypeStruct((bsz, n), F32),
        compiler_params=_cparams("arbitrary"),
        name="ada",
    )(c, w, b.reshape(1, n))


def _in_proj_kernel(x_ref, mod_ref, g_ref, w_ref, qa_ref, kva_ref, qkvb_ref, mg_ref, ga_ref, *, splits):
    x = x_ref[0]
    h = _rmsnorm(x, g_ref[...]) * (1.0 + mod_ref[0, 1:2, :]) + mod_ref[0, 0:1, :]
    hb = h.astype(BF16)
    n_q, n_kv, n_g, n_b, n_m = splits
    col = 0
    for ref, width, act in ((qa_ref, n_q, None), (kva_ref, n_kv, None), (ga_ref, n_g, "sig"),
                            (qkvb_ref, n_b, None), (mg_ref, n_m, "sig")):
        for c0 in range(0, width, 512):
            cw = min(512, width - c0)
            r = _dot(hb, w_ref[:, col + c0:col + c0 + cw])
            if act == "sig":
                r = jax.nn.sigmoid(r)
            ref[0, :, c0:c0 + cw] = r.astype(ref.dtype)
        col += width


def _in_proj(x, mod3, g, w_all, splits):
    bsz, s, d = x.shape
    n_q, n_kv, n_g, n_b, n_m = splits
    tm = TM_IN
    out_shape = (jax.ShapeDtypeStruct((bsz, s, n_q), BF16),
                 jax.ShapeDtypeStruct((bsz, s, n_kv), BF16),
                 jax.ShapeDtypeStruct((bsz, s, n_b), BF16),
                 jax.ShapeDtypeStruct((bsz, s, n_m), BF16),
                 jax.ShapeDtypeStruct((bsz, s, n_g), F32))
    tok = lambda n: pl.BlockSpec((1, tm, n), lambda b, i: (b, i, 0))
    return pl.pallas_call(
        functools.partial(_in_proj_kernel, splits=splits),
        grid=(bsz, s // tm),
        in_specs=[tok(d),
                  pl.BlockSpec((1, 6, d), lambda b, i: (b, 0, 0)),
                  pl.BlockSpec((1, d), lambda b, i: (0, 0)),
                  _resident(w_all.shape)],
        out_specs=(tok(n_q), tok(n_kv), tok(n_b), tok(n_m), tok(n_g)),
        out_shape=out_shape,
        compiler_params=_cparams("parallel", "parallel"),
        name="in_proj",
    )(x, mod3, g.reshape(1, d), w_all)


def _gelu_tanh(x):
    return 0.5 * x * (1.0 + jnp.tanh(math.sqrt(2.0 / math.pi) * (x + 0.044715 * (x * x * x))))


def _compress_kernel(k16_ref, v16_ref, pk_ref, pv_ref, w1k_ref, w1v_ref, w2k_ref, w2v_ref, kc_ref, vc_ref):
    nc = k16_ref.shape[1]
    half = w1k_ref.shape[0] // 2

    def one(x_ref, p_ref, w1_ref, w2_ref, o_ref):
        xin = x_ref[0].astype(F32)
        top = (xin + p_ref[0:1, :]).astype(BF16)
        bot = (xin + p_ref[1:2, :]).astype(BF16)
        a = _dot(top, w1_ref[0:half, :])
        b = _dot(bot, w1_ref[half:, :])
        hid = a + pltpu.roll(b, nc - 1, 0)
        o_ref[0] = _dot(_gelu_tanh(hid).astype(BF16), w2_ref[...]).astype(o_ref.dtype)

    one(k16_ref, pk_ref, w1k_ref, w2k_ref, kc_ref)
    one(v16_ref, pv_ref, w1v_ref, w2v_ref, vc_ref)


def _compress(k16, v16, pk, pv, w1k, w1v, w2k, w2v):
    bsz, nc, wide = k16.shape
    blk = pl.BlockSpec((1, nc, wide), lambda b: (b, 0, 0))
    out = pl.BlockSpec((1, nc, LANES), lambda b: (b, 0, 0))
    full = lambda a: pl.BlockSpec(a.shape, lambda b: (0,) * a.ndim)
    return pl.pallas_call(
        _compress_kernel,
        grid=(bsz,),
        in_specs=[blk, blk, full(pk), full(pv), full(w1k), full(w1v), full(w2k), full(w2v)],
        out_specs=(out, out),
        out_shape=(jax.ShapeDtypeStruct((bsz, nc, LANES), BF16),) * 2,
        compiler_params=_cparams("parallel"),
        name="compress",
    )(k16, v16, pk, pv, w1k, w1v, w2k, w2v)


def _build_qaug(qa_ref, qx_ref, qaug_ref, tq):
    lo = lax.broadcasted_iota(jnp.int32, (tq, LANES), 1) < HEAD_DIM
    for g in range(NSA_GROUPS):
        for r in range(NSA_REP):
            h = g * NSA_REP + r
            qr = qa_ref[0, :, r * LANES:(r + 1) * LANES].astype(F32)
            qm = jnp.where(lo if g == 0 else jnp.logical_not(lo), qr, 0.0)
            qaug_ref[h * tq:(h + 1) * tq, 0:LANES] = qm.astype(BF16)
            qaug_ref[h * tq:(h + 1) * tq, LANES:2 * LANES] = jnp.broadcast_to(qx_ref[h:h + 1, :], (tq, LANES)).astype(BF16)


def _nsa_select_kernel(qa_ref, kc_ref, kxc_ref, ovl_ref, qx_ref, bias_ref, qaug_ref):
    tq = TQ_SELECT
    rows = NSA_HEADS * tq
    q0 = pl.program_id(1) * tq
    nc = kc_ref.shape[1]
    n_cmp = nc - 1
    _build_qaug(qa_ref, qx_ref, qaug_ref, tq)
    kc_aug = jnp.concatenate([kc_ref[0], kxc_ref[...]], axis=1)
    st = _dot_nt(kc_aug, qaug_ref[...])
    n_s = lax.broadcasted_iota(jnp.int32, (nc, rows), 0)
    t_s = q0 + (lax.broadcasted_iota(jnp.int32, (nc, rows), 1) & (tq - 1))
    sm = jnp.where((t_s >= n_s * CMP_STRIDE + (CMP_BLOCK - 1)) & (n_s < n_cmp), st, NEG_INF)
    e = jnp.exp2(sm - jnp.max(sm, axis=0, keepdims=True))
    p_t = jnp.where(t_s >= CMP_BLOCK - 1, e / jnp.sum(e, axis=0, keepdims=True), 0.0)
    nsel_rows = 32
    j_s = lax.broadcasted_iota(jnp.int32, (nsel_rows, tq), 0)
    cur = (q0 + lax.broadcasted_iota(jnp.int32, (nsel_rows, tq), 1)) // SEL_BLOCK
    valid_j = j_s <= cur
    forced = ((j_s == 0) | (j_s == cur) | (j_s == cur - 1)) & valid_j
    for g in range(NSA_GROUPS):
        base = g * NSA_REP * tq
        psum = (p_t[:, base:base + tq] + p_t[:, base + tq:base + 2 * tq]
                + p_t[:, base + 2 * tq:base + 3 * tq] + p_t[:, base + 3 * tq:base + 4 * tq])
        p_hi, p_lo = _split_bf16(psum)
        p_slc = _dot(ovl_ref[...], p_hi) + _dot(ovl_ref[...], p_lo)
        prio = p_slc[SEL_LANE0:SEL_LANE0 + nsel_rows]
        prio = jnp.where(valid_j, prio, NEG_INF)
        prio = jnp.where(forced, FORCE_PRIORITY, prio)
        chosen = jnp.zeros((nsel_rows, tq), jnp.bool_)
        for _ in range(SEL_TOPK):
            best = jnp.max(prio, axis=0, keepdims=True)
            first = jnp.min(jnp.where(prio == best, j_s, nsel_rows), axis=0, keepdims=True)
            pick = j_s == first
            chosen = chosen | pick
            prio = jnp.where(pick, -3e38, prio)
        bias_t = jnp.where(chosen, 0.0, NEG_INF)
        bias_t = jnp.concatenate([jnp.zeros((SEL_LANE0, tq), F32), bias_t,
                                  jnp.zeros((LANES - SEL_LANE0 - nsel_rows, tq), F32)], axis=0)
        for c0 in range(0, tq, LANES):
            bias_ref[0, c0:c0 + LANES, g * LANES:(g + 1) * LANES] = bias_t[:, c0:c0 + LANES].T.astype(BF16)


def _nsa_select(qa, kc, kxc, ovl, qx):
    bsz, s, nq = qa.shape
    nc = kc.shape[1]
    tq = TQ_SELECT
    const = lambda a: pl.BlockSpec(a.shape, lambda b, q: (0,) * a.ndim)
    return pl.pallas_call(
        _nsa_select_kernel,
        grid=(bsz, s // tq),
        in_specs=[pl.BlockSpec((1, tq, nq), lambda b, q: (b, q, 0)),
                  pl.BlockSpec((1, nc, LANES), lambda b, q: (b, 0, 0)),
                  const(kxc), const(ovl), const(qx)],
        out_specs=pl.BlockSpec((1, tq, NSA_GROUPS * LANES), lambda b, q: (b, q, 0)),
        out_shape=jax.ShapeDtypeStruct((bsz, s, NSA_GROUPS * LANES), BF16),
        scratch_shapes=[pltpu.VMEM((NSA_HEADS * tq, 2 * LANES), BF16)],
        compiler_params=_cparams("parallel", "parallel"),
        name="nsa_select",
    )(qa, kc, kxc, ovl, qx)


def _nsa_kernel(qa_ref, kc_ref, vc_ref, ksel_ref, vsel_ref, kwin_ref, vwin_ref, kx_ref, kxc_ref, qx_ref, bias_ref,
                ga_ref, o_ref, qaug_ref, qsel_ref, oacc_ref):
    tq = TQ_NSA
    nh = NSA_HEADS
    rows = nh * tq
    qi = pl.program_id(1)
    q0 = pl.multiple_of(qi * tq, tq)
    nc = kc_ref.shape[1]
    n_cmp = nc - 1

    lane = lax.broadcasted_iota(jnp.int32, (tq, LANES), 1)
    row = lax.broadcasted_iota(jnp.int32, (tq, LANES), 0)
    lo = lane < HEAD_DIM

    _build_qaug(qa_ref, qx_ref, qaug_ref, tq)
    for h in range(nh):
        g = h // NSA_REP
        qsel_ref[h * tq:(h + 1) * tq, 0:LANES] = qaug_ref[h * tq:(h + 1) * tq, 0:LANES]
        qsel_ref[h * tq:(h + 1) * tq, LANES:2 * LANES] = (
            bias_ref[0, :, g * LANES:(g + 1) * LANES].astype(F32) + qx_ref[h:h + 1, :]).astype(BF16)

    ga = ga_ref[0]

    def emit(o_heads, branch, first):
        for r in range(NSA_REP):
            c0 = r * 3 + branch
            c1 = NSA_REP * 3 + r * 3 + branch
            gt = jnp.where(lo, ga[:, c0:c0 + 1], ga[:, c1:c1 + 1])
            val = gt * jnp.where(lo, o_heads[r * tq:(r + 1) * tq], o_heads[(NSA_REP + r) * tq:(NSA_REP + r + 1) * tq])
            if first:
                oacc_ref[:, r * LANES:(r + 1) * LANES] = val
            else:
                oacc_ref[:, r * LANES:(r + 1) * LANES] += val

    def attend(q, k_aug, v, mask_fn):
        w = k_aug.shape[0]
        v_aug = jnp.concatenate([v, jnp.ones((w, LANES), BF16)], axis=1)
        part = rows // NSA_CHAINS
        outs = []
        for c in range(NSA_CHAINS):
            s3 = mask_fn(_dot_nt(q[c * part:(c + 1) * part], k_aug).reshape(part // tq, tq, w))
            p = jnp.exp2(s3 - jnp.max(s3, axis=-1, keepdims=True)).astype(BF16).reshape(part, w)
            pv = _dot(p, v_aug)
            outs.append(pv[:, 0:LANES] / pv[:, LANES:2 * LANES])
        return jnp.concatenate(outs, axis=0)

    qaug = qaug_ref[...]

    qsel = qsel_ref[...]
    last = q0 // SEL_CHUNK

    def branches(n_full):
        kc_aug = jnp.concatenate([kc_ref[0], kxc_ref[...]], axis=1)
        n_l = lax.broadcasted_iota(jnp.int32, (tq, nc), 1)
        t_l = q0 + lax.broadcasted_iota(jnp.int32, (tq, nc), 0)
        valid = (t_l >= n_l * CMP_STRIDE + (CMP_BLOCK - 1)) & (n_l < n_cmp)
        o_cmp = attend(qaug, kc_aug, vc_ref[0], lambda s3: jnp.where(valid[None], s3, NEG_INF))
        has_cmp = jnp.concatenate([q0 + row >= CMP_BLOCK - 1] * nh, axis=0)
        emit(jnp.where(has_cmp, o_cmp, 0.0), 0, True)

        span = WINDOW + tq
        w_start = pl.multiple_of(jnp.maximum(q0 - WINDOW, 0), tq)
        k_aug = jnp.concatenate([kwin_ref[0, pl.ds(w_start, span), :], kx_ref[pl.ds(w_start, span), :]], axis=1)
        dist = (q0 + lax.broadcasted_iota(jnp.int32, (tq, span), 0)) - (w_start + lax.broadcasted_iota(jnp.int32, (tq, span), 1))
        band = (dist >= 0) & (dist < WINDOW)
        emit(attend(qaug, k_aug, vwin_ref[0, pl.ds(w_start, span), :], lambda s3: jnp.where(band[None], s3, NEG_INF)), 2, False)

        nk = (n_full + 1) * SEL_CHUNK
        k_aug = jnp.concatenate([ksel_ref[0, 0:nk, :], kx_ref[0:nk, :]], axis=1)
        kpos = n_full * SEL_CHUNK + lax.broadcasted_iota(jnp.int32, (tq, SEL_CHUNK), 1)
        tpos = q0 + lax.broadcasted_iota(jnp.int32, (tq, SEL_CHUNK), 0)
        causal = (kpos <= tpos)[None]

        def mask_fn(s3):
            diag = jnp.where(causal, s3[:, :, n_full * SEL_CHUNK:nk], NEG_INF)
            return jnp.concatenate([s3[:, :, 0:n_full * SEL_CHUNK], diag], axis=2) if n_full else diag

        emit(attend(qsel, k_aug, vsel_ref[0, 0:nk, :], mask_fn), 1, False)

    for n_full in range(ksel_ref.shape[1] // SEL_CHUNK):
        pl.when(last == n_full)(functools.partial(branches, n_full))

    o_ref[0] = oacc_ref[...].astype(o_ref.dtype)


def _nsa(qa, kva, kc, vc, ga, kx, kxc, qx, selbias):
    bsz, s, nq = qa.shape
    nc = kc.shape[1]
    tq = TQ_NSA
    rows = NSA_HEADS * tq
    kv_blk = lambda i: pl.BlockSpec((1, s, LANES), lambda b, q, i=i: (b, 0, i))
    cmp_blk = pl.BlockSpec((1, nc, LANES), lambda b, q: (b, 0, 0))
    const = lambda a: pl.BlockSpec(a.shape, lambda b, q: (0,) * a.ndim)
    return pl.pallas_call(
        _nsa_kernel,
        grid=(bsz, s // tq),
        in_specs=[pl.BlockSpec((1, tq, nq), lambda b, q: (b, q, 0)),
                  cmp_blk, cmp_blk, kv_blk(2), kv_blk(3), kv_blk(4), kv_blk(5),
                  const(kx), const(kxc), const(qx),
                  pl.BlockSpec((1, tq, NSA_GROUPS * LANES), lambda b, q: (b, q, 0)),
                  pl.BlockSpec((1, tq, GATE_PAD), lambda b, q: (b, q, 0))],
        out_specs=pl.BlockSpec((1, tq, nq), lambda b, q: (b, q, 0)),
        out_shape=jax.ShapeDtypeStruct((bsz, s, nq), BF16),
        scratch_shapes=[pltpu.VMEM((rows, 2 * LANES), BF16),
                        pltpu.VMEM((rows, 2 * LANES), BF16),
                        pltpu.VMEM((tq, nq), F32)],
        compiler_params=_cparams("parallel", "arbitrary"),
        name="nsa",
    )(qa, kc, vc, kva, kva, kva, kva, kx, kxc, qx, selbias, ga)


def _sb_kernel(q_ref, k_ref, v_ref, tri_ref, o_ref, carry_ref, acc_ref):
    tq, tk = TQ_SB, TK_SB
    qi = pl.program_id(2)
    lane = lax.broadcasted_iota(jnp.int32, (tq, LANES), 1)
    lo = lane < HEAD_DIM
    q2 = q_ref[0].astype(F32)
    qs = jnp.concatenate([jnp.where(lo, q2, 0.0), jnp.where(lo, 0.0, q2)], axis=0).astype(BF16)
    ntri = tri_ref[0:tk, :]

    def stick(z):
        t = jnp.log2(1.0 + jnp.exp2(-jnp.abs(z)))
        return jnp.maximum(z, 0.0) + t, jnp.minimum(z, 0.0) - t

    kpos = lax.broadcasted_iota(jnp.int32, (tq, tk), 1)
    tpos = lax.broadcasted_iota(jnp.int32, (tq, tk), 0)
    keep = jnp.concatenate([kpos < tpos] * 2, axis=0)

    @pl.when(qi == 0)
    def _():
        sp, ls = stick(_dot_nt(qs, k_ref[0, 0:tk, :]))
        later = _dot(jnp.where(keep, sp, 0.0).astype(BF16), ntri)
        a = jnp.where(keep, jnp.exp2(ls + later), 0.0)
        acc_ref[...] = _dot(a.astype(BF16), v_ref[0, 0:tk, :])

    @pl.when(qi > 0)
    def _():
        k_start = pl.multiple_of((qi - 1) * tk, tk)
        sp, ls = stick(_dot_nt(qs, k_ref[0, pl.ds(k_start, 2 * tk), :]))
        sp_b = jnp.concatenate([sp[:, 0:tk], jnp.where(keep, sp[:, tk:2 * tk], 0.0)], axis=1).astype(BF16)
        later_d = _dot(sp_b[:, tk:2 * tk], ntri)
        later_p = _dot(sp_b, tri_ref[...])
        a_d = jnp.where(keep, jnp.exp2(ls[:, tk:2 * tk] + later_d), 0.0)
        a_p = jnp.exp2(ls[:, 0:tk] + later_p)
        a = jnp.concatenate([a_p, a_d], axis=1).astype(BF16)
        acc_ref[...] = _dot(a, v_ref[0, pl.ds(k_start, 2 * tk), :])
        carry_ref[...] = later_p[:, 0:1] - sp_b[:, 0:1].astype(F32)

        def tile(kt):
            k0 = pl.multiple_of(kt * tk, tk)
            sp, ls = stick(_dot_nt(qs, k_ref[0, pl.ds(k0, tk), :]))
            later = _dot(sp.astype(BF16), ntri)
            pv = _dot(jnp.exp2(ls + later).astype(BF16), v_ref[0, pl.ds(k0, tk), :])
            carry = carry_ref[...]
            acc_ref[...] += jnp.exp2(carry) * pv
            carry_ref[...] = carry - jnp.sum(sp, axis=-1, keepdims=True)

        def live():
            return jnp.max(carry_ref[...]) > SB_DEAD_LOG2

        def cond(st):
            return (st[0] >= 0) & st[1]

        def body(st):
            tile(st[0])
            return st[0] - 1, live()

        lax.while_loop(cond, body, (qi - 2, live()))

    o_ref[0] = jnp.where(lo, acc_ref[0:tq], acc_ref[tq:2 * tq]).astype(o_ref.dtype)


def _sb(qkvb, tri):
    bsz, s, n3 = qkvb.shape
    npair = SB_HEADS // 2
    tq = TQ_SB
    return pl.pallas_call(
        _sb_kernel,
        grid=(bsz, npair, s // tq),
        in_specs=[pl.BlockSpec((1, tq, LANES), lambda b, p, q: (b, q, p)),
                  pl.BlockSpec((1, s, LANES), lambda b, p, q: (b, 0, npair + p)),
                  pl.BlockSpec((1, s, LANES), lambda b, p, q: (b, 0, 2 * npair + p)),
                  pl.BlockSpec(tri.shape, lambda b, p, q: (0, 0))],
        out_specs=pl.BlockSpec((1, tq, LANES), lambda b, p, q: (b, q, p)),
        out_shape=jax.ShapeDtypeStruct((bsz, s, npair * LANES), BF16),
        scratch_shapes=[pltpu.VMEM((2 * tq, 1), F32),
                        pltpu.VMEM((2 * tq, LANES), F32)],
        compiler_params=_cparams("parallel", "parallel", "arbitrary"),
        name="sb",
    )(qkvb, qkvb, qkvb, tri)


def _post_kernel(x_ref, oa_ref, ob_ref, mg_ref, mod_ref, wpa_ref, wpb_ref, wo_ref, g2_ref,
                 wg_ref, wu_ref, wd_ref, gf_ref, out_ref):
    d = x_ref.shape[2]
    x = x_ref[0]
    ya = _dot(oa_ref[0], wpa_ref[...])
    yb = _dot(ob_ref[0], wpb_ref[...])
    y = mg_ref[0, :, 0:d].astype(F32) * ya + mg_ref[0, :, d:2 * d].astype(F32) * yb
    x1 = x + mod_ref[0, 2:3, :] * _dot(y.astype(BF16), wo_ref[...])
    h2 = (_rmsnorm(x1, g2_ref[...]) * (1.0 + mod_ref[0, 4:5, :]) + mod_ref[0, 3:4, :]).astype(BF16)
    ffn = None
    for c in range(wg_ref.shape[0]):
        gte = _dot(h2, wg_ref[c])
        up = _dot(h2, wu_ref[c])
        act = (gte * jax.nn.sigmoid(gte) * up).astype(BF16)
        part = _dot(act, wd_ref[c])
        ffn = part if ffn is None else ffn + part
    x2 = x1 + mod_ref[0, 5:6, :] * ffn
    out_ref[0] = _rmsnorm(x2, gf_ref[...])


def _post(x, oa, ob, mg, mod3, wpa, wpb, wo, g2, wg3, wu3, wd3, gf):
    bsz, s, d = x.shape
    tm = TM_POST
    tok = lambda n: pl.BlockSpec((1, tm, n), lambda b, i: (b, i, 0))
    return pl.pallas_call(
        _post_kernel,
        grid=(bsz, s // tm),
        in_specs=[tok(d), tok(oa.shape[2]), tok(ob.shape[2]), tok(mg.shape[2]),
                  pl.BlockSpec((1, 6, d), lambda b, i: (b, 0, 0)),
                  _resident(wpa.shape), _resident(wpb.shape), _resident(wo.shape),
                  pl.BlockSpec((1, d), lambda b, i: (0, 0)),
                  _resident(wg3.shape), _resident(wu3.shape), _resident(wd3.shape),
                  pl.BlockSpec((1, d), lambda b, i: (0, 0))],
        out_specs=tok(d),
        out_shape=jax.ShapeDtypeStruct((bsz, s, d), F32),
        compiler_params=_cparams("parallel", "parallel"),
        name="post",
    )(x, oa, ob, mg, mod3, wpa, wpb, wo, g2.reshape(1, d), wg3, wu3, wd3, gf.reshape(1, d))


def _bf16_pieces(x, n=3):
    out, rem = [], float(x)
    for _ in range(n):
        p = float(np.asarray(rem, np.float32).astype(jnp.bfloat16).astype(np.float32))
        out.append(p)
        rem -= p
    return out


N_PIECES = 3


def _key_extras(pos):
    lane = jnp.arange(LANES)[None, :]
    a = (pos // SEL_BLOCK)[:, None]
    b = (pos % SEL_BLOCK)[:, None]
    ext = jnp.where(lane < N_PIECES, a, jnp.where(lane < 2 * N_PIECES, b, 0)).astype(F32)
    return ext, a


def _query_extras():
    qx = np.zeros((NSA_HEADS, LANES), np.float32)
    for h in range(NSA_HEADS):
        slope = 2.0 ** (-(h + 1))
        for i, p in enumerate(_bf16_pieces(LOG2E, N_PIECES)):
            qx[h, i] = SEL_BLOCK * slope * p
            qx[h, N_PIECES + i] = slope * p
    return jnp.asarray(qx)


def kernel(x, c, w_ada, b_ada, norm_mix_g, w_in, cmp_pos_k, cmp_w1_k, cmp_w2_k, cmp_pos_v, cmp_w1_v, cmp_w2_v,
           w_proj_a, w_proj_b, w_out, norm_ffn_g, w_ffn_gate, w_ffn_up, w_ffn_down, norm_final_g):
    bsz, s, d = x.shape
    depth = w_ada.shape[0]
    assert s % SEL_CHUNK == 0 and s // SEL_BLOCK <= 32 and s % TM_IN == 0 and s >= WINDOW + TQ_NSA
    n_q = NSA_HEADS * HEAD_DIM
    n_kv = 6 * NSA_GROUPS * HEAD_DIM
    n_gate = 3 * NSA_HEADS
    n_b = 3 * SB_HEADS * HEAD_DIM
    n_m = 2 * d
    nc = s // CMP_STRIDE
    d_ff = w_ffn_gate.shape[2]
    assert d_ff % FF_CHUNK == 0

    pos = jnp.arange(s)
    ext, blk = _key_extras(pos)
    lane = jnp.arange(LANES)[None, :]
    kx = jnp.where(lane == SEL_LANE0 + blk, 1.0, ext).astype(BF16)
    kxc = _key_extras(jnp.arange(nc) * CMP_STRIDE + CMP_BLOCK - 1)[0].astype(BF16)
    jrow = jnp.arange(LANES)[:, None] - SEL_LANE0
    ncol = jnp.arange(nc)[None, :]
    ovl = ((ncol * CMP_STRIDE <= jrow * SEL_BLOCK + SEL_BLOCK - 1)
           & (ncol * CMP_STRIDE + CMP_BLOCK - 1 >= jrow * SEL_BLOCK)
           & (jrow >= 0) & (jrow < s // SEL_BLOCK) & (ncol < nc - 1)).astype(BF16)
    qx = _query_extras()
    jj = jnp.arange(TK_SB)
    tri = -jnp.concatenate([jj[:, None] > jj[None, :], jnp.ones((TK_SB, TK_SB), jnp.bool_)], axis=0).astype(BF16)

    for l in range(depth):
        w = w_in[l]
        wq = (w[:, :n_q] * (ATTN_SCALE * LOG2E)).reshape(d, NSA_GROUPS, NSA_REP, HEAD_DIM).transpose(0, 2, 1, 3).reshape(d, n_q)
        o1 = n_q + n_kv
        o2 = o1 + n_gate
        o3 = o2 + n_b
        wgate = jnp.pad(w[:, o1:o2], ((0, 0), (0, GATE_PAD - n_gate)))
        wb = jnp.concatenate([w[:, o2:o2 + n_b // 3] * (ATTN_SCALE * LOG2E), w[:, o2 + n_b // 3:o3]], axis=1)
        w_all = jnp.concatenate([wq, w[:, n_q:o1], wgate, wb, w[:, o3:]], axis=1).astype(BF16)
        splits = (n_q, n_kv, GATE_PAD, n_b, n_m)

        def cmp_weights(pos_emb, w1, w2):
            half = CMP_BLOCK // 2
            w1r = w1.reshape(2, half, HEAD_DIM, CMP_HIDDEN)
            z = jnp.zeros_like(w1r)
            g0 = jnp.concatenate([w1r, z], axis=3)
            g1 = jnp.concatenate([z, w1r], axis=3)
            w1p = jnp.stack([g0, g1], axis=2).reshape(2 * half * 2 * HEAD_DIM, 2 * CMP_HIDDEN)
            zz = jnp.zeros_like(w2)
            w2p = jnp.concatenate([jnp.concatenate([w2, zz], axis=1), jnp.concatenate([zz, w2], axis=1)], axis=0)
            pe = pos_emb.reshape(2, half, 1, HEAD_DIM)
            pe = jnp.broadcast_to(pe, (2, half, 2, HEAD_DIM)).reshape(2, half * 2 * HEAD_DIM)
            return pe, w1p.astype(BF16), w2p.astype(BF16)

        pk, w1k, w2k = cmp_weights(cmp_pos_k[l], cmp_w1_k[l], cmp_w2_k[l])
        pv, w1v, w2v = cmp_weights(cmp_pos_v[l], cmp_w1_v[l], cmp_w2_v[l])
        wpa = w_proj_a[l].reshape(NSA_GROUPS, NSA_REP, HEAD_DIM, d).transpose(1, 0, 2, 3).reshape(n_q, d).astype(BF16)
        wpb = w_proj_b[l].astype(BF16)
        wo = w_out[l].astype(BF16)
        nchunk = d_ff // FF_CHUNK
        wg3 = w_ffn_gate[l].reshape(d, nchunk, FF_CHUNK).transpose(1, 0, 2).astype(BF16)
        wu3 = w_ffn_up[l].reshape(d, nchunk, FF_CHUNK).transpose(1, 0, 2).astype(BF16)
        wd3 = w_ffn_down[l].reshape(nchunk, FF_CHUNK, d).astype(BF16)

        mod3 = _ada(c, w_ada[l], b_ada[l]).reshape(bsz, 6, d)
        qa, kva, qkvb, mg, ga = _in_proj(x, mod3, norm_mix_g[l], w_all, splits)
        k16 = kva[:, :, 0:LANES].reshape(bsz, nc, CMP_STRIDE * LANES)
        v16 = kva[:, :, LANES:2 * LANES].reshape(bsz, nc, CMP_STRIDE * LANES)
        kc, vc = _compress(k16, v16, pk, pv, w1k, w1v, w2k, w2v)
        selbias = _nsa_select(qa, kc, kxc, ovl, qx)
        oa = _nsa(qa, kva, kc, vc, ga, kx, kxc, qx, selbias)
        ob = _sb(qkvb, tri)
        gf = norm_final_g if l == depth - 1 else jnp.ones_like(norm_final_g)
        x = _post(x, oa, ob, mg, mod3, wpa, wpb, wo, norm_ffn_g[l], wg3, wu3, wd3, gf)
        assert depth == 1, "final norm is fused into the last layer's post kernel"
    return x
```

```python
import functools
import math

import jax
import jax.numpy as jnp
import numpy as np
from jax import lax
from jax.experimental import pallas as pl
from jax.experimental.pallas import tpu as pltpu

F32 = jnp.float32
BF16 = jnp.bfloat16

HEAD_DIM = 64
NSA_HEADS = 8
NSA_GROUPS = 2
NSA_REP = NSA_HEADS // NSA_GROUPS
SB_HEADS = 8
CMP_BLOCK = 32
CMP_STRIDE = 16
CMP_HIDDEN = 2 * HEAD_DIM
SEL_BLOCK = 64
SEL_TOPK = 8
WINDOW = 512
RMS_EPS = 1e-6
NEG_INF = -1e30
FORCE_PRIORITY = 1e4
ATTN_SCALE = 1.0 / math.sqrt(HEAD_DIM)
LOG2E = math.log2(math.e)

LANES = 128
PAIR = 2 * HEAD_DIM
SEL_LANE0 = 32
GATE_PAD = 128

TM_IN = 1024
TM_POST = 512
POST_CHAINS = 1
TQ_NSA = 128
TQ_SELECT = 256
NSA_CHAINS = 2
SEL_CHUNK = 512
TQ_SB = 256
TK_SB = 256
FF_CHUNK = 256
SB_DEAD_LOG2 = -150.0
SB_FIRST_TILES = 3

VMEM_LIMIT = 56 * 1024 * 1024


def _cparams(*sem):
    return pltpu.CompilerParams(dimension_semantics=sem, vmem_limit_bytes=VMEM_LIMIT)


def _resident(shape):
    nd = len(shape)
    return pl.BlockSpec(shape, lambda *_: (0,) * nd, pipeline_mode=pl.Buffered(1))


def _dot(a, b):
    return jnp.dot(a, b, preferred_element_type=F32)


def _dot_nt(a, b):
    return lax.dot_general(a, b, (((1,), (1,)), ((), ())), preferred_element_type=F32)


def _split_bf16(v):
    hi = v.astype(BF16)
    lo = (v - hi.astype(F32)).astype(BF16)
    return hi, lo


def _rmsnorm(x, g):
    return x * lax.rsqrt(jnp.mean(x * x, axis=-1, keepdims=True) + RMS_EPS) * g


def _ada_kernel(c_ref, w_ref, b_ref, o_ref):
    c = c_ref[...]
    a = (c * jax.nn.sigmoid(c)).astype(BF16)
    o_ref[...] = _dot(a, w_ref[...].astype(BF16)) + b_ref[...]


def _ada(c, w, b):
    bsz, d = c.shape
    n = w.shape[1]
    tn = d
    return pl.pallas_call(
        _ada_kernel,
        grid=(n // tn,),
        in_specs=[pl.BlockSpec((bsz, d), lambda j: (0, 0)),
                  pl.BlockSpec((d, tn), lambda j: (0, j)),
                  pl.BlockSpec((1, tn), lambda j: (0, j))],
        out_specs=pl.BlockSpec((bsz, tn), lambda j: (0, j)),
        out_shape=jax.ShapeDtypeStruct((bsz, n), F32),
        compiler_params=_cparams("arbitrary"),
        name="ada",
    )(c, w, b.reshape(1, n))


def _in_proj_kernel(x_ref, mod_ref, g_ref, w_ref, qa_ref, kva_ref, qkvb_ref, mg_ref, ga_ref, *, splits):
    x = x_ref[0]
    h = _rmsnorm(x, g_ref[...]) * (1.0 + mod_ref[0, 1:2, :]) + mod_ref[0, 0:1, :]
    hb = h.astype(BF16)
    n_q, n_kv, n_g, n_b, n_m = splits
    col = 0
    for ref, width, act in ((qa_ref, n_q, None), (kva_ref, n_kv, None), (ga_ref, n_g, "sig"),
                            (qkvb_ref, n_b, None), (mg_ref, n_m, "sig")):
        for c0 in range(0, width, 512):
            cw = min(512, width - c0)
            r = _dot(hb, w_ref[:, col + c0:col + c0 + cw])
            if act == "sig":
                r = jax.nn.sigmoid(r)
            ref[0, :, c0:c0 + cw] = r.astype(ref.dtype)
        col += width


def _in_proj(x, mod3, g, w_all, splits):
    bsz, s, d = x.shape
    n_q, n_kv, n_g, n_b, n_m = splits
    tm = TM_IN
    out_shape = (jax.ShapeDtypeStruct((bsz, s, n_q), BF16),
                 jax.ShapeDtypeStruct((bsz, s, n_kv), BF16),
                 jax.ShapeDtypeStruct((bsz, s, n_b), BF16),
                 jax.ShapeDtypeStruct((bsz, s, n_m), BF16),
                 jax.ShapeDtypeStruct((bsz, s, n_g), F32))
    tok = lambda n: pl.BlockSpec((1, tm, n), lambda b, i: (b, i, 0))
    return pl.pallas_call(
        functools.partial(_in_proj_kernel, splits=splits),
        grid=(bsz, s // tm),
        in_specs=[tok(d),
                  pl.BlockSpec((1, 6, d), lambda b, i: (b, 0, 0)),
                  pl.BlockSpec((1, d), lambda b, i: (0, 0)),
                  _resident(w_all.shape)],
        out_specs=(tok(n_q), tok(n_kv), tok(n_b), tok(n_m), tok(n_g)),
        out_shape=out_shape,
        compiler_params=_cparams("parallel", "parallel"),
        name="in_proj",
    )(x, mod3, g.reshape(1, d), w_all)


def _gelu_tanh(x):
    return 0.5 * x * (1.0 + jnp.tanh(math.sqrt(2.0 / math.pi) * (x + 0.044715 * (x * x * x))))


def _compress_kernel(k16_ref, v16_ref, pk_ref, pv_ref, w1k_ref, w1v_ref, w2k_ref, w2v_ref, kc_ref, vc_ref):
    nc = k16_ref.shape[1]
    half = w1k_ref.shape[0] // 2

    def one(x_ref, p_ref, w1_ref, w2_ref, o_ref):
        xin = x_ref[0].astype(F32)
        top = (xin + p_ref[0:1, :]).astype(BF16)
        bot = (xin + p_ref[1:2, :]).astype(BF16)
        a = _dot(top, w1_ref[0:half, :])
        b = _dot(bot, w1_ref[half:, :])
        hid = a + pltpu.roll(b, nc - 1, 0)
        o_ref[0] = _dot(_gelu_tanh(hid).astype(BF16), w2_ref[...]).astype(o_ref.dtype)

    one(k16_ref, pk_ref, w1k_ref, w2k_ref, kc_ref)
    one(v16_ref, pv_ref, w1v_ref, w2v_ref, vc_ref)


def _compress(k16, v16, pk, pv, w1k, w1v, w2k, w2v):
    bsz, nc, wide = k16.shape
    blk = pl.BlockSpec((1, nc, wide), lambda b: (b, 0, 0))
    out = pl.BlockSpec((1, nc, LANES), lambda b: (b, 0, 0))
    full = lambda a: pl.BlockSpec(a.shape, lambda b: (0,) * a.ndim)
    return pl.pallas_call(
        _compress_kernel,
        grid=(bsz,),
        in_specs=[blk, blk, full(pk), full(pv), full(w1k), full(w1v), full(w2k), full(w2v)],
        out_specs=(out, out),
        out_shape=(jax.ShapeDtypeStruct((bsz, nc, LANES), BF16),) * 2,
        compiler_params=_cparams("parallel"),
        name="compress",
    )(k16, v16, pk, pv, w1k, w1v, w2k, w2v)


def _build_qaug(qa_ref, qx_ref, qaug_ref, tq):
    lo = lax.broadcasted_iota(jnp.int32, (tq, LANES), 1) < HEAD_DIM
    for g in range(NSA_GROUPS):
        for r in range(NSA_REP):
            h = g * NSA_REP + r
            qr = qa_ref[0, :, r * LANES:(r + 1) * LANES].astype(F32)
            qm = jnp.where(lo if g == 0 else jnp.logical_not(lo), qr, 0.0)
            qaug_ref[h * tq:(h + 1) * tq, 0:LANES] = qm.astype(BF16)
            qaug_ref[h * tq:(h + 1) * tq, LANES:2 * LANES] = jnp.broadcast_to(qx_ref[h:h + 1, :], (tq, LANES)).astype(BF16)


def _nsa_select_kernel(qa_ref, kc_ref, kxc_ref, ovl_ref, qx_ref, bias_ref, qaug_ref):
    tq = TQ_SELECT
    rows = NSA_HEADS * tq
    q0 = pl.program_id(1) * tq
    nc = kc_ref.shape[1]
    n_cmp = nc - 1
    _build_qaug(qa_ref, qx_ref, qaug_ref, tq)
    kc_aug = jnp.concatenate([kc_ref[0], kxc_ref[...]], axis=1)
    st = _dot_nt(kc_aug, qaug_ref[...])
    n_s = lax.broadcasted_iota(jnp.int32, (nc, rows), 0)
    t_s = q0 + (lax.broadcasted_iota(jnp.int32, (nc, rows), 1) & (tq - 1))
    sm = jnp.where((t_s >= n_s * CMP_STRIDE + (CMP_BLOCK - 1)) & (n_s < n_cmp), st, NEG_INF)
    e = jnp.exp2(sm - jnp.max(sm, axis=0, keepdims=True))
    p_t = jnp.where(t_s >= CMP_BLOCK - 1, e / jnp.sum(e, axis=0, keepdims=True), 0.0)
    nsel_rows = 32
    j_s = lax.broadcasted_iota(jnp.int32, (nsel_rows, tq), 0)
    cur = (q0 + lax.broadcasted_iota(jnp.int32, (nsel_rows, tq), 1)) // SEL_BLOCK
    valid_j = j_s <= cur
    forced = ((j_s == 0) | (j_s == cur) | (j_s == cur - 1)) & valid_j
    for g in range(NSA_GROUPS):
        base = g * NSA_REP * tq
        psum = (p_t[:, base:base + tq] + p_t[:, base + tq:base + 2 * tq]
                + p_t[:, base + 2 * tq:base + 3 * tq] + p_t[:, base + 3 * tq:base + 4 * tq])
        p_hi, p_lo = _split_bf16(psum)
        p_slc = _dot(ovl_ref[...], p_hi) + _dot(ovl_ref[...], p_lo)
        prio = p_slc[SEL_LANE0:SEL_LANE0 + nsel_rows]
        prio = jnp.where(valid_j, prio, NEG_INF)
        prio = jnp.where(forced, FORCE_PRIORITY, prio)
        chosen = jnp.zeros((nsel_rows, tq), jnp.bool_)
        for _ in range(SEL_TOPK):
            best = jnp.max(prio, axis=0, keepdims=True)
            first = jnp.min(jnp.where(prio == best, j_s, nsel_rows), axis=0, keepdims=True)
            pick = j_s == first
            chosen = chosen | pick
            prio = jnp.where(pick, -3e38, prio)
        bias_t = jnp.where(chosen, 0.0, NEG_INF)
        bias_t = jnp.concatenate([jnp.zeros((SEL_LANE0, tq), F32), bias_t,
                                  jnp.zeros((LANES - SEL_LANE0 - nsel_rows, tq), F32)], axis=0)
        for c0 in range(0, tq, LANES):
            bias_ref[0, c0:c0 + LANES, g * LANES:(g + 1) * LANES] = bias_t[:, c0:c0 + LANES].T.astype(BF16)


def _nsa_select(qa, kc, kxc, ovl, qx):
    bsz, s, nq = qa.shape
    nc = kc.shape[1]
    tq = TQ_SELECT
    const = lambda a: pl.BlockSpec(a.shape, lambda b, q: (0,) * a.ndim)
    return pl.pallas_call(
        _nsa_select_kernel,
        grid=(bsz, s // tq),
        in_specs=[pl.BlockSpec((1, tq, nq), lambda b, q: (b, q, 0)),
                  pl.BlockSpec((1, nc, LANES), lambda b, q: (b, 0, 0)),
                  const(kxc), const(ovl), const(qx)],
        out_specs=pl.BlockSpec((1, tq, NSA_GROUPS * LANES), lambda b, q: (b, q, 0)),
        out_shape=jax.ShapeDtypeStruct((bsz, s, NSA_GROUPS * LANES), BF16),
        scratch_shapes=[pltpu.VMEM((NSA_HEADS * tq, 2 * LANES), BF16)],
        compiler_params=_cparams("parallel", "parallel"),
        name="nsa_select",
    )(qa, kc, kxc, ovl, qx)


def _nsa_kernel(qa_ref, kc_ref, vc_ref, ksel_ref, vsel_ref, kwin_ref, vwin_ref, kx_ref, kxc_ref, qx_ref, bias_ref,
                ga_ref, o_ref, qaug_ref, qsel_ref, oacc_ref):
    tq = TQ_NSA
    nh = NSA_HEADS
    rows = nh * tq
    qi = pl.program_id(1)
    q0 = pl.multiple_of(qi * tq, tq)
    nc = kc_ref.shape[1]
    n_cmp = nc - 1

    lane = lax.broadcasted_iota(jnp.int32, (tq, LANES), 1)
    row = lax.broadcasted_iota(jnp.int32, (tq, LANES), 0)
    lo = lane < HEAD_DIM

    _build_qaug(qa_ref, qx_ref, qaug_ref, tq)
    for h in range(nh):
        g = h // NSA_REP
        qsel_ref[h * tq:(h + 1) * tq, 0:LANES] = qaug_ref[h * tq:(h + 1) * tq, 0:LANES]
        qsel_ref[h * tq:(h + 1) * tq, LANES:2 * LANES] = (
            bias_ref[0, :, g * LANES:(g + 1) * LANES].astype(F32) + qx_ref[h:h + 1, :]).astype(BF16)

    ga = ga_ref[0]

    def emit(o_heads, branch, first):
        for r in range(NSA_REP):
            c0 = r * 3 + branch
            c1 = NSA_REP * 3 + r * 3 + branch
            gt = jnp.where(lo, ga[:, c0:c0 + 1], ga[:, c1:c1 + 1])
            val = gt * jnp.where(lo, o_heads[r * tq:(r + 1) * tq], o_heads[(NSA_REP + r) * tq:(NSA_REP + r + 1) * tq])
            if first:
                oacc_ref[:, r * LANES:(r + 1) * LANES] = val
            else:
                oacc_ref[:, r * LANES:(r + 1) * LANES] += val

    def attend(q, k_aug, v, mask_fn):
        w = k_aug.shape[0]
        v_aug = jnp.concatenate([v, jnp.ones((w, LANES), BF16)], axis=1)
        part = rows // NSA_CHAINS
        outs = []
        for c in range(NSA_CHAINS):
            s3 = mask_fn(_dot_nt(q[c * part:(c + 1) * part], k_aug).reshape(part // tq, tq, w))
            p = jnp.exp2(s3 - jnp.max(s3, axis=-1, keepdims=True)).astype(BF16).reshape(part, w)
            pv = _dot(p, v_aug)
            outs.append(pv[:, 0:LANES] / pv[:, LANES:2 * LANES])
        return jnp.concatenate(outs, axis=0)

    qaug = qaug_ref[...]

    qsel = qsel_ref[...]
    last = q0 // SEL_CHUNK

    def branches(n_full):
        kc_aug = jnp.concatenate([kc_ref[0], kxc_ref[...]], axis=1)
        n_l = lax.broadcasted_iota(jnp.int32, (tq, nc), 1)
        t_l = q0 + lax.broadcasted_iota(jnp.int32, (tq, nc), 0)
        valid = (t_l >= n_l * CMP_STRIDE + (CMP_BLOCK - 1)) & (n_l < n_cmp)
        o_cmp = attend(qaug, kc_aug, vc_ref[0], lambda s3: jnp.where(valid[None], s3, NEG_INF))
        has_cmp = jnp.concatenate([q0 + row >= CMP_BLOCK - 1] * nh, axis=0)
        emit(jnp.where(has_cmp, o_cmp, 0.0), 0, True)

        span = WINDOW + tq
        w_start = pl.multiple_of(jnp.maximum(q0 - WINDOW, 0), tq)
        k_aug = jnp.concatenate([kwin_ref[0, pl.ds(w_start, span), :], kx_ref[pl.ds(w_start, span), :]], axis=1)
        dist = (q0 + lax.broadcasted_iota(jnp.int32, (tq, span), 0)) - (w_start + lax.broadcasted_iota(jnp.int32, (tq, span), 1))
        band = (dist >= 0) & (dist < WINDOW)
        emit(attend(qaug, k_aug, vwin_ref[0, pl.ds(w_start, span), :], lambda s3: jnp.where(band[None], s3, NEG_INF)), 2, False)

        nk = (n_full + 1) * SEL_CHUNK
        k_aug = jnp.concatenate([ksel_ref[0, 0:nk, :], kx_ref[0:nk, :]], axis=1)
        kpos = n_full * SEL_CHUNK + lax.broadcasted_iota(jnp.int32, (tq, SEL_CHUNK), 1)
        tpos = q0 + lax.broadcasted_iota(jnp.int32, (tq, SEL_CHUNK), 0)
        causal = (kpos <= tpos)[None]

        def mask_fn(s3):
            diag = jnp.where(causal, s3[:, :, n_full * SEL_CHUNK:nk], NEG_INF)
            return jnp.concatenate([s3[:, :, 0:n_full * SEL_CHUNK], diag], axis=2) if n_full else diag

        emit(attend(qsel, k_aug, vsel_ref[0, 0:nk, :], mask_fn), 1, False)

    for n_full in range(ksel_ref.shape[1] // SEL_CHUNK):
        pl.when(last == n_full)(functools.partial(branches, n_full))

    o_ref[0] = oacc_ref[...].astype(o_ref.dtype)


def _nsa(qa, kva, kc, vc, ga, kx, kxc, qx, selbias):
    bsz, s, nq = qa.shape
    nc = kc.shape[1]
    tq = TQ_NSA
    rows = NSA_HEADS * tq
    kv_blk = lambda i: pl.BlockSpec((1, s, LANES), lambda b, q, i=i: (b, 0, i))
    cmp_blk = pl.BlockSpec((1, nc, LANES), lambda b, q: (b, 0, 0))
    const = lambda a: pl.BlockSpec(a.shape, lambda b, q: (0,) * a.ndim)
    return pl.pallas_call(
        _nsa_kernel,
        grid=(bsz, s // tq),
        in_specs=[pl.BlockSpec((1, tq, nq), lambda b, q: (b, q, 0)),
                  cmp_blk, cmp_blk, kv_blk(2), kv_blk(3), kv_blk(4), kv_blk(5),
                  const(kx), const(kxc), const(qx),
                  pl.BlockSpec((1, tq, NSA_GROUPS * LANES), lambda b, q: (b, q, 0)),
                  pl.BlockSpec((1, tq, GATE_PAD), lambda b, q: (b, q, 0))],
        out_specs=pl.BlockSpec((1, tq, nq), lambda b, q: (b, q, 0)),
        out_shape=jax.ShapeDtypeStruct((bsz, s, nq), BF16),
        scratch_shapes=[pltpu.VMEM((rows, 2 * LANES), BF16),
                        pltpu.VMEM((rows, 2 * LANES), BF16),
                        pltpu.VMEM((tq, nq), F32)],
        compiler_params=_cparams("parallel", "arbitrary"),
        name="nsa",
    )(qa, kc, vc, kva, kva, kva, kva, kx, kxc, qx, selbias, ga)


def _sb_kernel(q_ref, k_ref, v_ref, tri_ref, o_ref, carry_ref, acc_ref):
    tq, tk = TQ_SB, TK_SB
    qi = pl.program_id(2)
    lane = lax.broadcasted_iota(jnp.int32, (tq, LANES), 1)
    lo = lane < HEAD_DIM
    q2 = q_ref[0].astype(F32)
    qs = jnp.concatenate([jnp.where(lo, q2, 0.0), jnp.where(lo, 0.0, q2)], axis=0).astype(BF16)
    ntri = tri_ref[0:tk, :]

    def stick(z):
        t = jnp.log2(1.0 + jnp.exp2(-jnp.abs(z)))
        return jnp.maximum(z, 0.0) + t, jnp.minimum(z, 0.0) - t

    kpos = lax.broadcasted_iota(jnp.int32, (tq, tk), 1)
    tpos = lax.broadcasted_iota(jnp.int32, (tq, tk), 0)
    keep = jnp.concatenate([kpos < tpos] * 2, axis=0)

    def first_block(nt):
        w = nt * tk
        d0 = w - tk
        k_start = pl.multiple_of((qi - (nt - 1)) * tk, tk)
        sp, ls = stick(_dot_nt(qs, k_ref[0, pl.ds(k_start, w), :]))
        sp_d = jnp.where(keep, sp[:, d0:w], 0.0)
        sp_b = (jnp.concatenate([sp[:, 0:d0], sp_d], axis=1) if nt > 1 else sp_d).astype(BF16)
        a_parts = []
        for i in range(nt):
            later = _dot(sp_b[:, i * tk:w], tri_ref[0:w - i * tk, :])
            a = jnp.exp2(ls[:, i * tk:(i + 1) * tk] + later)
            a_parts.append(jnp.where(keep, a, 0.0) if i == nt - 1 else a)
            if i == 0:
                carry_ref[...] = later[:, 0:1] - sp_b[:, 0:1].astype(F32)
        a = (jnp.concatenate(a_parts, axis=1) if nt > 1 else a_parts[0]).astype(BF16)
        acc_ref[...] = _dot(a, v_ref[0, pl.ds(k_start, w), :])

    for nt in range(1, SB_FIRST_TILES):
        pl.when(qi == nt - 1)(functools.partial(first_block, nt))

    @pl.when(qi >= SB_FIRST_TILES - 1)
    def _():
        first_block(SB_FIRST_TILES)

        def tile(kt):
            k0 = pl.multiple_of(kt * tk, tk)
            sp, ls = stick(_dot_nt(qs, k_ref[0, pl.ds(k0, tk), :]))
            later = _dot(sp.astype(BF16), ntri)
            pv = _dot(jnp.exp2(ls + later).astype(BF16), v_ref[0, pl.ds(k0, tk), :])
            carry = carry_ref[...]
            acc_ref[...] += jnp.exp2(carry) * pv
            carry_ref[...] = carry - jnp.sum(sp, axis=-1, keepdims=True)

        def live():
            return jnp.max(carry_ref[...]) > SB_DEAD_LOG2

        def cond(st):
            return (st[0] >= 0) & st[1]

        def body(st):
            tile(st[0])
            return st[0] - 1, live()

        lax.while_loop(cond, body, (qi - SB_FIRST_TILES, live()))

    o_ref[0] = jnp.where(lo, acc_ref[0:tq], acc_ref[tq:2 * tq]).astype(o_ref.dtype)


def _sb(qkvb, tri):
    bsz, s, n3 = qkvb.shape
    npair = SB_HEADS // 2
    tq = TQ_SB
    return pl.pallas_call(
        _sb_kernel,
        grid=(bsz, npair, s // tq),
        in_specs=[pl.BlockSpec((1, tq, LANES), lambda b, p, q: (b, q, p)),
                  pl.BlockSpec((1, s, LANES), lambda b, p, q: (b, 0, npair + p)),
                  pl.BlockSpec((1, s, LANES), lambda b, p, q: (b, 0, 2 * npair + p)),
                  pl.BlockSpec(tri.shape, lambda b, p, q: (0, 0))],
        out_specs=pl.BlockSpec((1, tq, LANES), lambda b, p, q: (b, q, p)),
        out_shape=jax.ShapeDtypeStruct((bsz, s, npair * LANES), BF16),
        scratch_shapes=[pltpu.VMEM((2 * tq, 1), F32),
                        pltpu.VMEM((2 * tq, LANES), F32)],
        compiler_params=_cparams("parallel", "parallel", "arbitrary"),
        name="sb",
    )(qkvb, qkvb, qkvb, tri)


def _post_kernel(x_ref, oa_ref, ob_ref, mg_ref, mod_ref, wpa_ref, wpb_ref, wo_ref, g2_ref,
                 wg_ref, wu_ref, wd_ref, gf_ref, out_ref):
    d = x_ref.shape[2]
    part_rows = x_ref.shape[1] // POST_CHAINS
    for i in range(POST_CHAINS):
        rs = slice(i * part_rows, (i + 1) * part_rows)
        x = x_ref[0, rs, :]
        ya = _dot(oa_ref[0, rs, :], wpa_ref[...])
        yb = _dot(ob_ref[0, rs, :], wpb_ref[...])
        y = mg_ref[0, rs, 0:d].astype(F32) * ya + mg_ref[0, rs, d:2 * d].astype(F32) * yb
        x1 = x + mod_ref[0, 2:3, :] * _dot(y.astype(BF16), wo_ref[...])
        h2 = (_rmsnorm(x1, g2_ref[...]) * (1.0 + mod_ref[0, 4:5, :]) + mod_ref[0, 3:4, :]).astype(BF16)
        ffn = None
        for c0 in range(0, wg_ref.shape[1], FF_CHUNK):
            gte = _dot(h2, wg_ref[:, c0:c0 + FF_CHUNK])
            up = _dot(h2, wu_ref[:, c0:c0 + FF_CHUNK])
            act = (gte * jax.nn.sigmoid(gte) * up).astype(BF16)
            part = _dot(act, wd_ref[c0:c0 + FF_CHUNK, :])
            ffn = part if ffn is None else ffn + part
        x2 = x1 + mod_ref[0, 5:6, :] * ffn
        out_ref[0, rs, :] = _rmsnorm(x2, gf_ref[...])


def _post(x, oa, ob, mg, mod3, wpa, wpb, wo, g2, wg3, wu3, wd3, gf):
    bsz, s, d = x.shape
    tm = TM_POST
    tok = lambda n: pl.BlockSpec((1, tm, n), lambda b, i: (b, i, 0))
    return pl.pallas_call(
        _post_kernel,
        grid=(bsz, s // tm),
        in_specs=[tok(d), tok(oa.shape[2]), tok(ob.shape[2]), tok(mg.shape[2]),
                  pl.BlockSpec((1, 6, d), lambda b, i: (b, 0, 0)),
                  _resident(wpa.shape), _resident(wpb.shape), _resident(wo.shape),
                  pl.BlockSpec((1, d), lambda b, i: (0, 0)),
                  _resident(wg3.shape), _resident(wu3.shape), _resident(wd3.shape),
                  pl.BlockSpec((1, d), lambda b, i: (0, 0))],
        out_specs=tok(d),
        out_shape=jax.ShapeDtypeStruct((bsz, s, d), F32),
        compiler_params=_cparams("parallel", "parallel"),
        name="post",
    )(x, oa, ob, mg, mod3, wpa, wpb, wo, g2.reshape(1, d), wg3, wu3, wd3, gf.reshape(1, d))


def _bf16_pieces(x, n=3):
    out, rem = [], float(x)
    for _ in range(n):
        p = float(np.asarray(rem, np.float32).astype(jnp.bfloat16).astype(np.float32))
        out.append(p)
        rem -= p
    return out


N_PIECES = 3


def _key_extras(pos):
    lane = jnp.arange(LANES)[None, :]
    a = (pos // SEL_BLOCK)[:, None]
    b = (pos % SEL_BLOCK)[:, None]
    ext = jnp.where(lane < N_PIECES, a, jnp.where(lane < 2 * N_PIECES, b, 0)).astype(F32)
    return ext, a


def _query_extras():
    qx = np.zeros((NSA_HEADS, LANES), np.float32)
    for h in range(NSA_HEADS):
        slope = 2.0 ** (-(h + 1))
        for i, p in enumerate(_bf16_pieces(LOG2E, N_PIECES)):
            qx[h, i] = SEL_BLOCK * slope * p
            qx[h, N_PIECES + i] = slope * p
    return jnp.asarray(qx)


def kernel(x, c, w_ada, b_ada, norm_mix_g, w_in, cmp_pos_k, cmp_w1_k, cmp_w2_k, cmp_pos_v, cmp_w1_v, cmp_w2_v,
           w_proj_a, w_proj_b, w_out, norm_ffn_g, w_ffn_gate, w_ffn_up, w_ffn_down, norm_final_g):
    bsz, s, d = x.shape
    depth = w_ada.shape[0]
    assert s % SEL_CHUNK == 0 and s // SEL_BLOCK <= 32 and s % TM_IN == 0 and s >= WINDOW + TQ_NSA
    n_q = NSA_HEADS * HEAD_DIM
    n_kv = 6 * NSA_GROUPS * HEAD_DIM
    n_gate = 3 * NSA_HEADS
    n_b = 3 * SB_HEADS * HEAD_DIM
    n_m = 2 * d
    nc = s // CMP_STRIDE
    d_ff = w_ffn_gate.shape[2]
    assert d_ff % FF_CHUNK == 0

    pos = jnp.arange(s)
    ext, blk = _key_extras(pos)
    lane = jnp.arange(LANES)[None, :]
    kx = jnp.where(lane == SEL_LANE0 + blk, 1.0, ext).astype(BF16)
    kxc = _key_extras(jnp.arange(nc) * CMP_STRIDE + CMP_BLOCK - 1)[0].astype(BF16)
    jrow = jnp.arange(LANES)[:, None] - SEL_LANE0
    ncol = jnp.arange(nc)[None, :]
    ovl = ((ncol * CMP_STRIDE <= jrow * SEL_BLOCK + SEL_BLOCK - 1)
           & (ncol * CMP_STRIDE + CMP_BLOCK - 1 >= jrow * SEL_BLOCK)
           & (jrow >= 0) & (jrow < s // SEL_BLOCK) & (ncol < nc - 1)).astype(BF16)
    qx = _query_extras()
    jj = jnp.arange(TK_SB)
    tri = -jnp.concatenate([jj[:, None] > jj[None, :]] + [jnp.ones((TK_SB, TK_SB), jnp.bool_)] * (SB_FIRST_TILES - 1),
                           axis=0).astype(BF16)

    for l in range(depth):
        w = w_in[l]
        wq = (w[:, :n_q] * (ATTN_SCALE * LOG2E)).reshape(d, NSA_GROUPS, NSA_REP, HEAD_DIM).transpose(0, 2, 1, 3).reshape(d, n_q)
        o1 = n_q + n_kv
        o2 = o1 + n_gate
        o3 = o2 + n_b
        wgate = jnp.pad(w[:, o1:o2], ((0, 0), (0, GATE_PAD - n_gate)))
        wb = jnp.concatenate([w[:, o2:o2 + n_b // 3] * (ATTN_SCALE * LOG2E), w[:, o2 + n_b // 3:o3]], axis=1)
        w_all = jnp.concatenate([wq, w[:, n_q:o1], wgate, wb, w[:, o3:]], axis=1).astype(BF16)
        splits = (n_q, n_kv, GATE_PAD, n_b, n_m)

        def cmp_weights(pos_emb, w1, w2):
            half = CMP_BLOCK // 2
            w1r = w1.reshape(2, half, HEAD_DIM, CMP_HIDDEN)
            z = jnp.zeros_like(w1r)
            g0 = jnp.concatenate([w1r, z], axis=3)
            g1 = jnp.concatenate([z, w1r], axis=3)
            w1p = jnp.stack([g0, g1], axis=2).reshape(2 * half * 2 * HEAD_DIM, 2 * CMP_HIDDEN)
            zz = jnp.zeros_like(w2)
            w2p = jnp.concatenate([jnp.concatenate([w2, zz], axis=1), jnp.concatenate([zz, w2], axis=1)], axis=0)
            pe = pos_emb.reshape(2, half, 1, HEAD_DIM)
            pe = jnp.broadcast_to(pe, (2, half, 2, HEAD_DIM)).reshape(2, half * 2 * HEAD_DIM)
            return pe, w1p.astype(BF16), w2p.astype(BF16)

        pk, w1k, w2k = cmp_weights(cmp_pos_k[l], cmp_w1_k[l], cmp_w2_k[l])
        pv, w1v, w2v = cmp_weights(cmp_pos_v[l], cmp_w1_v[l], cmp_w2_v[l])
        wpa = w_proj_a[l].reshape(NSA_GROUPS, NSA_REP, HEAD_DIM, d).transpose(1, 0, 2, 3).reshape(n_q, d).astype(BF16)
        wpb = w_proj_b[l].astype(BF16)
        wo = w_out[l].astype(BF16)
        wg3 = w_ffn_gate[l].astype(BF16)
        wu3 = w_ffn_up[l].astype(BF16)
        wd3 = w_ffn_down[l].astype(BF16)

        mod3 = _ada(c, w_ada[l], b_ada[l]).reshape(bsz, 6, d)
        qa, kva, qkvb, mg, ga = _in_proj(x, mod3, norm_mix_g[l], w_all, splits)
        k16 = kva[:, :, 0:LANES].reshape(bsz, nc, CMP_STRIDE * LANES)
        v16 = kva[:, :, LANES:2 * LANES].reshape(bsz, nc, CMP_STRIDE * LANES)
        kc, vc = _compress(k16, v16, pk, pv, w1k, w1v, w2k, w2v)
        selbias = _nsa_select(qa, kc, kxc, ovl, qx)
        oa = _nsa(qa, kva, kc, vc, ga, kx, kxc, qx, selbias)
        ob = _sb(qkvb, tri)
        gf = norm_final_g if l == depth - 1 else jnp.ones_like(norm_final_g)
        x = _post(x, oa, ob, mg, mod3, wpa, wpb, wo, norm_ffn_g[l], wg3, wu3, wd3, gf)
        assert depth == 1, "final norm is fused into the last layer's post kernel"
    return x
```

```python
import functools
import math

import jax
import jax.numpy as jnp
import numpy as np
from jax import lax
from jax.experimental import pallas as pl
from jax.experimental.pallas import tpu as pltpu

F32 = jnp.float32
BF16 = jnp.bfloat16

HEAD_DIM = 64
NSA_HEADS = 8
NSA_GROUPS = 2
NSA_REP = NSA_HEADS // NSA_GROUPS
SB_HEADS = 8
CMP_BLOCK = 32
CMP_STRIDE = 16
CMP_HIDDEN = 2 * HEAD_DIM
SEL_BLOCK = 64
SEL_TOPK = 8
WINDOW = 512
RMS_EPS = 1e-6
NEG_INF = -1e30
FORCE_PRIORITY = 1e4
ATTN_SCALE = 1.0 / math.sqrt(HEAD_DIM)
LOG2E = math.log2(math.e)

LANES = 128
PAIR = 2 * HEAD_DIM
SEL_LANE0 = 32
GATE_PAD = 128

TM_IN = 1024
IN_CHUNK = 512
TM_POST = 512
POST_CHAINS = 1
TQ_NSA = 256
TQ_SELECT = 512
NSA_CHAINS = 4
SEL_CHUNK = 512
TQ_SB = 256
TK_SB = 256
FF_CHUNK = 256
SB_DEAD_LOG2 = -150.0
SB_FIRST_TILES = 3

VMEM_LIMIT = 56 * 1024 * 1024


def _cparams(*sem):
    return pltpu.CompilerParams(dimension_semantics=sem, vmem_limit_bytes=VMEM_LIMIT)


def _resident(shape):
    nd = len(shape)
    return pl.BlockSpec(shape, lambda *_: (0,) * nd, pipeline_mode=pl.Buffered(1))


def _dot(a, b):
    return jnp.dot(a, b, preferred_element_type=F32)


def _dot_nt(a, b):
    return lax.dot_general(a, b, (((1,), (1,)), ((), ())), preferred_element_type=F32)


def _split_bf16(v):
    hi = v.astype(BF16)
    lo = (v - hi.astype(F32)).astype(BF16)
    return hi, lo


def _rmsnorm(x, g):
    return x * lax.rsqrt(jnp.mean(x * x, axis=-1, keepdims=True) + RMS_EPS) * g


def _ada_kernel(c_ref, w_ref, b_ref, o_ref):
    c = c_ref[...]
    a = (c * jax.nn.sigmoid(c)).astype(BF16)
    o_ref[...] = _dot(a, w_ref[...].astype(BF16)) + b_ref[...]


def _ada(c, w, b):
    bsz, d = c.shape
    n = w.shape[1]
    tn = d
    return pl.pallas_call(
        _ada_kernel,
        grid=(n // tn,),
        in_specs=[pl.BlockSpec((bsz, d), lambda j: (0, 0)),
                  pl.BlockSpec((d, tn), lambda j: (0, j)),
                  pl.BlockSpec((1, tn), lambda j: (0, j))],
        out_specs=pl.BlockSpec((bsz, tn), lambda j: (0, j)),
        out_shape=jax.ShapeDtypeStruct((bsz, n), F32),
        compiler_params=_cparams("arbitrary"),
        name="ada",
    )(c, w, b.reshape(1, n))


def _in_proj_kernel(x_ref, mod_ref, g_ref, w_ref, qa_ref, kva_ref, qkvb_ref, mg_ref, ga_ref, *, splits):
    x = x_ref[0]
    h = _rmsnorm(x, g_ref[...]) * (1.0 + mod_ref[0, 1:2, :]) + mod_ref[0, 0:1, :]
    hb = h.astype(BF16)
    n_q, n_kv, n_g, n_b, n_m = splits
    col = 0
    for ref, width, act in ((qa_ref, n_q, None), (kva_ref, n_kv, None), (ga_ref, n_g, "sig"),
                            (qkvb_ref, n_b, None), (mg_ref, n_m, "sig")):
        for c0 in range(0, width, IN_CHUNK):
            cw = min(IN_CHUNK, width - c0)
            r = _dot(hb, w_ref[:, col + c0:col + c0 + cw])
            if act == "sig":
                r = jax.nn.sigmoid(r)
            ref[0, :, c0:c0 + cw] = r.astype(ref.dtype)
        col += width


def _in_proj(x, mod3, g, w_all, splits):
    bsz, s, d = x.shape
    n_q, n_kv, n_g, n_b, n_m = splits
    tm = TM_IN
    out_shape = (jax.ShapeDtypeStruct((bsz, s, n_q), BF16),
                 jax.ShapeDtypeStruct((bsz, s, n_kv), BF16),
                 jax.ShapeDtypeStruct((bsz, s, n_b), BF16),
                 jax.ShapeDtypeStruct((bsz, s, n_m), BF16),
                 jax.ShapeDtypeStruct((bsz, s, n_g), F32))
    tok = lambda n: pl.BlockSpec((1, tm, n), lambda b, i: (b, i, 0))
    return pl.pallas_call(
        functools.partial(_in_proj_kernel, splits=splits),
        grid=(bsz, s // tm),
        in_specs=[tok(d),
                  pl.BlockSpec((1, 6, d), lambda b, i: (b, 0, 0)),
                  pl.BlockSpec((1, d), lambda b, i: (0, 0)),
                  _resident(w_all.shape)],
        out_specs=(tok(n_q), tok(n_kv), tok(n_b), tok(n_m), tok(n_g)),
        out_shape=out_shape,
        compiler_params=_cparams("parallel", "parallel"),
        name="in_proj",
    )(x, mod3, g.reshape(1, d), w_all)


def _gelu_tanh(x):
    return 0.5 * x * (1.0 + jnp.tanh(math.sqrt(2.0 / math.pi) * (x + 0.044715 * (x * x * x))))


def _compress_kernel(k16_ref, v16_ref, pk_ref, pv_ref, w1k_ref, w1v_ref, w2k_ref, w2v_ref, kc_ref, vc_ref):
    nc = k16_ref.shape[1]
    half = w1k_ref.shape[0] // 2

    def one(x_ref, p_ref, w1_ref, w2_ref, o_ref):
        xin = x_ref[0].astype(F32)
        top = (xin + p_ref[0:1, :]).astype(BF16)
        bot = (xin + p_ref[1:2, :]).astype(BF16)
        a = _dot(top, w1_ref[0:half, :])
        b = _dot(bot, w1_ref[half:, :])
        hid = a + pltpu.roll(b, nc - 1, 0)
        o_ref[0] = _dot(_gelu_tanh(hid).astype(BF16), w2_ref[...]).astype(o_ref.dtype)

    one(k16_ref, pk_ref, w1k_ref, w2k_ref, kc_ref)
    one(v16_ref, pv_ref, w1v_ref, w2v_ref, vc_ref)


def _compress(k16, v16, pk, pv, w1k, w1v, w2k, w2v):
    bsz, nc, wide = k16.shape
    blk = pl.BlockSpec((1, nc, wide), lambda b: (b, 0, 0))
    out = pl.BlockSpec((1, nc, LANES), lambda b: (b, 0, 0))
    full = lambda a: pl.BlockSpec(a.shape, lambda b: (0,) * a.ndim)
    return pl.pallas_call(
        _compress_kernel,
        grid=(bsz,),
        in_specs=[blk, blk, full(pk), full(pv), full(w1k), full(w1v), full(w2k), full(w2v)],
        out_specs=(out, out),
        out_shape=(jax.ShapeDtypeStruct((bsz, nc, LANES), BF16),) * 2,
        compiler_params=_cparams("parallel"),
        name="compress",
    )(k16, v16, pk, pv, w1k, w1v, w2k, w2v)


def _build_qaug(qa_ref, qx_ref, qaug_ref, tq):
    lo = lax.broadcasted_iota(jnp.int32, (tq, LANES), 1) < HEAD_DIM
    for g in range(NSA_GROUPS):
        for r in range(NSA_REP):
            h = g * NSA_REP + r
            qr = qa_ref[0, :, r * LANES:(r + 1) * LANES].astype(F32)
            qm = jnp.where(lo if g == 0 else jnp.logical_not(lo), qr, 0.0)
            qaug_ref[h * tq:(h + 1) * tq, 0:LANES] = qm.astype(BF16)
            qaug_ref[h * tq:(h + 1) * tq, LANES:2 * LANES] = jnp.broadcast_to(qx_ref[h:h + 1, :], (tq, LANES)).astype(BF16)


def _nsa_select_kernel(qa_ref, kc_ref, kxc_ref, ovl_ref, qx_ref, bias_ref, qaug_ref):
    tq = TQ_SELECT
    rows = NSA_HEADS * tq
    q0 = pl.program_id(1) * tq
    nc = kc_ref.shape[1]
    n_cmp = nc - 1
    _build_qaug(qa_ref, qx_ref, qaug_ref, tq)
    kc_aug = jnp.concatenate([kc_ref[0], kxc_ref[...]], axis=1)
    st = _dot_nt(kc_aug, qaug_ref[...])
    n_s = lax.broadcasted_iota(jnp.int32, (nc, rows), 0)
    t_s = q0 + (lax.broadcasted_iota(jnp.int32, (nc, rows), 1) & (tq - 1))
    sm = jnp.where((t_s >= n_s * CMP_STRIDE + (CMP_BLOCK - 1)) & (n_s < n_cmp), st, NEG_INF)
    e = jnp.exp2(sm - jnp.max(sm, axis=0, keepdims=True))
    p_t = jnp.where(t_s >= CMP_BLOCK - 1, e / jnp.sum(e, axis=0, keepdims=True), 0.0)
    nsel_rows = 32
    j_s = lax.broadcasted_iota(jnp.int32, (nsel_rows, tq), 0)
    cur = (q0 + lax.broadcasted_iota(jnp.int32, (nsel_rows, tq), 1)) // SEL_BLOCK
    valid_j = j_s <= cur
    forced = ((j_s == 0) | (j_s == cur) | (j_s == cur - 1)) & valid_j
    for g in range(NSA_GROUPS):
        base = g * NSA_REP * tq
        psum = (p_t[:, base:base + tq] + p_t[:, base + tq:base + 2 * tq]
                + p_t[:, base + 2 * tq:base + 3 * tq] + p_t[:, base + 3 * tq:base + 4 * tq])
        p_hi, p_lo = _split_bf16(psum)
        p_slc = _dot(ovl_ref[...], p_hi) + _dot(ovl_ref[...], p_lo)
        prio = p_slc[SEL_LANE0:SEL_LANE0 + nsel_rows]
        prio = jnp.where(valid_j, prio, NEG_INF)
        prio = jnp.where(forced, FORCE_PRIORITY, prio)
        chosen = jnp.zeros((nsel_rows, tq), jnp.bool_)
        for _ in range(SEL_TOPK):
            best = jnp.max(prio, axis=0, keepdims=True)
            first = jnp.min(jnp.where(prio == best, j_s, nsel_rows), axis=0, keepdims=True)
            pick = j_s == first
            chosen = chosen | pick
            prio = jnp.where(pick, -3e38, prio)
        bias_t = jnp.where(chosen, 0.0, NEG_INF)
        bias_t = jnp.concatenate([jnp.zeros((SEL_LANE0, tq), F32), bias_t,
                                  jnp.zeros((LANES - SEL_LANE0 - nsel_rows, tq), F32)], axis=0)
        for c0 in range(0, tq, LANES):
            bias_ref[0, c0:c0 + LANES, g * LANES:(g + 1) * LANES] = bias_t[:, c0:c0 + LANES].T.astype(BF16)


def _nsa_select(qa, kc, kxc, ovl, qx):
    bsz, s, nq = qa.shape
    nc = kc.shape[1]
    tq = TQ_SELECT
    const = lambda a: pl.BlockSpec(a.shape, lambda b, q: (0,) * a.ndim)
    return pl.pallas_call(
        _nsa_select_kernel,
        grid=(bsz, s // tq),
        in_specs=[pl.BlockSpec((1, tq, nq), lambda b, q: (b, q, 0)),
                  pl.BlockSpec((1, nc, LANES), lambda b, q: (b, 0, 0)),
                  const(kxc), const(ovl), const(qx)],
        out_specs=pl.BlockSpec((1, tq, NSA_GROUPS * LANES), lambda b, q: (b, q, 0)),
        out_shape=jax.ShapeDtypeStruct((bsz, s, NSA_GROUPS * LANES), BF16),
        scratch_shapes=[pltpu.VMEM((NSA_HEADS * tq, 2 * LANES), BF16)],
        compiler_params=_cparams("parallel", "parallel"),
        name="nsa_select",
    )(qa, kc, kxc, ovl, qx)


def _nsa_kernel(qa_ref, kc_ref, vc_ref, ksel_ref, vsel_ref, kwin_ref, vwin_ref, kx_ref, kxc_ref, qx_ref, bias_ref,
                ga_ref, o_ref, qaug_ref, qsel_ref, oacc_ref):
    tq = TQ_NSA
    nh = NSA_HEADS
    rows = nh * tq
    qi = pl.program_id(1)
    q0 = pl.multiple_of(qi * tq, tq)
    nc = kc_ref.shape[1]
    n_cmp = nc - 1

    lane = lax.broadcasted_iota(jnp.int32, (tq, LANES), 1)
    row = lax.broadcasted_iota(jnp.int32, (tq, LANES), 0)
    lo = lane < HEAD_DIM

    _build_qaug(qa_ref, qx_ref, qaug_ref, tq)
    for h in range(nh):
        g = h // NSA_REP
        qsel_ref[h * tq:(h + 1) * tq, 0:LANES] = qaug_ref[h * tq:(h + 1) * tq, 0:LANES]
        qsel_ref[h * tq:(h + 1) * tq, LANES:2 * LANES] = (
            bias_ref[0, :, g * LANES:(g + 1) * LANES].astype(F32) + qx_ref[h:h + 1, :]).astype(BF16)

    ga = ga_ref[0]

    def emit(o_heads, branch, first):
        for r in range(NSA_REP):
            c0 = r * 3 + branch
            c1 = NSA_REP * 3 + r * 3 + branch
            gt = jnp.where(lo, ga[:, c0:c0 + 1], ga[:, c1:c1 + 1])
            val = gt * jnp.where(lo, o_heads[r * tq:(r + 1) * tq], o_heads[(NSA_REP + r) * tq:(NSA_REP + r + 1) * tq])
            if first:
                oacc_ref[:, r * LANES:(r + 1) * LANES] = val
            else:
                oacc_ref[:, r * LANES:(r + 1) * LANES] += val

    def attend(q, k_aug, v, mask_fn):
        w = k_aug.shape[0]
        v_aug = jnp.concatenate([v, jnp.ones((w, LANES), BF16)], axis=1)
        part = rows // NSA_CHAINS
        outs = []
        for c in range(NSA_CHAINS):
            s3 = mask_fn(_dot_nt(q[c * part:(c + 1) * part], k_aug).reshape(part // tq, tq, w))
            p = jnp.exp2(s3 - jnp.max(s3, axis=-1, keepdims=True)).astype(BF16).reshape(part, w)
            pv = _dot(p, v_aug)
            outs.append(pv[:, 0:LANES] / pv[:, LANES:2 * LANES])
        return jnp.concatenate(outs, axis=0)

    qaug = qaug_ref[...]

    qsel = qsel_ref[...]
    last = q0 // SEL_CHUNK

    def branches(n_full):
        kc_aug = jnp.concatenate([kc_ref[0], kxc_ref[...]], axis=1)
        n_l = lax.broadcasted_iota(jnp.int32, (tq, nc), 1)
        t_l = q0 + lax.broadcasted_iota(jnp.int32, (tq, nc), 0)
        valid = (t_l >= n_l * CMP_STRIDE + (CMP_BLOCK - 1)) & (n_l < n_cmp)
        o_cmp = attend(qaug, kc_aug, vc_ref[0], lambda s3: jnp.where(valid[None], s3, NEG_INF))
        has_cmp = jnp.concatenate([q0 + row >= CMP_BLOCK - 1] * nh, axis=0)
        emit(jnp.where(has_cmp, o_cmp, 0.0), 0, True)

        span = WINDOW + tq
        w_start = pl.multiple_of(jnp.maximum(q0 - WINDOW, 0), tq)
        k_aug = jnp.concatenate([kwin_ref[0, pl.ds(w_start, span), :], kx_ref[pl.ds(w_start, span), :]], axis=1)
        dist = (q0 + lax.broadcasted_iota(jnp.int32, (tq, span), 0)) - (w_start + lax.broadcasted_iota(jnp.int32, (tq, span), 1))
        band = (dist >= 0) & (dist < WINDOW)
        emit(attend(qaug, k_aug, vwin_ref[0, pl.ds(w_start, span), :], lambda s3: jnp.where(band[None], s3, NEG_INF)), 2, False)

        nk = (n_full + 1) * SEL_CHUNK
        k_aug = jnp.concatenate([ksel_ref[0, 0:nk, :], kx_ref[0:nk, :]], axis=1)
        kpos = n_full * SEL_CHUNK + lax.broadcasted_iota(jnp.int32, (tq, SEL_CHUNK), 1)
        tpos = q0 + lax.broadcasted_iota(jnp.int32, (tq, SEL_CHUNK), 0)
        causal = (kpos <= tpos)[None]

        def mask_fn(s3):
            diag = jnp.where(causal, s3[:, :, n_full * SEL_CHUNK:nk], NEG_INF)
            return jnp.concatenate([s3[:, :, 0:n_full * SEL_CHUNK], diag], axis=2) if n_full else diag

        emit(attend(qsel, k_aug, vsel_ref[0, 0:nk, :], mask_fn), 1, False)

    for n_full in range(ksel_ref.shape[1] // SEL_CHUNK):
        pl.when(last == n_full)(functools.partial(branches, n_full))

    o_ref[0] = oacc_ref[...].astype(o_ref.dtype)


def _nsa(qa, kva, kc, vc, ga, kx, kxc, qx, selbias):
    bsz, s, nq = qa.shape
    nc = kc.shape[1]
    tq = TQ_NSA
    rows = NSA_HEADS * tq
    kv_blk = lambda i: pl.BlockSpec((1, s, LANES), lambda b, q, i=i: (b, 0, i))
    cmp_blk = pl.BlockSpec((1, nc, LANES), lambda b, q: (b, 0, 0))
    const = lambda a: pl.BlockSpec(a.shape, lambda b, q: (0,) * a.ndim)
    return pl.pallas_call(
        _nsa_kernel,
        grid=(bsz, s // tq),
        in_specs=[pl.BlockSpec((1, tq, nq), lambda b, q: (b, q, 0)),
                  cmp_blk, cmp_blk, kv_blk(2), kv_blk(3), kv_blk(4), kv_blk(5),
                  const(kx), const(kxc), const(qx),
                  pl.BlockSpec((1, tq, NSA_GROUPS * LANES), lambda b, q: (b, q, 0)),
                  pl.BlockSpec((1, tq, GATE_PAD), lambda b, q: (b, q, 0))],
        out_specs=pl.BlockSpec((1, tq, nq), lambda b, q: (b, q, 0)),
        out_shape=jax.ShapeDtypeStruct((bsz, s, nq), BF16),
        scratch_shapes=[pltpu.VMEM((rows, 2 * LANES), BF16),
                        pltpu.VMEM((rows, 2 * LANES), BF16),
                        pltpu.VMEM((tq, nq), F32)],
        compiler_params=_cparams("parallel", "arbitrary"),
        name="nsa",
    )(qa, kc, vc, kva, kva, kva, kva, kx, kxc, qx, selbias, ga)


def _sb_kernel(q_ref, k_ref, v_ref, tri_ref, o_ref, carry_ref, acc_ref):
    tq, tk = TQ_SB, TK_SB
    qi = pl.program_id(2)
    lane = lax.broadcasted_iota(jnp.int32, (tq, LANES), 1)
    lo = lane < HEAD_DIM
    q2 = q_ref[0].astype(F32)
    qs = jnp.concatenate([jnp.where(lo, q2, 0.0), jnp.where(lo, 0.0, q2)], axis=0).astype(BF16)
    ntri = tri_ref[0:tk, :]

    def stick(z):
        neg_abs = lax.bitcast_convert_type(lax.bitcast_convert_type(z, jnp.int32) | jnp.int32(-2 ** 31), F32)
        sp = jnp.maximum(z, 0.0) + jnp.log2(1.0 + jnp.exp2(neg_abs))
        return sp, z - sp

    kpos = lax.broadcasted_iota(jnp.int32, (tq, tk), 1)
    tpos = lax.broadcasted_iota(jnp.int32, (tq, tk), 0)
    keep = jnp.concatenate([kpos < tpos] * 2, axis=0)

    def first_block(nt):
        w = nt * tk
        d0 = w - tk
        k_start = pl.multiple_of((qi - (nt - 1)) * tk, tk)
        sp, ls = stick(_dot_nt(qs, k_ref[0, pl.ds(k_start, w), :]))
        sp_d = jnp.where(keep, sp[:, d0:w], 0.0)
        sp_b = (jnp.concatenate([sp[:, 0:d0], sp_d], axis=1) if nt > 1 else sp_d).astype(BF16)
        a_parts = []
        for i in range(nt):
            later = _dot(sp_b[:, i * tk:w], tri_ref[0:w - i * tk, :])
            a = jnp.exp2(ls[:, i * tk:(i + 1) * tk] + later)
            a_parts.append(jnp.where(keep, a, 0.0) if i == nt - 1 else a)
            if i == 0:
                carry_ref[...] = later[:, 0:1] - sp_b[:, 0:1].astype(F32)
        a = (jnp.concatenate(a_parts, axis=1) if nt > 1 else a_parts[0]).astype(BF16)
        acc_ref[...] = _dot(a, v_ref[0, pl.ds(k_start, w), :])

    for nt in range(1, SB_FIRST_TILES):
        pl.when(qi == nt - 1)(functools.partial(first_block, nt))

    @pl.when(qi >= SB_FIRST_TILES - 1)
    def _():
        first_block(SB_FIRST_TILES)

        def tile(kt):
            k0 = pl.multiple_of(kt * tk, tk)
            sp, ls = stick(_dot_nt(qs, k_ref[0, pl.ds(k0, tk), :]))
            later = _dot(sp.astype(BF16), ntri)
            pv = _dot(jnp.exp2(ls + later).astype(BF16), v_ref[0, pl.ds(k0, tk), :])
            carry = carry_ref[...]
            acc_ref[...] += jnp.exp2(carry) * pv
            carry_ref[...] = carry - jnp.sum(sp, axis=-1, keepdims=True)

        def live():
            return jnp.max(carry_ref[...]) > SB_DEAD_LOG2

        def cond(st):
            return (st[0] >= 0) & st[1]

        def body(st):
            tile(st[0])
            return st[0] - 1, live()

        lax.while_loop(cond, body, (qi - SB_FIRST_TILES, live()))

    o_ref[0] = jnp.where(lo, acc_ref[0:tq], acc_ref[tq:2 * tq]).astype(o_ref.dtype)


def _sb(qkvb, tri):
    bsz, s, n3 = qkvb.shape
    npair = SB_HEADS // 2
    tq = TQ_SB
    return pl.pallas_call(
        _sb_kernel,
        grid=(bsz, npair, s // tq),
        in_specs=[pl.BlockSpec((1, tq, LANES), lambda b, p, q: (b, q, p)),
                  pl.BlockSpec((1, s, LANES), lambda b, p, q: (b, 0, npair + p)),
                  pl.BlockSpec((1, s, LANES), lambda b, p, q: (b, 0, 2 * npair + p)),
                  pl.BlockSpec(tri.shape, lambda b, p, q: (0, 0))],
        out_specs=pl.BlockSpec((1, tq, LANES), lambda b, p, q: (b, q, p)),
        out_shape=jax.ShapeDtypeStruct((bsz, s, npair * LANES), BF16),
        scratch_shapes=[pltpu.VMEM((2 * tq, 1), F32),
                        pltpu.VMEM((2 * tq, LANES), F32)],
        compiler_params=_cparams("parallel", "parallel", "arbitrary"),
        name="sb",
    )(qkvb, qkvb, qkvb, tri)


def _post_kernel(x_ref, oa_ref, ob_ref, mg_ref, mod_ref, wpa_ref, wpb_ref, wo_ref, g2_ref,
                 wg_ref, wu_ref, wd_ref, gf_ref, out_ref):
    d = x_ref.shape[2]
    part_rows = x_ref.shape[1] // POST_CHAINS
    for i in range(POST_CHAINS):
        rs = slice(i * part_rows, (i + 1) * part_rows)
        x = x_ref[0, rs, :]
        ya = _dot(oa_ref[0, rs, :], wpa_ref[...])
        yb = _dot(ob_ref[0, rs, :], wpb_ref[...])
        y = mg_ref[0, rs, 0:d].astype(F32) * ya + mg_ref[0, rs, d:2 * d].astype(F32) * yb
        x1 = x + mod_ref[0, 2:3, :] * _dot(y.astype(BF16), wo_ref[...])
        h2 = (_rmsnorm(x1, g2_ref[...]) * (1.0 + mod_ref[0, 4:5, :]) + mod_ref[0, 3:4, :]).astype(BF16)
        ffn = None
        for c0 in range(0, wg_ref.shape[1], FF_CHUNK):
            gte = _dot(h2, wg_ref[:, c0:c0 + FF_CHUNK])
            up = _dot(h2, wu_ref[:, c0:c0 + FF_CHUNK])
            act = (gte * jax.nn.sigmoid(gte) * up).astype(BF16)
            part = _dot(act, wd_ref[c0:c0 + FF_CHUNK, :])
            ffn = part if ffn is None else ffn + part
        x2 = x1 + mod_ref[0, 5:6, :] * ffn
        out_ref[0, rs, :] = _rmsnorm(x2, gf_ref[...])


def _post(x, oa, ob, mg, mod3, wpa, wpb, wo, g2, wg3, wu3, wd3, gf):
    bsz, s, d = x.shape
    tm = TM_POST
    tok = lambda n: pl.BlockSpec((1, tm, n), lambda b, i: (b, i, 0))
    return pl.pallas_call(
        _post_kernel,
        grid=(bsz, s // tm),
        in_specs=[tok(d), tok(oa.shape[2]), tok(ob.shape[2]), tok(mg.shape[2]),
                  pl.BlockSpec((1, 6, d), lambda b, i: (b, 0, 0)),
                  _resident(wpa.shape), _resident(wpb.shape), _resident(wo.shape),
                  pl.BlockSpec((1, d), lambda b, i: (0, 0)),
                  _resident(wg3.shape), _resident(wu3.shape), _resident(wd3.shape),
                  pl.BlockSpec((1, d), lambda b, i: (0, 0))],
        out_specs=tok(d),
        out_shape=jax.ShapeDtypeStruct((bsz, s, d), F32),
        compiler_params=_cparams("parallel", "parallel"),
        name="post",
    )(x, oa, ob, mg, mod3, wpa, wpb, wo, g2.reshape(1, d), wg3, wu3, wd3, gf.reshape(1, d))


def _bf16_pieces(x, n=3):
    out, rem = [], float(x)
    for _ in range(n):
        p = float(np.asarray(rem, np.float32).astype(jnp.bfloat16).astype(np.float32))
        out.append(p)
        rem -= p
    return out


N_PIECES = 3


def _key_extras(pos):
    lane = jnp.arange(LANES)[None, :]
    a = (pos // SEL_BLOCK)[:, None]
    b = (pos % SEL_BLOCK)[:, None]
    ext = jnp.where(lane < N_PIECES, a, jnp.where(lane < 2 * N_PIECES, b, 0)).astype(F32)
    return ext, a


def _query_extras():
    qx = np.zeros((NSA_HEADS, LANES), np.float32)
    for h in range(NSA_HEADS):
        slope = 2.0 ** (-(h + 1))
        for i, p in enumerate(_bf16_pieces(LOG2E, N_PIECES)):
            qx[h, i] = SEL_BLOCK * slope * p
            qx[h, N_PIECES + i] = slope * p
    return jnp.asarray(qx)


def kernel(x, c, w_ada, b_ada, norm_mix_g, w_in, cmp_pos_k, cmp_w1_k, cmp_w2_k, cmp_pos_v, cmp_w1_v, cmp_w2_v,
           w_proj_a, w_proj_b, w_out, norm_ffn_g, w_ffn_gate, w_ffn_up, w_ffn_down, norm_final_g):
    bsz, s, d = x.shape
    depth = w_ada.shape[0]
    assert s % SEL_CHUNK == 0 and s // SEL_BLOCK <= 32 and s % TM_IN == 0 and s >= WINDOW + TQ_NSA
    n_q = NSA_HEADS * HEAD_DIM
    n_kv = 6 * NSA_GROUPS * HEAD_DIM
    n_gate = 3 * NSA_HEADS
    n_b = 3 * SB_HEADS * HEAD_DIM
    n_m = 2 * d
    nc = s // CMP_STRIDE
    d_ff = w_ffn_gate.shape[2]
    assert d_ff % FF_CHUNK == 0

    pos = jnp.arange(s)
    ext, blk = _key_extras(pos)
    lane = jnp.arange(LANES)[None, :]
    kx = jnp.where(lane == SEL_LANE0 + blk, 1.0, ext).astype(BF16)
    kxc = _key_extras(jnp.arange(nc) * CMP_STRIDE + CMP_BLOCK - 1)[0].astype(BF16)
    jrow = jnp.arange(LANES)[:, None] - SEL_LANE0
    ncol = jnp.arange(nc)[None, :]
    ovl = ((ncol * CMP_STRIDE <= jrow * SEL_BLOCK + SEL_BLOCK - 1)
           & (ncol * CMP_STRIDE + CMP_BLOCK - 1 >= jrow * SEL_BLOCK)
           & (jrow >= 0) & (jrow < s // SEL_BLOCK) & (ncol < nc - 1)).astype(BF16)
    qx = _query_extras()
    jj = jnp.arange(TK_SB)
    tri = -jnp.concatenate([jj[:, None] > jj[None, :]] + [jnp.ones((TK_SB, TK_SB), jnp.bool_)] * (SB_FIRST_TILES - 1),
                           axis=0).astype(BF16)

    for l in range(depth):
        w = w_in[l]
        wq = (w[:, :n_q] * (ATTN_SCALE * LOG2E)).reshape(d, NSA_GROUPS, NSA_REP, HEAD_DIM).transpose(0, 2, 1, 3).reshape(d, n_q)
        o1 = n_q + n_kv
        o2 = o1 + n_gate
        o3 = o2 + n_b
        wgate = jnp.pad(w[:, o1:o2], ((0, 0), (0, GATE_PAD - n_gate)))
        wb = jnp.concatenate([w[:, o2:o2 + n_b // 3] * (ATTN_SCALE * LOG2E), w[:, o2 + n_b // 3:o3]], axis=1)
        w_all = jnp.concatenate([wq, w[:, n_q:o1], wgate, wb, w[:, o3:]], axis=1).astype(BF16)
        splits = (n_q, n_kv, GATE_PAD, n_b, n_m)

        def cmp_weights(pos_emb, w1, w2):
            half = CMP_BLOCK // 2
            w1r = w1.reshape(2, half, HEAD_DIM, CMP_HIDDEN)
            z = jnp.zeros_like(w1r)
            g0 = jnp.concatenate([w1r, z], axis=3)
            g1 = jnp.concatenate([z, w1r], axis=3)
            w1p = jnp.stack([g0, g1], axis=2).reshape(2 * half * 2 * HEAD_DIM, 2 * CMP_HIDDEN)
            zz = jnp.zeros_like(w2)
            w2p = jnp.concatenate([jnp.concatenate([w2, zz], axis=1), jnp.concatenate([zz, w2], axis=1)], axis=0)
            pe = pos_emb.reshape(2, half, 1, HEAD_DIM)
            pe = jnp.broadcast_to(pe, (2, half, 2, HEAD_DIM)).reshape(2, half * 2 * HEAD_DIM)
            return pe, w1p.astype(BF16), w2p.astype(BF16)

        pk, w1k, w2k = cmp_weights(cmp_pos_k[l], cmp_w1_k[l], cmp_w2_k[l])
        pv, w1v, w2v = cmp_weights(cmp_pos_v[l], cmp_w1_v[l], cmp_w2_v[l])
        wpa = w_proj_a[l].reshape(NSA_GROUPS, NSA_REP, HEAD_DIM, d).transpose(1, 0, 2, 3).reshape(n_q, d).astype(BF16)
        wpb = w_proj_b[l].astype(BF16)
        wo = w_out[l].astype(BF16)
        wg3 = w_ffn_gate[l].astype(BF16)
        wu3 = w_ffn_up[l].astype(BF16)
        wd3 = w_ffn_down[l].astype(BF16)

        mod3 = _ada(c, w_ada[l], b_ada[l]).reshape(bsz, 6, d)
        qa, kva, qkvb, mg, ga = _in_proj(x, mod3, norm_mix_g[l], w_all, splits)
        k16 = kva[:, :, 0:LANES].reshape(bsz, nc, CMP_STRIDE * LANES)
        v16 = kva[:, :, LANES:2 * LANES].reshape(bsz, nc, CMP_STRIDE * LANES)
        kc, vc = _compress(k16, v16, pk, pv, w1k, w1v, w2k, w2v)
        selbias = _nsa_select(qa, kc, kxc, ovl, qx)
        oa = _nsa(qa, kva, kc, vc, ga, kx, kxc, qx, selbias)
        ob = _sb(qkvb, tri)
        gf = norm_final_g if l == depth - 1 else jnp.ones_like(norm_final_g)
        x = _post(x, oa, ob, mg, mod3, wpa, wpb, wo, norm_ffn_g[l], wg3, wu3, wd3, gf)
        assert depth == 1, "final norm is fused into the last layer's post kernel"
    return x
```

```python
import functools
import math

import jax
import jax.numpy as jnp
import numpy as np
from jax import lax
from jax.experimental import pallas as pl
from jax.experimental.pallas import tpu as pltpu

F32 = jnp.float32
BF16 = jnp.bfloat16

HEAD_DIM = 64
NSA_HEADS = 8
NSA_GROUPS = 2
NSA_REP = NSA_HEADS // NSA_GROUPS
SB_HEADS = 8
CMP_BLOCK = 32
CMP_STRIDE = 16
CMP_HIDDEN = 2 * HEAD_DIM
SEL_BLOCK = 64
SEL_TOPK = 8
WINDOW = 512
RMS_EPS = 1e-6
NEG_INF = -1e30
FORCE_PRIORITY = 1e4
ATTN_SCALE = 1.0 / math.sqrt(HEAD_DIM)
LOG2E = math.log2(math.e)

LANES = 128
PAIR = 2 * HEAD_DIM
SEL_LANE0 = 32
GATE_PAD = 128

TM_IN = 1024
IN_CHUNK = 512
TM_POST = 512
POST_CHAINS = 1
TQ_NSA = 256
TQ_SELECT = 1024
NSA_CHAINS = 4
SEL_CHUNK = 256
TQ_SB = 256
TK_SB = 256
FF_CHUNK = 256
SB_DEAD_LOG2 = -150.0
SB_FIRST_TILES = 3

VMEM_LIMIT = 56 * 1024 * 1024


def _cparams(*sem):
    return pltpu.CompilerParams(dimension_semantics=sem, vmem_limit_bytes=VMEM_LIMIT)


def _resident(shape):
    nd = len(shape)
    return pl.BlockSpec(shape, lambda *_: (0,) * nd, pipeline_mode=pl.Buffered(1))


def _dot(a, b):
    return jnp.dot(a, b, preferred_element_type=F32)


def _dot_nt(a, b):
    return lax.dot_general(a, b, (((1,), (1,)), ((), ())), preferred_element_type=F32)


def _split_bf16(v):
    hi = v.astype(BF16)
    lo = (v - hi.astype(F32)).astype(BF16)
    return hi, lo


def _rmsnorm(x, g):
    return x * lax.rsqrt(jnp.mean(x * x, axis=-1, keepdims=True) + RMS_EPS) * g


def _ada_kernel(c_ref, w_ref, b_ref, o_ref):
    c = c_ref[...]
    a = (c * jax.nn.sigmoid(c)).astype(BF16)
    o_ref[...] = _dot(a, w_ref[...].astype(BF16)) + b_ref[...]


def _ada(c, w, b):
    bsz, d = c.shape
    n = w.shape[1]
    tn = d
    return pl.pallas_call(
        _ada_kernel,
        grid=(n // tn,),
        in_specs=[pl.BlockSpec((bsz, d), lambda j: (0, 0)),
                  pl.BlockSpec((d, tn), lambda j: (0, j)),
                  pl.BlockSpec((1, tn), lambda j: (0, j))],
        out_specs=pl.BlockSpec((bsz, tn), lambda j: (0, j)),
        out_shape=jax.ShapeDtypeStruct((bsz, n), F32),
        compiler_params=_cparams("arbitrary"),
        name="ada",
    )(c, w, b.reshape(1, n))


def _in_proj_kernel(x_ref, mod_ref, g_ref, w_ref, qa_ref, kva_ref, qkvb_ref, mg_ref, ga_ref, *, splits):
    x = x_ref[0]
    h = _rmsnorm(x, g_ref[...]) * (1.0 + mod_ref[0, 1:2, :]) + mod_ref[0, 0:1, :]
    hb = h.astype(BF16)
    n_q, n_kv, n_g, n_b, n_m = splits
    col = 0
    for ref, width, act in ((qa_ref, n_q, None), (kva_ref, n_kv, None), (ga_ref, n_g, "sig"),
                            (qkvb_ref, n_b, None), (mg_ref, n_m, "sig")):
        for c0 in range(0, width, IN_CHUNK):
            cw = min(IN_CHUNK, width - c0)
            r = _dot(hb, w_ref[:, col + c0:col + c0 + cw])
            if act == "sig":
                r = jax.nn.sigmoid(r)
            ref[0, :, c0:c0 + cw] = r.astype(ref.dtype)
        col += width


def _in_proj(x, mod3, g, w_all, splits):
    bsz, s, d = x.shape
    n_q, n_kv, n_g, n_b, n_m = splits
    tm = TM_IN
    out_shape = (jax.ShapeDtypeStruct((bsz, s, n_q), BF16),
                 jax.ShapeDtypeStruct((bsz, s, n_kv), BF16),
                 jax.ShapeDtypeStruct((bsz, s, n_b), BF16),
                 jax.ShapeDtypeStruct((bsz, s, n_m), BF16),
                 jax.ShapeDtypeStruct((bsz, s, n_g), F32))
    tok = lambda n: pl.BlockSpec((1, tm, n), lambda b, i: (b, i, 0))
    return pl.pallas_call(
        functools.partial(_in_proj_kernel, splits=splits),
        grid=(bsz, s // tm),
        in_specs=[tok(d),
                  pl.BlockSpec((1, 6, d), lambda b, i: (b, 0, 0)),
                  pl.BlockSpec((1, d), lambda b, i: (0, 0)),
                  _resident(w_all.shape)],
        out_specs=(tok(n_q), tok(n_kv), tok(n_b), tok(n_m), tok(n_g)),
        out_shape=out_shape,
        compiler_params=_cparams("parallel", "parallel"),
        name="in_proj",
    )(x, mod3, g.reshape(1, d), w_all)


def _gelu_tanh(x):
    return 0.5 * x * (1.0 + jnp.tanh(math.sqrt(2.0 / math.pi) * (x + 0.044715 * (x * x * x))))


def _compress_kernel(k16_ref, v16_ref, pk_ref, pv_ref, w1k_ref, w1v_ref, w2k_ref, w2v_ref, kc_ref, vc_ref):
    nc = k16_ref.shape[1]
    half = w1k_ref.shape[0] // 2

    def one(x_ref, p_ref, w1_ref, w2_ref, o_ref):
        xin = x_ref[0].astype(F32)
        top = (xin + p_ref[0:1, :]).astype(BF16)
        bot = (xin + p_ref[1:2, :]).astype(BF16)
        a = _dot(top, w1_ref[0:half, :])
        b = _dot(bot, w1_ref[half:, :])
        hid = a + pltpu.roll(b, nc - 1, 0)
        o_ref[0] = _dot(_gelu_tanh(hid).astype(BF16), w2_ref[...]).astype(o_ref.dtype)

    one(k16_ref, pk_ref, w1k_ref, w2k_ref, kc_ref)
    one(v16_ref, pv_ref, w1v_ref, w2v_ref, vc_ref)


def _compress(k16, v16, pk, pv, w1k, w1v, w2k, w2v):
    bsz, nc, wide = k16.shape
    blk = pl.BlockSpec((1, nc, wide), lambda b: (b, 0, 0))
    out = pl.BlockSpec((1, nc, LANES), lambda b: (b, 0, 0))
    full = lambda a: pl.BlockSpec(a.shape, lambda b: (0,) * a.ndim)
    return pl.pallas_call(
        _compress_kernel,
        grid=(bsz,),
        in_specs=[blk, blk, full(pk), full(pv), full(w1k), full(w1v), full(w2k), full(w2v)],
        out_specs=(out, out),
        out_shape=(jax.ShapeDtypeStruct((bsz, nc, LANES), BF16),) * 2,
        compiler_params=_cparams("parallel"),
        name="compress",
    )(k16, v16, pk, pv, w1k, w1v, w2k, w2v)


def _build_qaug(qa_ref, qx_ref, qaug_ref, tq):
    lo = lax.broadcasted_iota(jnp.int32, (tq, LANES), 1) < HEAD_DIM
    for g in range(NSA_GROUPS):
        for r in range(NSA_REP):
            h = g * NSA_REP + r
            qr = qa_ref[0, :, r * LANES:(r + 1) * LANES].astype(F32)
            qm = jnp.where(lo if g == 0 else jnp.logical_not(lo), qr, 0.0)
            qaug_ref[h * tq:(h + 1) * tq, 0:LANES] = qm.astype(BF16)
            qaug_ref[h * tq:(h + 1) * tq, LANES:2 * LANES] = jnp.broadcast_to(qx_ref[h:h + 1, :], (tq, LANES)).astype(BF16)


def _nsa_select_kernel(qa_ref, kc_ref, kxc_ref, ovl_ref, qx_ref, bias_ref, qaug_ref):
    tq = TQ_SELECT
    rows = NSA_HEADS * tq
    q0 = pl.program_id(1) * tq
    nc = kc_ref.shape[1]
    n_cmp = nc - 1
    _build_qaug(qa_ref, qx_ref, qaug_ref, tq)
    kc_aug = jnp.concatenate([kc_ref[0], kxc_ref[...]], axis=1)
    st = _dot_nt(kc_aug, qaug_ref[...])
    n_s = lax.broadcasted_iota(jnp.int32, (nc, rows), 0)
    t_s = q0 + (lax.broadcasted_iota(jnp.int32, (nc, rows), 1) & (tq - 1))
    sm = jnp.where((t_s >= n_s * CMP_STRIDE + (CMP_BLOCK - 1)) & (n_s < n_cmp), st, NEG_INF)
    e = jnp.exp2(sm - jnp.max(sm, axis=0, keepdims=True))
    p_t = jnp.where(t_s >= CMP_BLOCK - 1, e / jnp.sum(e, axis=0, keepdims=True), 0.0)
    nsel_rows = 32
    j_s = lax.broadcasted_iota(jnp.int32, (nsel_rows, tq), 0)
    cur = (q0 + lax.broadcasted_iota(jnp.int32, (nsel_rows, tq), 1)) // SEL_BLOCK
    valid_j = j_s <= cur
    forced = ((j_s == 0) | (j_s == cur) | (j_s == cur - 1)) & valid_j
    for g in range(NSA_GROUPS):
        base = g * NSA_REP * tq
        psum = (p_t[:, base:base + tq] + p_t[:, base + tq:base + 2 * tq]
                + p_t[:, base + 2 * tq:base + 3 * tq] + p_t[:, base + 3 * tq:base + 4 * tq])
        p_hi, p_lo = _split_bf16(psum)
        p_slc = _dot(ovl_ref[...], p_hi) + _dot(ovl_ref[...], p_lo)
        prio = p_slc[SEL_LANE0:SEL_LANE0 + nsel_rows]
        prio = jnp.where(valid_j, prio, NEG_INF)
        prio = jnp.where(forced, FORCE_PRIORITY, prio)
        chosen = jnp.zeros((nsel_rows, tq), jnp.bool_)
        for _ in range(SEL_TOPK):
            best = jnp.max(prio, axis=0, keepdims=True)
            first = jnp.min(jnp.where(prio == best, j_s, nsel_rows), axis=0, keepdims=True)
            pick = j_s == first
            chosen = chosen | pick
            prio = jnp.where(pick, -3e38, prio)
        bias_t = jnp.where(chosen, 0.0, NEG_INF)
        bias_t = jnp.concatenate([jnp.zeros((SEL_LANE0, tq), F32), bias_t,
                                  jnp.zeros((LANES - SEL_LANE0 - nsel_rows, tq), F32)], axis=0)
        for c0 in range(0, tq, LANES):
            bias_ref[0, c0:c0 + LANES, g * LANES:(g + 1) * LANES] = bias_t[:, c0:c0 + LANES].T.astype(BF16)


def _nsa_select(qa, kc, kxc, ovl, qx):
    bsz, s, nq = qa.shape
    nc = kc.shape[1]
    tq = TQ_SELECT
    const = lambda a: pl.BlockSpec(a.shape, lambda b, q: (0,) * a.ndim)
    return pl.pallas_call(
        _nsa_select_kernel,
        grid=(bsz, s // tq),
        in_specs=[pl.BlockSpec((1, tq, nq), lambda b, q: (b, q, 0)),
                  pl.BlockSpec((1, nc, LANES), lambda b, q: (b, 0, 0)),
                  const(kxc), const(ovl), const(qx)],
        out_specs=pl.BlockSpec((1, tq, NSA_GROUPS * LANES), lambda b, q: (b, q, 0)),
        out_shape=jax.ShapeDtypeStruct((bsz, s, NSA_GROUPS * LANES), BF16),
        scratch_shapes=[pltpu.VMEM((NSA_HEADS * tq, 2 * LANES), BF16)],
        compiler_params=_cparams("parallel", "parallel"),
        name="nsa_select",
    )(qa, kc, kxc, ovl, qx)


def _nsa_kernel(qa_ref, kc_ref, vc_ref, ksel_ref, vsel_ref, kwin_ref, vwin_ref, kx_ref, kxc_ref, qx_ref, bias_ref,
                ga_ref, o_ref, qaug_ref, qsel_ref, oacc_ref):
    tq = TQ_NSA
    nh = NSA_HEADS
    rows = nh * tq
    qi = pl.program_id(1)
    q0 = pl.multiple_of(qi * tq, tq)
    nc = kc_ref.shape[1]
    n_cmp = nc - 1

    lane = lax.broadcasted_iota(jnp.int32, (tq, LANES), 1)
    row = lax.broadcasted_iota(jnp.int32, (tq, LANES), 0)
    lo = lane < HEAD_DIM

    _build_qaug(qa_ref, qx_ref, qaug_ref, tq)
    for h in range(nh):
        g = h // NSA_REP
        qsel_ref[h * tq:(h + 1) * tq, 0:LANES] = qaug_ref[h * tq:(h + 1) * tq, 0:LANES]
        qsel_ref[h * tq:(h + 1) * tq, LANES:2 * LANES] = (
            bias_ref[0, :, g * LANES:(g + 1) * LANES].astype(F32) + qx_ref[h:h + 1, :]).astype(BF16)

    ga = ga_ref[0]

    def emit(o_heads, branch, first):
        for r in range(NSA_REP):
            c0 = r * 3 + branch
            c1 = NSA_REP * 3 + r * 3 + branch
            gt = jnp.where(lo, ga[:, c0:c0 + 1], ga[:, c1:c1 + 1])
            val = gt * jnp.where(lo, o_heads[r * tq:(r + 1) * tq], o_heads[(NSA_REP + r) * tq:(NSA_REP + r + 1) * tq])
            if first:
                oacc_ref[:, r * LANES:(r + 1) * LANES] = val
            else:
                oacc_ref[:, r * LANES:(r + 1) * LANES] += val

    def attend(q, k_aug, v, mask_fn):
        w = k_aug.shape[0]
        v_aug = jnp.concatenate([v, jnp.ones((w, LANES), BF16)], axis=1)
        part = rows // NSA_CHAINS
        outs = []
        for c in range(NSA_CHAINS):
            s3 = mask_fn(_dot_nt(q[c * part:(c + 1) * part], k_aug).reshape(part // tq, tq, w))
            p = jnp.exp2(s3 - jnp.max(s3, axis=-1, keepdims=True)).astype(BF16).reshape(part, w)
            pv = _dot(p, v_aug)
            outs.append(pv[:, 0:LANES] / pv[:, LANES:2 * LANES])
        return jnp.concatenate(outs, axis=0)

    qaug = qaug_ref[...]

    qsel = qsel_ref[...]
    last = q0 // SEL_CHUNK

    def branches(n_full):
        kc_aug = jnp.concatenate([kc_ref[0], kxc_ref[...]], axis=1)
        n_l = lax.broadcasted_iota(jnp.int32, (tq, nc), 1)
        t_l = q0 + lax.broadcasted_iota(jnp.int32, (tq, nc), 0)
        valid = (t_l >= n_l * CMP_STRIDE + (CMP_BLOCK - 1)) & (n_l < n_cmp)
        o_cmp = attend(qaug, kc_aug, vc_ref[0], lambda s3: jnp.where(valid[None], s3, NEG_INF))
        has_cmp = jnp.concatenate([q0 + row >= CMP_BLOCK - 1] * nh, axis=0)
        emit(jnp.where(has_cmp, o_cmp, 0.0), 0, True)

        span = WINDOW + tq
        w_start = pl.multiple_of(jnp.maximum(q0 - WINDOW, 0), tq)
        k_aug = jnp.concatenate([kwin_ref[0, pl.ds(w_start, span), :], kx_ref[pl.ds(w_start, span), :]], axis=1)
        dist = (q0 + lax.broadcasted_iota(jnp.int32, (tq, span), 0)) - (w_start + lax.broadcasted_iota(jnp.int32, (tq, span), 1))
        band = (dist >= 0) & (dist < WINDOW)
        emit(attend(qaug, k_aug, vwin_ref[0, pl.ds(w_start, span), :], lambda s3: jnp.where(band[None], s3, NEG_INF)), 2, False)

        nk = (n_full + 1) * SEL_CHUNK
        k_aug = jnp.concatenate([ksel_ref[0, 0:nk, :], kx_ref[0:nk, :]], axis=1)
        kpos = n_full * SEL_CHUNK + lax.broadcasted_iota(jnp.int32, (tq, SEL_CHUNK), 1)
        tpos = q0 + lax.broadcasted_iota(jnp.int32, (tq, SEL_CHUNK), 0)
        causal = (kpos <= tpos)[None]

        def mask_fn(s3):
            diag = jnp.where(causal, s3[:, :, n_full * SEL_CHUNK:nk], NEG_INF)
            return jnp.concatenate([s3[:, :, 0:n_full * SEL_CHUNK], diag], axis=2) if n_full else diag

        emit(attend(qsel, k_aug, vsel_ref[0, 0:nk, :], mask_fn), 1, False)

    for n_full in range(ksel_ref.shape[1] // SEL_CHUNK):
        pl.when(last == n_full)(functools.partial(branches, n_full))

    o_ref[0] = oacc_ref[...].astype(o_ref.dtype)


def _nsa(qa, kva, kc, vc, ga, kx, kxc, qx, selbias):
    bsz, s, nq = qa.shape
    nc = kc.shape[1]
    tq = TQ_NSA
    rows = NSA_HEADS * tq
    kv_blk = lambda i: pl.BlockSpec((1, s, LANES), lambda b, q, i=i: (b, 0, i))
    cmp_blk = pl.BlockSpec((1, nc, LANES), lambda b, q: (b, 0, 0))
    const = lambda a: pl.BlockSpec(a.shape, lambda b, q: (0,) * a.ndim)
    return pl.pallas_call(
        _nsa_kernel,
        grid=(bsz, s // tq),
        in_specs=[pl.BlockSpec((1, tq, nq), lambda b, q: (b, q, 0)),
                  cmp_blk, cmp_blk, kv_blk(2), kv_blk(3), kv_blk(4), kv_blk(5),
                  const(kx), const(kxc), const(qx),
                  pl.BlockSpec((1, tq, NSA_GROUPS * LANES), lambda b, q: (b, q, 0)),
                  pl.BlockSpec((1, tq, GATE_PAD), lambda b, q: (b, q, 0))],
        out_specs=pl.BlockSpec((1, tq, nq), lambda b, q: (b, q, 0)),
        out_shape=jax.ShapeDtypeStruct((bsz, s, nq), BF16),
        scratch_shapes=[pltpu.VMEM((rows, 2 * LANES), BF16),
                        pltpu.VMEM((rows, 2 * LANES), BF16),
                        pltpu.VMEM((tq, nq), F32)],
        compiler_params=_cparams("parallel", "arbitrary"),
        name="nsa",
    )(qa, kc, vc, kva, kva, kva, kva, kx, kxc, qx, selbias, ga)


def _sb_kernel(q_ref, k_ref, v_ref, tri_ref, o_ref, carry_ref, acc_ref):
    tq, tk = TQ_SB, TK_SB
    qi = pl.program_id(2)
    lane = lax.broadcasted_iota(jnp.int32, (tq, LANES), 1)
    lo = lane < HEAD_DIM
    q2 = q_ref[0].astype(F32)
    qs = jnp.concatenate([jnp.where(lo, q2, 0.0), jnp.where(lo, 0.0, q2)], axis=0).astype(BF16)
    ntri = tri_ref[0:tk, :]

    def stick(z):
        sp = jnp.maximum(z, 0.0) + jnp.log2(1.0 + jnp.exp2(-jnp.abs(z)))
        return sp, z - sp

    kpos = lax.broadcasted_iota(jnp.int32, (tq, tk), 1)
    tpos = lax.broadcasted_iota(jnp.int32, (tq, tk), 0)
    keep = jnp.concatenate([kpos < tpos] * 2, axis=0)

    def first_block(nt):
        w = nt * tk
        d0 = w - tk
        k_start = pl.multiple_of((qi - (nt - 1)) * tk, tk)
        sp, ls = stick(_dot_nt(qs, k_ref[0, pl.ds(k_start, w), :]))
        sp_d = jnp.where(keep, sp[:, d0:w], 0.0)
        sp_b = (jnp.concatenate([sp[:, 0:d0], sp_d], axis=1) if nt > 1 else sp_d).astype(BF16)
        a_parts = []
        for i in range(nt):
            later = _dot(sp_b[:, i * tk:w], tri_ref[0:w - i * tk, :])
            a = jnp.exp2(ls[:, i * tk:(i + 1) * tk] + later)
            a_parts.append(jnp.where(keep, a, 0.0) if i == nt - 1 else a)
            if i == 0:
                carry_ref[...] = later[:, 0:1] - sp_b[:, 0:1].astype(F32)
        a = (jnp.concatenate(a_parts, axis=1) if nt > 1 else a_parts[0]).astype(BF16)
        acc_ref[...] = _dot(a, v_ref[0, pl.ds(k_start, w), :])

    for nt in range(1, SB_FIRST_TILES):
        pl.when(qi == nt - 1)(functools.partial(first_block, nt))

    @pl.when(qi >= SB_FIRST_TILES - 1)
    def _():
        first_block(SB_FIRST_TILES)

        def tile(kt):
            k0 = pl.multiple_of(kt * tk, tk)
            sp, ls = stick(_dot_nt(qs, k_ref[0, pl.ds(k0, tk), :]))
            later = _dot(sp.astype(BF16), ntri)
            pv = _dot(jnp.exp2(ls + later).astype(BF16), v_ref[0, pl.ds(k0, tk), :])
            carry = carry_ref[...]
            acc_ref[...] += jnp.exp2(carry) * pv
            carry_ref[...] = carry - jnp.sum(sp, axis=-1, keepdims=True)

        def live():
            return jnp.max(carry_ref[...]) > SB_DEAD_LOG2

        def cond(st):
            return (st[0] >= 0) & st[1]

        def body(st):
            tile(st[0])
            return st[0] - 1, live()

        lax.while_loop(cond, body, (qi - SB_FIRST_TILES, live()))

    o_ref[0] = jnp.where(lo, acc_ref[0:tq], acc_ref[tq:2 * tq]).astype(o_ref.dtype)


def _sb(qkvb, tri):
    bsz, s, n3 = qkvb.shape
    npair = SB_HEADS // 2
    tq = TQ_SB
    return pl.pallas_call(
        _sb_kernel,
        grid=(bsz, npair, s // tq),
        in_specs=[pl.BlockSpec((1, tq, LANES), lambda b, p, q: (b, q, p)),
                  pl.BlockSpec((1, s, LANES), lambda b, p, q: (b, 0, npair + p)),
                  pl.BlockSpec((1, s, LANES), lambda b, p, q: (b, 0, 2 * npair + p)),
                  pl.BlockSpec(tri.shape, lambda b, p, q: (0, 0))],
        out_specs=pl.BlockSpec((1, tq, LANES), lambda b, p, q: (b, q, p)),
        out_shape=jax.ShapeDtypeStruct((bsz, s, npair * LANES), BF16),
        scratch_shapes=[pltpu.VMEM((2 * tq, 1), F32),
                        pltpu.VMEM((2 * tq, LANES), F32)],
        compiler_params=_cparams("parallel", "parallel", "arbitrary"),
        name="sb",
    )(qkvb, qkvb, qkvb, tri)


def _post_kernel(x_ref, oa_ref, ob_ref, mg_ref, mod_ref, wpa_ref, wpb_ref, wo_ref, g2_ref,
                 wg_ref, wu_ref, wd_ref, gf_ref, out_ref):
    d = x_ref.shape[2]
    part_rows = x_ref.shape[1] // POST_CHAINS
    for i in range(POST_CHAINS):
        rs = slice(i * part_rows, (i + 1) * part_rows)
        x = x_ref[0, rs, :]
        ya = _dot(oa_ref[0, rs, :], wpa_ref[...])
        yb = _dot(ob_ref[0, rs, :], wpb_ref[...])
        y = mg_ref[0, rs, 0:d].astype(F32) * ya + mg_ref[0, rs, d:2 * d].astype(F32) * yb
        x1 = x + mod_ref[0, 2:3, :] * _dot(y.astype(BF16), wo_ref[...])
        h2 = (_rmsnorm(x1, g2_ref[...]) * (1.0 + mod_ref[0, 4:5, :]) + mod_ref[0, 3:4, :]).astype(BF16)
        ffn = None
        for c0 in range(0, wg_ref.shape[1], FF_CHUNK):
            gte = _dot(h2, wg_ref[:, c0:c0 + FF_CHUNK])
            up = _dot(h2, wu_ref[:, c0:c0 + FF_CHUNK])
            act = (gte * jax.nn.sigmoid(gte) * up).astype(BF16)
            part = _dot(act, wd_ref[c0:c0 + FF_CHUNK, :])
            ffn = part if ffn is None else ffn + part
        x2 = x1 + mod_ref[0, 5:6, :] * ffn
        out_ref[0, rs, :] = _rmsnorm(x2, gf_ref[...])


def _post(x, oa, ob, mg, mod3, wpa, wpb, wo, g2, wg3, wu3, wd3, gf):
    bsz, s, d = x.shape
    tm = TM_POST
    tok = lambda n: pl.BlockSpec((1, tm, n), lambda b, i: (b, i, 0))
    return pl.pallas_call(
        _post_kernel,
        grid=(bsz, s // tm),
        in_specs=[tok(d), tok(oa.shape[2]), tok(ob.shape[2]), tok(mg.shape[2]),
                  pl.BlockSpec((1, 6, d), lambda b, i: (b, 0, 0)),
                  _resident(wpa.shape), _resident(wpb.shape), _resident(wo.shape),
                  pl.BlockSpec((1, d), lambda b, i: (0, 0)),
                  _resident(wg3.shape), _resident(wu3.shape), _resident(wd3.shape),
                  pl.BlockSpec((1, d), lambda b, i: (0, 0))],
        out_specs=tok(d),
        out_shape=jax.ShapeDtypeStruct((bsz, s, d), F32),
        compiler_params=_cparams("parallel", "parallel"),
        name="post",
    )(x, oa, ob, mg, mod3, wpa, wpb, wo, g2.reshape(1, d), wg3, wu3, wd3, gf.reshape(1, d))


def _bf16_pieces(x, n=3):
    out, rem = [], float(x)
    for _ in range(n):
        p = float(np.asarray(rem, np.float32).astype(jnp.bfloat16).astype(np.float32))
        out.append(p)
        rem -= p
    return out


N_PIECES = 3


def _key_extras(pos):
    lane = jnp.arange(LANES)[None, :]
    a = (pos // SEL_BLOCK)[:, None]
    b = (pos % SEL_BLOCK)[:, None]
    ext = jnp.where(lane < N_PIECES, a, jnp.where(lane < 2 * N_PIECES, b, 0)).astype(F32)
    return ext, a


def _query_extras():
    qx = np.zeros((NSA_HEADS, LANES), np.float32)
    for h in range(NSA_HEADS):
        slope = 2.0 ** (-(h + 1))
        for i, p in enumerate(_bf16_pieces(LOG2E, N_PIECES)):
            qx[h, i] = SEL_BLOCK * slope * p
            qx[h, N_PIECES + i] = slope * p
    return jnp.asarray(qx)


def kernel(x, c, w_ada, b_ada, norm_mix_g, w_in, cmp_pos_k, cmp_w1_k, cmp_w2_k, cmp_pos_v, cmp_w1_v, cmp_w2_v,
           w_proj_a, w_proj_b, w_out, norm_ffn_g, w_ffn_gate, w_ffn_up, w_ffn_down, norm_final_g):
    bsz, s, d = x.shape
    depth = w_ada.shape[0]
    assert s % SEL_CHUNK == 0 and s // SEL_BLOCK <= 32 and s % TM_IN == 0 and s >= WINDOW + TQ_NSA
    n_q = NSA_HEADS * HEAD_DIM
    n_kv = 6 * NSA_GROUPS * HEAD_DIM
    n_gate = 3 * NSA_HEADS
    n_b = 3 * SB_HEADS * HEAD_DIM
    n_m = 2 * d
    nc = s // CMP_STRIDE
    d_ff = w_ffn_gate.shape[2]
    assert d_ff % FF_CHUNK == 0

    pos = jnp.arange(s)
    ext, blk = _key_extras(pos)
    lane = jnp.arange(LANES)[None, :]
    kx = jnp.where(lane == SEL_LANE0 + blk, 1.0, ext).astype(BF16)
    kxc = _key_extras(jnp.arange(nc) * CMP_STRIDE + CMP_BLOCK - 1)[0].astype(BF16)
    jrow = jnp.arange(LANES)[:, None] - SEL_LANE0
    ncol = jnp.arange(nc)[None, :]
    ovl = ((ncol * CMP_STRIDE <= jrow * SEL_BLOCK + SEL_BLOCK - 1)
           & (ncol * CMP_STRIDE + CMP_BLOCK - 1 >= jrow * SEL_BLOCK)
           & (jrow >= 0) & (jrow < s // SEL_BLOCK) & (ncol < nc - 1)).astype(BF16)
    qx = _query_extras()
    jj = jnp.arange(TK_SB)
    tri = -jnp.concatenate([jj[:, None] > jj[None, :]] + [jnp.ones((TK_SB, TK_SB), jnp.bool_)] * (SB_FIRST_TILES - 1),
                           axis=0).astype(BF16)

    for l in range(depth):
        w = w_in[l]
        wq = (w[:, :n_q] * (ATTN_SCALE * LOG2E)).reshape(d, NSA_GROUPS, NSA_REP, HEAD_DIM).transpose(0, 2, 1, 3).reshape(d, n_q)
        o1 = n_q + n_kv
        o2 = o1 + n_gate
        o3 = o2 + n_b
        wgate = jnp.pad(w[:, o1:o2], ((0, 0), (0, GATE_PAD - n_gate)))
        wb = jnp.concatenate([w[:, o2:o2 + n_b // 3] * (ATTN_SCALE * LOG2E), w[:, o2 + n_b // 3:o3]], axis=1)
        w_all = jnp.concatenate([wq, w[:, n_q:o1], wgate, wb, w[:, o3:]], axis=1).astype(BF16)
        splits = (n_q, n_kv, GATE_PAD, n_b, n_m)

        def cmp_weights(pos_emb, w1, w2):
            half = CMP_BLOCK // 2
            w1r = w1.reshape(2, half, HEAD_DIM, CMP_HIDDEN)
            z = jnp.zeros_like(w1r)
            g0 = jnp.concatenate([w1r, z], axis=3)
            g1 = jnp.concatenate([z, w1r], axis=3)
            w1p = jnp.stack([g0, g1], axis=2).reshape(2 * half * 2 * HEAD_DIM, 2 * CMP_HIDDEN)
            zz = jnp.zeros_like(w2)
            w2p = jnp.concatenate([jnp.concatenate([w2, zz], axis=1), jnp.concatenate([zz, w2], axis=1)], axis=0)
            pe = pos_emb.reshape(2, half, 1, HEAD_DIM)
            pe = jnp.broadcast_to(pe, (2, half, 2, HEAD_DIM)).reshape(2, half * 2 * HEAD_DIM)
            return pe, w1p.astype(BF16), w2p.astype(BF16)

        pk, w1k, w2k = cmp_weights(cmp_pos_k[l], cmp_w1_k[l], cmp_w2_k[l])
        pv, w1v, w2v = cmp_weights(cmp_pos_v[l], cmp_w1_v[l], cmp_w2_v[l])
        wpa = w_proj_a[l].reshape(NSA_GROUPS, NSA_REP, HEAD_DIM, d).transpose(1, 0, 2, 3).reshape(n_q, d).astype(BF16)
        wpb = w_proj_b[l].astype(BF16)
        wo = w_out[l].astype(BF16)
        wg3 = w_ffn_gate[l].astype(BF16)
        wu3 = w_ffn_up[l].astype(BF16)
        wd3 = w_ffn_down[l].astype(BF16)

        mod3 = _ada(c, w_ada[l], b_ada[l]).reshape(bsz, 6, d)
        qa, kva, qkvb, mg, ga = _in_proj(x, mod3, norm_mix_g[l], w_all, splits)
        k16 = kva[:, :, 0:LANES].reshape(bsz, nc, CMP_STRIDE * LANES)
        v16 = kva[:, :, LANES:2 * LANES].reshape(bsz, nc, CMP_STRIDE * LANES)
        kc, vc = _compress(k16, v16, pk, pv, w1k, w1v, w2k, w2v)
        selbias = _nsa_select(qa, kc, kxc, ovl, qx)
        oa = _nsa(qa, kva, kc, vc, ga, kx, kxc, qx, selbias)
        ob = _sb(qkvb, tri)
        gf = norm_final_g if l == depth - 1 else jnp.ones_like(norm_final_g)
        x = _post(x, oa, ob, mg, mod3, wpa, wpb, wo, norm_ffn_g[l], wg3, wu3, wd3, gf)
        assert depth == 1, "final norm is fused into the last layer's post kernel"
    return x
```

```python
import functools
import math

import jax
import jax.numpy as jnp
import numpy as np
from jax import lax
from jax.experimental import pallas as pl
from jax.experimental.pallas import tpu as pltpu

F32 = jnp.float32
BF16 = jnp.bfloat16

HEAD_DIM = 64
NSA_HEADS = 8
NSA_GROUPS = 2
NSA_REP = NSA_HEADS // NSA_GROUPS
SB_HEADS = 8
CMP_BLOCK = 32
CMP_STRIDE = 16
CMP_HIDDEN = 2 * HEAD_DIM
SEL_BLOCK = 64
SEL_TOPK = 8
WINDOW = 512
RMS_EPS = 1e-6
NEG_INF = -1e30
FORCE_PRIORITY = 1e4
ATTN_SCALE = 1.0 / math.sqrt(HEAD_DIM)
LOG2E = math.log2(math.e)

LANES = 128
PAIR = 2 * HEAD_DIM
SEL_LANE0 = 32
GATE_PAD = 128

TM_IN = 1024
IN_CHUNK = 512
TM_POST = 512
POST_CHAINS = 1
TQ_NSA = 256
TQ_SELECT = 1024
NSA_CHAINS = 4
SEL_CHUNK = 512
TQ_SB = 256
TK_SB = 256
FF_CHUNK = 256
SB_DEAD_LOG2 = -150.0
SB_FIRST_TILES = 3
SB_PAIRS = 4

VMEM_LIMIT = 56 * 1024 * 1024


def _cparams(*sem):
    return pltpu.CompilerParams(dimension_semantics=sem, vmem_limit_bytes=VMEM_LIMIT)


def _resident(shape):
    nd = len(shape)
    return pl.BlockSpec(shape, lambda *_: (0,) * nd, pipeline_mode=pl.Buffered(1))


def _dot(a, b):
    return jnp.dot(a, b, preferred_element_type=F32)


def _dot_nt(a, b):
    return lax.dot_general(a, b, (((1,), (1,)), ((), ())), preferred_element_type=F32)


def _split_bf16(v):
    hi = v.astype(BF16)
    lo = (v - hi.astype(F32)).astype(BF16)
    return hi, lo


def _rmsnorm(x, g):
    return x * lax.rsqrt(jnp.mean(x * x, axis=-1, keepdims=True) + RMS_EPS) * g


def _ada_kernel(c_ref, w_ref, b_ref, o_ref):
    c = c_ref[...]
    a = (c * jax.nn.sigmoid(c)).astype(BF16)
    o_ref[...] = _dot(a, w_ref[...].astype(BF16)) + b_ref[...]


def _ada(c, w, b):
    bsz, d = c.shape
    n = w.shape[1]
    tn = d
    return pl.pallas_call(
        _ada_kernel,
        grid=(n // tn,),
        in_specs=[pl.BlockSpec((bsz, d), lambda j: (0, 0)),
                  pl.BlockSpec((d, tn), lambda j: (0, j)),
                  pl.BlockSpec((1, tn), lambda j: (0, j))],
        out_specs=pl.BlockSpec((bsz, tn), lambda j: (0, j)),
        out_shape=jax.ShapeDtypeStruct((bsz, n), F32),
        compiler_params=_cparams("arbitrary"),
        name="ada",
    )(c, w, b.reshape(1, n))


def _in_proj_kernel(x_ref, mod_ref, g_ref, w_ref, qa_ref, kva_ref, qkvb_ref, mg_ref, ga_ref, *, splits):
    x = x_ref[0]
    h = _rmsnorm(x, g_ref[...]) * (1.0 + mod_ref[0, 1:2, :]) + mod_ref[0, 0:1, :]
    hb = h.astype(BF16)
    n_q, n_kv, n_g, n_b, n_m = splits
    col = 0
    for ref, width, act in ((qa_ref, n_q, None), (kva_ref, n_kv, None), (ga_ref, n_g, "sig"),
                            (qkvb_ref, n_b, None), (mg_ref, n_m, "sig")):
        for c0 in range(0, width, IN_CHUNK):
            cw = min(IN_CHUNK, width - c0)
            r = _dot(hb, w_ref[:, col + c0:col + c0 + cw])
            if act == "sig":
                r = jax.nn.sigmoid(r)
            ref[0, :, c0:c0 + cw] = r.astype(ref.dtype)
        col += width


def _in_proj(x, mod3, g, w_all, splits):
    bsz, s, d = x.shape
    n_q, n_kv, n_g, n_b, n_m = splits
    tm = TM_IN
    out_shape = (jax.ShapeDtypeStruct((bsz, s, n_q), BF16),
                 jax.ShapeDtypeStruct((bsz, s, n_kv), BF16),
                 jax.ShapeDtypeStruct((bsz, s, n_b), BF16),
                 jax.ShapeDtypeStruct((bsz, s, n_m), BF16),
                 jax.ShapeDtypeStruct((bsz, s, n_g), F32))
    tok = lambda n: pl.BlockSpec((1, tm, n), lambda b, i: (b, i, 0))
    return pl.pallas_call(
        functools.partial(_in_proj_kernel, splits=splits),
        grid=(bsz, s // tm),
        in_specs=[tok(d),
                  pl.BlockSpec((1, 6, d), lambda b, i: (b, 0, 0)),
                  pl.BlockSpec((1, d), lambda b, i: (0, 0)),
                  _resident(w_all.shape)],
        out_specs=(tok(n_q), tok(n_kv), tok(n_b), tok(n_m), tok(n_g)),
        out_shape=out_shape,
        compiler_params=_cparams("parallel", "parallel"),
        name="in_proj",
    )(x, mod3, g.reshape(1, d), w_all)


def _gelu_tanh(x):
    return 0.5 * x * (1.0 + jnp.tanh(math.sqrt(2.0 / math.pi) * (x + 0.044715 * (x * x * x))))


def _compress_kernel(k16_ref, v16_ref, pk_ref, pv_ref, w1k_ref, w1v_ref, w2k_ref, w2v_ref, kc_ref, vc_ref):
    nc = k16_ref.shape[1]
    half = w1k_ref.shape[0] // 2

    def one(x_ref, p_ref, w1_ref, w2_ref, o_ref):
        xin = x_ref[0].astype(F32)
        top = (xin + p_ref[0:1, :]).astype(BF16)
        bot = (xin + p_ref[1:2, :]).astype(BF16)
        a = _dot(top, w1_ref[0:half, :])
        b = _dot(bot, w1_ref[half:, :])
        hid = a + pltpu.roll(b, nc - 1, 0)
        o_ref[0] = _dot(_gelu_tanh(hid).astype(BF16), w2_ref[...]).astype(o_ref.dtype)

    one(k16_ref, pk_ref, w1k_ref, w2k_ref, kc_ref)
    one(v16_ref, pv_ref, w1v_ref, w2v_ref, vc_ref)


def _compress(k16, v16, pk, pv, w1k, w1v, w2k, w2v):
    bsz, nc, wide = k16.shape
    blk = pl.BlockSpec((1, nc, wide), lambda b: (b, 0, 0))
    out = pl.BlockSpec((1, nc, LANES), lambda b: (b, 0, 0))
    full = lambda a: pl.BlockSpec(a.shape, lambda b: (0,) * a.ndim)
    return pl.pallas_call(
        _compress_kernel,
        grid=(bsz,),
        in_specs=[blk, blk, full(pk), full(pv), full(w1k), full(w1v), full(w2k), full(w2v)],
        out_specs=(out, out),
        out_shape=(jax.ShapeDtypeStruct((bsz, nc, LANES), BF16),) * 2,
        compiler_params=_cparams("parallel"),
        name="compress",
    )(k16, v16, pk, pv, w1k, w1v, w2k, w2v)


def _build_qaug(qa_ref, qx_ref, qaug_ref, tq):
    lo = lax.broadcasted_iota(jnp.int32, (tq, LANES), 1) < HEAD_DIM
    for g in range(NSA_GROUPS):
        for r in range(NSA_REP):
            h = g * NSA_REP + r
            qr = qa_ref[0, :, r * LANES:(r + 1) * LANES].astype(F32)
            qm = jnp.where(lo if g == 0 else jnp.logical_not(lo), qr, 0.0)
            qaug_ref[h * tq:(h + 1) * tq, 0:LANES] = qm.astype(BF16)
            qaug_ref[h * tq:(h + 1) * tq, LANES:2 * LANES] = jnp.broadcast_to(qx_ref[h:h + 1, :], (tq, LANES)).astype(BF16)


def _nsa_select_kernel(qa_ref, kc_ref, kxc_ref, ovl_ref, qx_ref, bias_ref, qaug_ref):
    tq = TQ_SELECT
    rows = NSA_HEADS * tq
    q0 = pl.program_id(1) * tq
    nc = kc_ref.shape[1]
    n_cmp = nc - 1
    _build_qaug(qa_ref, qx_ref, qaug_ref, tq)
    kc_aug = jnp.concatenate([kc_ref[0], kxc_ref[...]], axis=1)
    st = _dot_nt(kc_aug, qaug_ref[...])
    n_s = lax.broadcasted_iota(jnp.int32, (nc, rows), 0)
    t_s = q0 + (lax.broadcasted_iota(jnp.int32, (nc, rows), 1) & (tq - 1))
    sm = jnp.where((t_s >= n_s * CMP_STRIDE + (CMP_BLOCK - 1)) & (n_s < n_cmp), st, NEG_INF)
    e = jnp.exp2(sm - jnp.max(sm, axis=0, keepdims=True))
    p_t = jnp.where(t_s >= CMP_BLOCK - 1, e / jnp.sum(e, axis=0, keepdims=True), 0.0)
    nsel_rows = 32
    j_s = lax.broadcasted_iota(jnp.int32, (nsel_rows, tq), 0)
    cur = (q0 + lax.broadcasted_iota(jnp.int32, (nsel_rows, tq), 1)) // SEL_BLOCK
    valid_j = j_s <= cur
    forced = ((j_s == 0) | (j_s == cur) | (j_s == cur - 1)) & valid_j
    for g in range(NSA_GROUPS):
        base = g * NSA_REP * tq
        psum = (p_t[:, base:base + tq] + p_t[:, base + tq:base + 2 * tq]
                + p_t[:, base + 2 * tq:base + 3 * tq] + p_t[:, base + 3 * tq:base + 4 * tq])
        p_hi, p_lo = _split_bf16(psum)
        p_slc = _dot(ovl_ref[...], p_hi) + _dot(ovl_ref[...], p_lo)
        prio = p_slc[SEL_LANE0:SEL_LANE0 + nsel_rows]
        prio = jnp.where(valid_j, prio, NEG_INF)
        prio = jnp.where(forced, FORCE_PRIORITY, prio)
        chosen = jnp.zeros((nsel_rows, tq), jnp.bool_)
        for _ in range(SEL_TOPK):
            best = jnp.max(prio, axis=0, keepdims=True)
            first = jnp.min(jnp.where(prio == best, j_s, nsel_rows), axis=0, keepdims=True)
            pick = j_s == first
            chosen = chosen | pick
            prio = jnp.where(pick, -3e38, prio)
        bias_t = jnp.where(chosen, 0.0, NEG_INF)
        bias_t = jnp.concatenate([jnp.zeros((SEL_LANE0, tq), F32), bias_t,
                                  jnp.zeros((LANES - SEL_LANE0 - nsel_rows, tq), F32)], axis=0)
        for c0 in range(0, tq, LANES):
            bias_ref[0, c0:c0 + LANES, g * LANES:(g + 1) * LANES] = bias_t[:, c0:c0 + LANES].T.astype(BF16)


def _nsa_select(qa, kc, kxc, ovl, qx):
    bsz, s, nq = qa.shape
    nc = kc.shape[1]
    tq = TQ_SELECT
    const = lambda a: pl.BlockSpec(a.shape, lambda b, q: (0,) * a.ndim)
    return pl.pallas_call(
        _nsa_select_kernel,
        grid=(bsz, s // tq),
        in_specs=[pl.BlockSpec((1, tq, nq), lambda b, q: (b, q, 0)),
                  pl.BlockSpec((1, nc, LANES), lambda b, q: (b, 0, 0)),
                  const(kxc), const(ovl), const(qx)],
        out_specs=pl.BlockSpec((1, tq, NSA_GROUPS * LANES), lambda b, q: (b, q, 0)),
        out_shape=jax.ShapeDtypeStruct((bsz, s, NSA_GROUPS * LANES), BF16),
        scratch_shapes=[pltpu.VMEM((NSA_HEADS * tq, 2 * LANES), BF16)],
        compiler_params=_cparams("parallel", "parallel"),
        name="nsa_select",
    )(qa, kc, kxc, ovl, qx)


def _nsa_kernel(qa_ref, kc_ref, vc_ref, ksel_ref, vsel_ref, kwin_ref, vwin_ref, kx_ref, kxc_ref, qx_ref, bias_ref,
                ga_ref, o_ref, qaug_ref, qsel_ref, oacc_ref):
    tq = TQ_NSA
    nh = NSA_HEADS
    rows = nh * tq
    qi = pl.program_id(1)
    q0 = pl.multiple_of(qi * tq, tq)
    nc = kc_ref.shape[1]
    n_cmp = nc - 1

    lane = lax.broadcasted_iota(jnp.int32, (tq, LANES), 1)
    row = lax.broadcasted_iota(jnp.int32, (tq, LANES), 0)
    lo = lane < HEAD_DIM

    _build_qaug(qa_ref, qx_ref, qaug_ref, tq)
    for h in range(nh):
        g = h // NSA_REP
        qsel_ref[h * tq:(h + 1) * tq, 0:LANES] = qaug_ref[h * tq:(h + 1) * tq, 0:LANES]
        qsel_ref[h * tq:(h + 1) * tq, LANES:2 * LANES] = (
            bias_ref[0, :, g * LANES:(g + 1) * LANES].astype(F32) + qx_ref[h:h + 1, :]).astype(BF16)

    ga = ga_ref[0]

    def emit(o_heads, branch, first):
        for r in range(NSA_REP):
            c0 = r * 3 + branch
            c1 = NSA_REP * 3 + r * 3 + branch
            gt = jnp.where(lo, ga[:, c0:c0 + 1], ga[:, c1:c1 + 1])
            val = gt * jnp.where(lo, o_heads[r * tq:(r + 1) * tq], o_heads[(NSA_REP + r) * tq:(NSA_REP + r + 1) * tq])
            if first:
                oacc_ref[:, r * LANES:(r + 1) * LANES] = val
            else:
                oacc_ref[:, r * LANES:(r + 1) * LANES] += val

    def attend(q, k_aug, v, mask_fn):
        w = k_aug.shape[0]
        v_aug = jnp.concatenate([v, jnp.ones((w, LANES), BF16)], axis=1)
        part = rows // NSA_CHAINS
        outs = []
        for c in range(NSA_CHAINS):
            s3 = mask_fn(_dot_nt(q[c * part:(c + 1) * part], k_aug).reshape(part // tq, tq, w))
            p = jnp.exp2(s3 - jnp.max(s3, axis=-1, keepdims=True)).astype(BF16).reshape(part, w)
            pv = _dot(p, v_aug)
            outs.append(pv[:, 0:LANES] / pv[:, LANES:2 * LANES])
        return jnp.concatenate(outs, axis=0)

    qaug = qaug_ref[...]

    qsel = qsel_ref[...]
    last = q0 // SEL_CHUNK

    def branches(n_full):
        kc_aug = jnp.concatenate([kc_ref[0], kxc_ref[...]], axis=1)
        n_l = lax.broadcasted_iota(jnp.int32, (tq, nc), 1)
        t_l = q0 + lax.broadcasted_iota(jnp.int32, (tq, nc), 0)
        valid = (t_l >= n_l * CMP_STRIDE + (CMP_BLOCK - 1)) & (n_l < n_cmp)
        o_cmp = attend(qaug, kc_aug, vc_ref[0], lambda s3: jnp.where(valid[None], s3, NEG_INF))
        has_cmp = jnp.concatenate([q0 + row >= CMP_BLOCK - 1] * nh, axis=0)
        emit(jnp.where(has_cmp, o_cmp, 0.0), 0, True)

        span = WINDOW + tq
        w_start = pl.multiple_of(jnp.maximum(q0 - WINDOW, 0), tq)
        k_aug = jnp.concatenate([kwin_ref[0, pl.ds(w_start, span), :], kx_ref[pl.ds(w_start, span), :]], axis=1)
        dist = (q0 + lax.broadcasted_iota(jnp.int32, (tq, span), 0)) - (w_start + lax.broadcasted_iota(jnp.int32, (tq, span), 1))
        band = (dist >= 0) & (dist < WINDOW)
        emit(attend(qaug, k_aug, vwin_ref[0, pl.ds(w_start, span), :], lambda s3: jnp.where(band[None], s3, NEG_INF)), 2, False)

        nk = (n_full + 1) * SEL_CHUNK
        k_aug = jnp.concatenate([ksel_ref[0, 0:nk, :], kx_ref[0:nk, :]], axis=1)
        kpos = n_full * SEL_CHUNK + lax.broadcasted_iota(jnp.int32, (tq, SEL_CHUNK), 1)
        tpos = q0 + lax.broadcasted_iota(jnp.int32, (tq, SEL_CHUNK), 0)
        causal = (kpos <= tpos)[None]

        def mask_fn(s3):
            diag = jnp.where(causal, s3[:, :, n_full * SEL_CHUNK:nk], NEG_INF)
            return jnp.concatenate([s3[:, :, 0:n_full * SEL_CHUNK], diag], axis=2) if n_full else diag

        emit(attend(qsel, k_aug, vsel_ref[0, 0:nk, :], mask_fn), 1, False)

    for n_full in range(ksel_ref.shape[1] // SEL_CHUNK):
        pl.when(last == n_full)(functools.partial(branches, n_full))

    o_ref[0] = oacc_ref[...].astype(o_ref.dtype)


def _nsa(qa, kva, kc, vc, ga, kx, kxc, qx, selbias):
    bsz, s, nq = qa.shape
    nc = kc.shape[1]
    tq = TQ_NSA
    rows = NSA_HEADS * tq
    kv_blk = lambda i: pl.BlockSpec((1, s, LANES), lambda b, q, i=i: (b, 0, i))
    cmp_blk = pl.BlockSpec((1, nc, LANES), lambda b, q: (b, 0, 0))
    const = lambda a: pl.BlockSpec(a.shape, lambda b, q: (0,) * a.ndim)
    return pl.pallas_call(
        _nsa_kernel,
        grid=(bsz, s // tq),
        in_specs=[pl.BlockSpec((1, tq, nq), lambda b, q: (b, q, 0)),
                  cmp_blk, cmp_blk, kv_blk(2), kv_blk(3), kv_blk(4), kv_blk(5),
                  const(kx), const(kxc), const(qx),
                  pl.BlockSpec((1, tq, NSA_GROUPS * LANES), lambda b, q: (b, q, 0)),
                  pl.BlockSpec((1, tq, GATE_PAD), lambda b, q: (b, q, 0))],
        out_specs=pl.BlockSpec((1, tq, nq), lambda b, q: (b, q, 0)),
        out_shape=jax.ShapeDtypeStruct((bsz, s, nq), BF16),
        scratch_shapes=[pltpu.VMEM((rows, 2 * LANES), BF16),
                        pltpu.VMEM((rows, 2 * LANES), BF16),
                        pltpu.VMEM((tq, nq), F32)],
        compiler_params=_cparams("parallel", "arbitrary"),
        name="nsa",
    )(qa, kc, vc, kva, kva, kva, kva, kx, kxc, qx, selbias, ga)


def _sb_kernel(q_ref, k_ref, v_ref, tri_ref, o_ref, carry_ref, acc_ref):
    tq, tk = TQ_SB, TK_SB
    qi = pl.program_id(2)
    lane = lax.broadcasted_iota(jnp.int32, (tq, LANES), 1)
    lo = lane < HEAD_DIM
    ntri = tri_ref[0:tk, :]
    pairs = range(SB_PAIRS)
    cols = [slice(p * LANES, (p + 1) * LANES) for p in pairs]
    rows = [slice(p * 2 * tq, (p + 1) * 2 * tq) for p in pairs]
    qs = []
    for p in pairs:
        q2 = q_ref[0, :, cols[p]].astype(F32)
        qs.append(jnp.concatenate([jnp.where(lo, q2, 0.0), jnp.where(lo, 0.0, q2)], axis=0).astype(BF16))

    def stick(z):
        sp = jnp.maximum(z, 0.0) + jnp.log2(1.0 + jnp.exp2(-jnp.abs(z)))
        return sp, z - sp

    kpos = lax.broadcasted_iota(jnp.int32, (tq, tk), 1)
    tpos = lax.broadcasted_iota(jnp.int32, (tq, tk), 0)
    keep = jnp.concatenate([kpos < tpos] * 2, axis=0)

    def first_block(nt):
        w = nt * tk
        d0 = w - tk
        k_start = pl.multiple_of((qi - (nt - 1)) * tk, tk)
        for p in pairs:
            sp, ls = stick(_dot_nt(qs[p], k_ref[0, pl.ds(k_start, w), cols[p]]))
            sp_d = jnp.where(keep, sp[:, d0:w], 0.0)
            sp_b = (jnp.concatenate([sp[:, 0:d0], sp_d], axis=1) if nt > 1 else sp_d).astype(BF16)
            a_parts = []
            for i in range(nt):
                later = _dot(sp_b[:, i * tk:w], tri_ref[0:w - i * tk, :])
                a = jnp.exp2(ls[:, i * tk:(i + 1) * tk] + later)
                a_parts.append(jnp.where(keep, a, 0.0) if i == nt - 1 else a)
                if i == 0:
                    carry_ref[rows[p], :] = later[:, 0:1] - sp_b[:, 0:1].astype(F32)
            a = (jnp.concatenate(a_parts, axis=1) if nt > 1 else a_parts[0]).astype(BF16)
            acc_ref[rows[p], :] = _dot(a, v_ref[0, pl.ds(k_start, w), cols[p]])

    for nt in range(1, SB_FIRST_TILES):
        pl.when(qi == nt - 1)(functools.partial(first_block, nt))

    @pl.when(qi >= SB_FIRST_TILES - 1)
    def _():
        first_block(SB_FIRST_TILES)

        for p in pairs:
            def tile(kt, p=p):
                k0 = pl.multiple_of(kt * tk, tk)
                sp, ls = stick(_dot_nt(qs[p], k_ref[0, pl.ds(k0, tk), cols[p]]))
                later = _dot(sp.astype(BF16), ntri)
                pv = _dot(jnp.exp2(ls + later).astype(BF16), v_ref[0, pl.ds(k0, tk), cols[p]])
                carry = carry_ref[rows[p], :]
                acc_ref[rows[p], :] += jnp.exp2(carry) * pv
                carry_ref[rows[p], :] = carry - jnp.sum(sp, axis=-1, keepdims=True)

            def live(p=p):
                return jnp.max(carry_ref[rows[p], :]) > SB_DEAD_LOG2

            def cond(st):
                return (st[0] >= 0) & st[1]

            def body(st, tile=tile, live=live):
                tile(st[0])
                return st[0] - 1, live()

            lax.while_loop(cond, body, (qi - SB_FIRST_TILES, live()))

    for p in pairs:
        r0 = p * 2 * tq
        o_ref[0, :, cols[p]] = jnp.where(lo, acc_ref[r0:r0 + tq], acc_ref[r0 + tq:r0 + 2 * tq]).astype(o_ref.dtype)


def _sb(qkvb, tri):
    bsz, s, n3 = qkvb.shape
    nblk = SB_HEADS // 2 // SB_PAIRS
    wide = SB_PAIRS * LANES
    tq = TQ_SB
    return pl.pallas_call(
        _sb_kernel,
        grid=(bsz, nblk, s // tq),
        in_specs=[pl.BlockSpec((1, tq, wide), lambda b, p, q: (b, q, p)),
                  pl.BlockSpec((1, s, wide), lambda b, p, q: (b, 0, nblk + p)),
                  pl.BlockSpec((1, s, wide), lambda b, p, q: (b, 0, 2 * nblk + p)),
                  pl.BlockSpec(tri.shape, lambda b, p, q: (0, 0))],
        out_specs=pl.BlockSpec((1, tq, wide), lambda b, p, q: (b, q, p)),
        out_shape=jax.ShapeDtypeStruct((bsz, s, nblk * wide), BF16),
        scratch_shapes=[pltpu.VMEM((SB_PAIRS * 2 * tq, 1), F32),
                        pltpu.VMEM((SB_PAIRS * 2 * tq, LANES), F32)],
        compiler_params=_cparams("parallel", "parallel", "arbitrary"),
        name="sb",
    )(qkvb, qkvb, qkvb, tri)


def _post_kernel(x_ref, oa_ref, ob_ref, mg_ref, mod_ref, wpa_ref, wpb_ref, wo_ref, g2_ref,
                 wg_ref, wu_ref, wd_ref, gf_ref, out_ref):
    d = x_ref.shape[2]
    part_rows = x_ref.shape[1] // POST_CHAINS
    for i in range(POST_CHAINS):
        rs = slice(i * part_rows, (i + 1) * part_rows)
        x = x_ref[0, rs, :]
        ya = _dot(oa_ref[0, rs, :], wpa_ref[...])
        yb = _dot(ob_ref[0, rs, :], wpb_ref[...])
        y = mg_ref[0, rs, 0:d].astype(F32) * ya + mg_ref[0, rs, d:2 * d].astype(F32) * yb
        x1 = x + mod_ref[0, 2:3, :] * _dot(y.astype(BF16), wo_ref[...])
        h2 = (_rmsnorm(x1, g2_ref[...]) * (1.0 + mod_ref[0, 4:5, :]) + mod_ref[0, 3:4, :]).astype(BF16)
        ffn = None
        for c0 in range(0, wg_ref.shape[1], FF_CHUNK):
            gte = _dot(h2, wg_ref[:, c0:c0 + FF_CHUNK])
            up = _dot(h2, wu_ref[:, c0:c0 + FF_CHUNK])
            act = (gte * jax.nn.sigmoid(gte) * up).astype(BF16)
            part = _dot(act, wd_ref[c0:c0 + FF_CHUNK, :])
            ffn = part if ffn is None else ffn + part
        x2 = x1 + mod_ref[0, 5:6, :] * ffn
        out_ref[0, rs, :] = _rmsnorm(x2, gf_ref[...])


def _post(x, oa, ob, mg, mod3, wpa, wpb, wo, g2, wg3, wu3, wd3, gf):
    bsz, s, d = x.shape
    tm = TM_POST
    tok = lambda n: pl.BlockSpec((1, tm, n), lambda b, i: (b, i, 0))
    return pl.pallas_call(
        _post_kernel,
        grid=(bsz, s // tm),
        in_specs=[tok(d), tok(oa.shape[2]), tok(ob.shape[2]), tok(mg.shape[2]),
                  pl.BlockSpec((1, 6, d), lambda b, i: (b, 0, 0)),
                  _resident(wpa.shape), _resident(wpb.shape), _resident(wo.shape),
                  pl.BlockSpec((1, d), lambda b, i: (0, 0)),
                  _resident(wg3.shape), _resident(wu3.shape), _resident(wd3.shape),
                  pl.BlockSpec((1, d), lambda b, i: (0, 0))],
        out_specs=tok(d),
        out_shape=jax.ShapeDtypeStruct((bsz, s, d), F32),
        compiler_params=_cparams("parallel", "parallel"),
        name="post",
    )(x, oa, ob, mg, mod3, wpa, wpb, wo, g2.reshape(1, d), wg3, wu3, wd3, gf.reshape(1, d))


def _bf16_pieces(x, n=3):
    out, rem = [], float(x)
    for _ in range(n):
        p = float(np.asarray(rem, np.float32).astype(jnp.bfloat16).astype(np.float32))
        out.append(p)
        rem -= p
    return out


N_PIECES = 3


def _key_extras(pos):
    lane = jnp.arange(LANES)[None, :]
    a = (pos // SEL_BLOCK)[:, None]
    b = (pos % SEL_BLOCK)[:, None]
    ext = jnp.where(lane < N_PIECES, a, jnp.where(lane < 2 * N_PIECES, b, 0)).astype(F32)
    return ext, a


def _query_extras():
    qx = np.zeros((NSA_HEADS, LANES), np.float32)
    for h in range(NSA_HEADS):
        slope = 2.0 ** (-(h + 1))
        for i, p in enumerate(_bf16_pieces(LOG2E, N_PIECES)):
            qx[h, i] = SEL_BLOCK * slope * p
            qx[h, N_PIECES + i] = slope * p
    return jnp.asarray(qx)


def kernel(x, c, w_ada, b_ada, norm_mix_g, w_in, cmp_pos_k, cmp_w1_k, cmp_w2_k, cmp_pos_v, cmp_w1_v, cmp_w2_v,
           w_proj_a, w_proj_b, w_out, norm_ffn_g, w_ffn_gate, w_ffn_up, w_ffn_down, norm_final_g):
    bsz, s, d = x.shape
    depth = w_ada.shape[0]
    assert s % SEL_CHUNK == 0 and s // SEL_BLOCK <= 32 and s % TM_IN == 0 and s >= WINDOW + TQ_NSA
    n_q = NSA_HEADS * HEAD_DIM
    n_kv = 6 * NSA_GROUPS * HEAD_DIM
    n_gate = 3 * NSA_HEADS
    n_b = 3 * SB_HEADS * HEAD_DIM
    n_m = 2 * d
    nc = s // CMP_STRIDE
    d_ff = w_ffn_gate.shape[2]
    assert d_ff % FF_CHUNK == 0

    pos = jnp.arange(s)
    ext, blk = _key_extras(pos)
    lane = jnp.arange(LANES)[None, :]
    kx = jnp.where(lane == SEL_LANE0 + blk, 1.0, ext).astype(BF16)
    kxc = _key_extras(jnp.arange(nc) * CMP_STRIDE + CMP_BLOCK - 1)[0].astype(BF16)
    jrow = jnp.arange(LANES)[:, None] - SEL_LANE0
    ncol = jnp.arange(nc)[None, :]
    ovl = ((ncol * CMP_STRIDE <= jrow * SEL_BLOCK + SEL_BLOCK - 1)
           & (ncol * CMP_STRIDE + CMP_BLOCK - 1 >= jrow * SEL_BLOCK)
           & (jrow >= 0) & (jrow < s // SEL_BLOCK) & (ncol < nc - 1)).astype(BF16)
    qx = _query_extras()
    jj = jnp.arange(TK_SB)
    tri = -jnp.concatenate([jj[:, None] > jj[None, :]] + [jnp.ones((TK_SB, TK_SB), jnp.bool_)] * (SB_FIRST_TILES - 1),
                           axis=0).astype(BF16)

    for l in range(depth):
        w = w_in[l]
        wq = (w[:, :n_q] * (ATTN_SCALE * LOG2E)).reshape(d, NSA_GROUPS, NSA_REP, HEAD_DIM).transpose(0, 2, 1, 3).reshape(d, n_q)
        o1 = n_q + n_kv
        o2 = o1 + n_gate
        o3 = o2 + n_b
        wgate = jnp.pad(w[:, o1:o2], ((0, 0), (0, GATE_PAD - n_gate)))
        wb = jnp.concatenate([w[:, o2:o2 + n_b // 3] * (ATTN_SCALE * LOG2E), w[:, o2 + n_b // 3:o3]], axis=1)
        w_all = jnp.concatenate([wq, w[:, n_q:o1], wgate, wb, w[:, o3:]], axis=1).astype(BF16)
        splits = (n_q, n_kv, GATE_PAD, n_b, n_m)

        def cmp_weights(pos_emb, w1, w2):
            half = CMP_BLOCK // 2
            w1r = w1.reshape(2, half, HEAD_DIM, CMP_HIDDEN)
            z = jnp.zeros_like(w1r)
            g0 = jnp.concatenate([w1r, z], axis=3)
            g1 = jnp.concatenate([z, w1r], axis=3)
            w1p = jnp.stack([g0, g1], axis=2).reshape(2 * half * 2 * HEAD_DIM, 2 * CMP_HIDDEN)
            zz = jnp.zeros_like(w2)
            w2p = jnp.concatenate([jnp.concatenate([w2, zz], axis=1), jnp.concatenate([zz, w2], axis=1)], axis=0)
            pe = pos_emb.reshape(2, half, 1, HEAD_DIM)
            pe = jnp.broadcast_to(pe, (2, half, 2, HEAD_DIM)).reshape(2, half * 2 * HEAD_DIM)
            return pe, w1p.astype(BF16), w2p.astype(BF16)

        pk, w1k, w2k = cmp_weights(cmp_pos_k[l], cmp_w1_k[l], cmp_w2_k[l])
        pv, w1v, w2v = cmp_weights(cmp_pos_v[l], cmp_w1_v[l], cmp_w2_v[l])
        wpa = w_proj_a[l].reshape(NSA_GROUPS, NSA_REP, HEAD_DIM, d).transpose(1, 0, 2, 3).reshape(n_q, d).astype(BF16)
        wpb = w_proj_b[l].astype(BF16)
        wo = w_out[l].astype(BF16)
        wg3 = w_ffn_gate[l].astype(BF16)
        wu3 = w_ffn_up[l].astype(BF16)
        wd3 = w_ffn_down[l].astype(BF16)

        mod3 = _ada(c, w_ada[l], b_ada[l]).reshape(bsz, 6, d)
        qa, kva, qkvb, mg, ga = _in_proj(x, mod3, norm_mix_g[l], w_all, splits)
        k16 = kva[:, :, 0:LANES].reshape(bsz, nc, CMP_STRIDE * LANES)
        v16 = kva[:, :, LANES:2 * LANES].reshape(bsz, nc, CMP_STRIDE * LANES)
        kc, vc = _compress(k16, v16, pk, pv, w1k, w1v, w2k, w2v)
        selbias = _nsa_select(qa, kc, kxc, ovl, qx)
        oa = _nsa(qa, kva, kc, vc, ga, kx, kxc, qx, selbias)
        ob = _sb(qkvb, tri)
        gf = norm_final_g if l == depth - 1 else jnp.ones_like(norm_final_g)
        x = _post(x, oa, ob, mg, mod3, wpa, wpb, wo, norm_ffn_g[l], wg3, wu3, wd3, gf)
        assert depth == 1, "final norm is fused into the last layer's post kernel"
    return x
```

```python
import functools
import math

import jax
import jax.numpy as jnp
import numpy as np
from jax import lax
from jax.experimental import pallas as pl
from jax.experimental.pallas import tpu as pltpu

F32 = jnp.float32
BF16 = jnp.bfloat16

HEAD_DIM = 64
NSA_HEADS = 8
NSA_GROUPS = 2
NSA_REP = NSA_HEADS // NSA_GROUPS
SB_HEADS = 8
CMP_BLOCK = 32
CMP_STRIDE = 16
CMP_HIDDEN = 2 * HEAD_DIM
SEL_BLOCK = 64
SEL_TOPK = 8
WINDOW = 512
RMS_EPS = 1e-6
NEG_INF = -1e30
FORCE_PRIORITY = 1e4
ATTN_SCALE = 1.0 / math.sqrt(HEAD_DIM)
LOG2E = math.log2(math.e)

LANES = 128
PAIR = 2 * HEAD_DIM
SEL_LANE0 = 32
GATE_PAD = 128

TM_IN = 1024
IN_CHUNK = 512
IN_CHAINS = 1
TM_POST = 512
POST_CHAINS = 1
TQ_NSA = 256
TQ_SELECT = 1024
NSA_CHAINS = 4
SEL_CHUNK = 512
TQ_SB = 256
TK_SB = 256
FF_CHUNK = 256
SB_DEAD_LOG2 = -150.0
SB_FIRST_TILES = 3
SB_PAIRS = 4

VMEM_LIMIT = 56 * 1024 * 1024


def _cparams(*sem):
    return pltpu.CompilerParams(dimension_semantics=sem, vmem_limit_bytes=VMEM_LIMIT)


def _resident(shape):
    nd = len(shape)
    return pl.BlockSpec(shape, lambda *_: (0,) * nd, pipeline_mode=pl.Buffered(1))


def _dot(a, b):
    return jnp.dot(a, b, preferred_element_type=F32)


def _dot_nt(a, b):
    return lax.dot_general(a, b, (((1,), (1,)), ((), ())), preferred_element_type=F32)


def _split_bf16(v):
    hi = v.astype(BF16)
    lo = (v - hi.astype(F32)).astype(BF16)
    return hi, lo


def _rmsnorm(x, g):
    return x * lax.rsqrt(jnp.mean(x * x, axis=-1, keepdims=True) + RMS_EPS) * g


def _ada_kernel(c_ref, w_ref, b_ref, o_ref):
    c = c_ref[...]
    a = (c * jax.nn.sigmoid(c)).astype(BF16)
    o_ref[...] = _dot(a, w_ref[...].astype(BF16)) + b_ref[...]


def _ada(c, w, b):
    bsz, d = c.shape
    n = w.shape[1]
    tn = d
    return pl.pallas_call(
        _ada_kernel,
        grid=(n // tn,),
        in_specs=[pl.BlockSpec((bsz, d), lambda j: (0, 0)),
                  pl.BlockSpec((d, tn), lambda j: (0, j)),
                  pl.BlockSpec((1, tn), lambda j: (0, j))],
        out_specs=pl.BlockSpec((bsz, tn), lambda j: (0, j)),
        out_shape=jax.ShapeDtypeStruct((bsz, n), F32),
        compiler_params=_cparams("arbitrary"),
        name="ada",
    )(c, w, b.reshape(1, n))


def _in_proj_kernel(x_ref, mod_ref, g_ref, w_ref, qa_ref, kva_ref, qkvb_ref, mg_ref, ga_ref, *, splits):
    n_q, n_kv, n_g, n_b, n_m = splits
    part = x_ref.shape[1] // IN_CHAINS
    for i in range(IN_CHAINS):
        rs = slice(i * part, (i + 1) * part)
        h = _rmsnorm(x_ref[0, rs, :], g_ref[...]) * (1.0 + mod_ref[0, 1:2, :]) + mod_ref[0, 0:1, :]
        hb = h.astype(BF16)
        col = 0
        for ref, width, act in ((qa_ref, n_q, None), (kva_ref, n_kv, None), (ga_ref, n_g, "sig"),
                                (qkvb_ref, n_b, None), (mg_ref, n_m, "sig")):
            for c0 in range(0, width, IN_CHUNK):
                cw = min(IN_CHUNK, width - c0)
                r = _dot(hb, w_ref[:, col + c0:col + c0 + cw])
                if act == "sig":
                    r = jax.nn.sigmoid(r)
                ref[0, rs, c0:c0 + cw] = r.astype(ref.dtype)
            col += width


def _in_proj(x, mod3, g, w_all, splits):
    bsz, s, d = x.shape
    n_q, n_kv, n_g, n_b, n_m = splits
    tm = TM_IN
    out_shape = (jax.ShapeDtypeStruct((bsz, s, n_q), BF16),
                 jax.ShapeDtypeStruct((bsz, s, n_kv), BF16),
                 jax.ShapeDtypeStruct((bsz, s, n_b), BF16),
                 jax.ShapeDtypeStruct((bsz, s, n_m), BF16),
                 jax.ShapeDtypeStruct((bsz, s, n_g), F32))
    tok = lambda n: pl.BlockSpec((1, tm, n), lambda b, i: (b, i, 0))
    return pl.pallas_call(
        functools.partial(_in_proj_kernel, splits=splits),
        grid=(bsz, s // tm),
        in_specs=[tok(d),
                  pl.BlockSpec((1, 6, d), lambda b, i: (b, 0, 0)),
                  pl.BlockSpec((1, d), lambda b, i: (0, 0)),
                  _resident(w_all.shape)],
        out_specs=(tok(n_q), tok(n_kv), tok(n_b), tok(n_m), tok(n_g)),
        out_shape=out_shape,
        compiler_params=_cparams("parallel", "parallel"),
        name="in_proj",
    )(x, mod3, g.reshape(1, d), w_all)


def _gelu_tanh(x):
    return 0.5 * x * (1.0 + jnp.tanh(math.sqrt(2.0 / math.pi) * (x + 0.044715 * (x * x * x))))


def _compress_kernel(k_ref, v_ref, pk_ref, pv_ref, w1k_ref, w1v_ref, w2k_ref, w2v_ref, kc_ref, vc_ref, x32_ref):
    s = k_ref.shape[1]
    nc = s // CMP_STRIDE
    half = w1k_ref.shape[0] // 2

    def one(x_ref, p_ref, w1_ref, w2_ref, o_ref):
        x32_ref[...] = x_ref[0].astype(F32)
        a = b = None
        for l in range(CMP_STRIDE):
            xl = x32_ref[pl.ds(l, nc, stride=CMP_STRIDE), :]
            rows = slice(l * LANES, (l + 1) * LANES)
            top = (xl + p_ref[0:1, rows]).astype(BF16)
            bot = (xl + p_ref[1:2, rows]).astype(BF16)
            da = _dot(top, w1_ref[rows, :])
            db = _dot(bot, w1_ref[half + l * LANES:half + (l + 1) * LANES, :])
            a = da if a is None else a + da
            b = db if b is None else b + db
        hid = a + pltpu.roll(b, nc - 1, 0)
        o_ref[0] = _dot(_gelu_tanh(hid).astype(BF16), w2_ref[...]).astype(o_ref.dtype)

    one(k_ref, pk_ref, w1k_ref, w2k_ref, kc_ref)
    one(v_ref, pv_ref, w1v_ref, w2v_ref, vc_ref)


def _compress(kva, pk, pv, w1k, w1v, w2k, w2v):
    bsz, s, _ = kva.shape
    nc = s // CMP_STRIDE
    blk = lambda i: pl.BlockSpec((1, s, LANES), lambda b, i=i: (b, 0, i))
    out = pl.BlockSpec((1, nc, LANES), lambda b: (b, 0, 0))
    full = lambda a: pl.BlockSpec(a.shape, lambda b: (0,) * a.ndim)
    return pl.pallas_call(
        _compress_kernel,
        grid=(bsz,),
        in_specs=[blk(0), blk(1), full(pk), full(pv), full(w1k), full(w1v), full(w2k), full(w2v)],
        out_specs=(out, out),
        out_shape=(jax.ShapeDtypeStruct((bsz, nc, LANES), BF16),) * 2,
        scratch_shapes=[pltpu.VMEM((s, LANES), F32)],
        compiler_params=_cparams("parallel"),
        name="compress",
    )(kva, kva, pk, pv, w1k, w1v, w2k, w2v)


def _build_qaug(qa_ref, qx_ref, qaug_ref, tq):
    lo = lax.broadcasted_iota(jnp.int32, (tq, LANES), 1) < HEAD_DIM
    for g in range(NSA_GROUPS):
        for r in range(NSA_REP):
            h = g * NSA_REP + r
            qr = qa_ref[0, :, r * LANES:(r + 1) * LANES].astype(F32)
            qm = jnp.where(lo if g == 0 else jnp.logical_not(lo), qr, 0.0)
            qaug_ref[h * tq:(h + 1) * tq, 0:LANES] = qm.astype(BF16)
            qaug_ref[h * tq:(h + 1) * tq, LANES:2 * LANES] = jnp.broadcast_to(qx_ref[h:h + 1, :], (tq, LANES)).astype(BF16)


def _nsa_select_kernel(qa_ref, kc_ref, kxc_ref, ovl_ref, qx_ref, cmask_ref, bias_ref, qaug_ref):
    tq = TQ_SELECT
    q0 = pl.program_id(1) * tq
    _build_qaug(qa_ref, qx_ref, qaug_ref, tq)
    kc_aug = jnp.concatenate([kc_ref[0], kxc_ref[...]], axis=1)
    st = _dot_nt(kc_aug, qaug_ref[...])
    cmask = cmask_ref[...]
    sm = st + jnp.concatenate([cmask] * NSA_HEADS, axis=1)
    e = jnp.exp2(sm - jnp.max(sm, axis=0, keepdims=True))
    has_any = jnp.concatenate([cmask[0:1, :] == 0.0] * NSA_HEADS, axis=1)
    p_t = e * jnp.where(has_any, 1.0 / jnp.sum(e, axis=0, keepdims=True), 0.0)
    nsel_rows = 32
    j_s = lax.broadcasted_iota(jnp.int32, (nsel_rows, tq), 0)
    cur = (q0 + lax.broadcasted_iota(jnp.int32, (nsel_rows, tq), 1)) // SEL_BLOCK
    valid_j = j_s <= cur
    forced = ((j_s == 0) | (j_s == cur) | (j_s == cur - 1)) & valid_j
    for g in range(NSA_GROUPS):
        base = g * NSA_REP * tq
        psum = (p_t[:, base:base + tq] + p_t[:, base + tq:base + 2 * tq]
                + p_t[:, base + 2 * tq:base + 3 * tq] + p_t[:, base + 3 * tq:base + 4 * tq])
        p_hi, p_lo = _split_bf16(psum)
        p_slc = _dot(ovl_ref[...], p_hi) + _dot(ovl_ref[...], p_lo)
        prio = p_slc[SEL_LANE0:SEL_LANE0 + nsel_rows]
        prio = jnp.where(valid_j, prio, NEG_INF)
        prio = jnp.where(forced, FORCE_PRIORITY, prio)
        chosen = jnp.zeros((nsel_rows, tq), jnp.bool_)
        for _ in range(SEL_TOPK):
            best = jnp.max(prio, axis=0, keepdims=True)
            first = jnp.min(jnp.where(prio == best, j_s, nsel_rows), axis=0, keepdims=True)
            pick = j_s == first
            chosen = chosen | pick
            prio = jnp.where(pick, -3e38, prio)
        bias_t = jnp.where(chosen, 0.0, NEG_INF)
        bias_t = jnp.concatenate([jnp.zeros((SEL_LANE0, tq), F32), bias_t,
                                  jnp.zeros((LANES - SEL_LANE0 - nsel_rows, tq), F32)], axis=0)
        for c0 in range(0, tq, LANES):
            bias_ref[0, c0:c0 + LANES, g * LANES:(g + 1) * LANES] = bias_t[:, c0:c0 + LANES].T.astype(BF16)


def _nsa_select(qa, kc, kxc, ovl, qx, cmask):
    bsz, s, nq = qa.shape
    nc = kc.shape[1]
    tq = TQ_SELECT
    const = lambda a: pl.BlockSpec(a.shape, lambda b, q: (0,) * a.ndim)
    return pl.pallas_call(
        _nsa_select_kernel,
        grid=(bsz, s // tq),
        in_specs=[pl.BlockSpec((1, tq, nq), lambda b, q: (b, q, 0)),
                  pl.BlockSpec((1, nc, LANES), lambda b, q: (b, 0, 0)),
                  const(kxc), const(ovl), const(qx),
                  pl.BlockSpec((nc, tq), lambda b, q: (0, q))],
        out_specs=pl.BlockSpec((1, tq, NSA_GROUPS * LANES), lambda b, q: (b, q, 0)),
        out_shape=jax.ShapeDtypeStruct((bsz, s, NSA_GROUPS * LANES), BF16),
        scratch_shapes=[pltpu.VMEM((NSA_HEADS * tq, 2 * LANES), BF16)],
        compiler_params=_cparams("parallel", "parallel"),
        name="nsa_select",
    )(qa, kc, kxc, ovl, qx, cmask)


def _nsa_kernel(qa_ref, kc_ref, vc_ref, ksel_ref, vsel_ref, kwin_ref, vwin_ref, kx_ref, kxc_ref, qx_ref, bias_ref,
                ga_ref, o_ref, qaug_ref, qsel_ref, oacc_ref):
    tq = TQ_NSA
    nh = NSA_HEADS
    rows = nh * tq
    qi = pl.program_id(1)
    q0 = pl.multiple_of(qi * tq, tq)
    nc = kc_ref.shape[1]
    n_cmp = nc - 1

    lane = lax.broadcasted_iota(jnp.int32, (tq, LANES), 1)
    row = lax.broadcasted_iota(jnp.int32, (tq, LANES), 0)
    lo = lane < HEAD_DIM

    _build_qaug(qa_ref, qx_ref, qaug_ref, tq)
    for h in range(nh):
        g = h // NSA_REP
        qsel_ref[h * tq:(h + 1) * tq, 0:LANES] = qaug_ref[h * tq:(h + 1) * tq, 0:LANES]
        qsel_ref[h * tq:(h + 1) * tq, LANES:2 * LANES] = (
            bias_ref[0, :, g * LANES:(g + 1) * LANES].astype(F32) + qx_ref[h:h + 1, :]).astype(BF16)

    ga = ga_ref[0]

    def emit(o_heads, branch, first):
        for r in range(NSA_REP):
            c0 = r * 3 + branch
            c1 = NSA_REP * 3 + r * 3 + branch
            gt = jnp.where(lo, ga[:, c0:c0 + 1], ga[:, c1:c1 + 1])
            val = gt * jnp.where(lo, o_heads[r * tq:(r + 1) * tq], o_heads[(NSA_REP + r) * tq:(NSA_REP + r + 1) * tq])
            if first:
                oacc_ref[:, r * LANES:(r + 1) * LANES] = val
            else:
                oacc_ref[:, r * LANES:(r + 1) * LANES] += val

    def attend(q, k_aug, v, mask_fn):
        w = k_aug.shape[0]
        v_aug = jnp.concatenate([v, jnp.ones((w, LANES), BF16)], axis=1)
        part = rows // NSA_CHAINS
        outs = []
        for c in range(NSA_CHAINS):
            s3 = mask_fn(_dot_nt(q[c * part:(c + 1) * part], k_aug).reshape(part // tq, tq, w))
            p = jnp.exp2(s3 - jnp.max(s3, axis=-1, keepdims=True)).astype(BF16).reshape(part, w)
            pv = _dot(p, v_aug)
            outs.append(pv[:, 0:LANES] / pv[:, LANES:2 * LANES])
        return jnp.concatenate(outs, axis=0)

    qaug = qaug_ref[...]

    qsel = qsel_ref[...]
    last = q0 // SEL_CHUNK

    def branches(n_full):
        kc_aug = jnp.concatenate([kc_ref[0], kxc_ref[...]], axis=1)
        n_l = lax.broadcasted_iota(jnp.int32, (tq, nc), 1)
        t_l = q0 + lax.broadcasted_iota(jnp.int32, (tq, nc), 0)
        valid = (t_l >= n_l * CMP_STRIDE + (CMP_BLOCK - 1)) & (n_l < n_cmp)
        o_cmp = attend(qaug, kc_aug, vc_ref[0], lambda s3: jnp.where(valid[None], s3, NEG_INF))
        has_cmp = jnp.concatenate([q0 + row >= CMP_BLOCK - 1] * nh, axis=0)
        emit(jnp.where(has_cmp, o_cmp, 0.0), 0, True)

        span = WINDOW + tq
        w_start = pl.multiple_of(jnp.maximum(q0 - WINDOW, 0), tq)
        k_aug = jnp.concatenate([kwin_ref[0, pl.ds(w_start, span), :], kx_ref[pl.ds(w_start, span), :]], axis=1)
        dist = (q0 + lax.broadcasted_iota(jnp.int32, (tq, span), 0)) - (w_start + lax.broadcasted_iota(jnp.int32, (tq, span), 1))
        band = (dist >= 0) & (dist < WINDOW)
        emit(attend(qaug, k_aug, vwin_ref[0, pl.ds(w_start, span), :], lambda s3: jnp.where(band[None], s3, NEG_INF)), 2, False)

        nk = (n_full + 1) * SEL_CHUNK
        k_aug = jnp.concatenate([ksel_ref[0, 0:nk, :], kx_ref[0:nk, :]], axis=1)
        kpos = n_full * SEL_CHUNK + lax.broadcasted_iota(jnp.int32, (tq, SEL_CHUNK), 1)
        tpos = q0 + lax.broadcasted_iota(jnp.int32, (tq, SEL_CHUNK), 0)
        causal = (kpos <= tpos)[None]

        def mask_fn(s3):
            diag = jnp.where(causal, s3[:, :, n_full * SEL_CHUNK:nk], NEG_INF)
            return jnp.concatenate([s3[:, :, 0:n_full * SEL_CHUNK], diag], axis=2) if n_full else diag

        emit(attend(qsel, k_aug, vsel_ref[0, 0:nk, :], mask_fn), 1, False)

    for n_full in range(ksel_ref.shape[1] // SEL_CHUNK):
        pl.when(last == n_full)(functools.partial(branches, n_full))

    o_ref[0] = oacc_ref[...].astype(o_ref.dtype)


def _nsa(qa, kva, kc, vc, ga, kx, kxc, qx, selbias):
    bsz, s, nq = qa.shape
    nc = kc.shape[1]
    tq = TQ_NSA
    rows = NSA_HEADS * tq
    kv_blk = lambda i: pl.BlockSpec((1, s, LANES), lambda b, q, i=i: (b, 0, i))
    cmp_blk = pl.BlockSpec((1, nc, LANES), lambda b, q: (b, 0, 0))
    const = lambda a: pl.BlockSpec(a.shape, lambda b, q: (0,) * a.ndim)
    return pl.pallas_call(
        _nsa_kernel,
        grid=(bsz, s // tq),
        in_specs=[pl.BlockSpec((1, tq, nq), lambda b, q: (b, q, 0)),
                  cmp_blk, cmp_blk, kv_blk(2), kv_blk(3), kv_blk(4), kv_blk(5),
                  const(kx), const(kxc), const(qx),
                  pl.BlockSpec((1, tq, NSA_GROUPS * LANES), lambda b, q: (b, q, 0)),
                  pl.BlockSpec((1, tq, GATE_PAD), lambda b, q: (b, q, 0))],
        out_specs=pl.BlockSpec((1, tq, nq), lambda b, q: (b, q, 0)),
        out_shape=jax.ShapeDtypeStruct((bsz, s, nq), BF16),
        scratch_shapes=[pltpu.VMEM((rows, 2 * LANES), BF16),
                        pltpu.VMEM((rows, 2 * LANES), BF16),
                        pltpu.VMEM((tq, nq), F32)],
        compiler_params=_cparams("parallel", "arbitrary"),
        name="nsa",
    )(qa, kc, vc, kva, kva, kva, kva, kx, kxc, qx, selbias, ga)


def _sb_kernel(q_ref, k_ref, v_ref, tri_ref, o_ref, carry_ref, acc_ref):
    tq, tk = TQ_SB, TK_SB
    qi = pl.program_id(2)
    lane = lax.broadcasted_iota(jnp.int32, (tq, LANES), 1)
    lo = lane < HEAD_DIM
    ntri = tri_ref[0:tk, :]
    pairs = range(SB_PAIRS)
    cols = [slice(p * LANES, (p + 1) * LANES) for p in pairs]
    rows = [slice(p * 2 * tq, (p + 1) * 2 * tq) for p in pairs]
    qs = []
    for p in pairs:
        q2 = q_ref[0, :, cols[p]].astype(F32)
        qs.append(jnp.concatenate([jnp.where(lo, q2, 0.0), jnp.where(lo, 0.0, q2)], axis=0).astype(BF16))

    def stick(z):
        sp = jnp.maximum(z, 0.0) + jnp.log2(1.0 + jnp.exp2(-jnp.abs(z)))
        return sp, z - sp

    kpos = lax.broadcasted_iota(jnp.int32, (tq, tk), 1)
    tpos = lax.broadcasted_iota(jnp.int32, (tq, tk), 0)
    keep = jnp.concatenate([kpos < tpos] * 2, axis=0)

    def first_block(nt):
        w = nt * tk
        d0 = w - tk
        k_start = pl.multiple_of((qi - (nt - 1)) * tk, tk)
        for p in pairs:
            sp, ls = stick(_dot_nt(qs[p], k_ref[0, pl.ds(k_start, w), cols[p]]))
            sp_d = jnp.where(keep, sp[:, d0:w], 0.0)
            sp_b = (jnp.concatenate([sp[:, 0:d0], sp_d], axis=1) if nt > 1 else sp_d).astype(BF16)
            a_parts = []
            for i in range(nt):
                later = _dot(sp_b[:, i * tk:w], tri_ref[0:w - i * tk, :])
                a = jnp.exp2(ls[:, i * tk:(i + 1) * tk] + later)
                a_parts.append(jnp.where(keep, a, 0.0) if i == nt - 1 else a)
                if i == 0:
                    carry_ref[rows[p], :] = later[:, 0:1] - sp_b[:, 0:1].astype(F32)
            a = (jnp.concatenate(a_parts, axis=1) if nt > 1 else a_parts[0]).astype(BF16)
            acc_ref[rows[p], :] = _dot(a, v_ref[0, pl.ds(k_start, w), cols[p]])

    for nt in range(1, SB_FIRST_TILES):
        pl.when(qi == nt - 1)(functools.partial(first_block, nt))

    @pl.when(qi >= SB_FIRST_TILES - 1)
    def _():
        first_block(SB_FIRST_TILES)

        for p in pairs:
            def tile(kt, p=p):
                k0 = pl.multiple_of(kt * tk, tk)
                sp, ls = stick(_dot_nt(qs[p], k_ref[0, pl.ds(k0, tk), cols[p]]))
                later = _dot(sp.astype(BF16), ntri)
                pv = _dot(jnp.exp2(ls + later).astype(BF16), v_ref[0, pl.ds(k0, tk), cols[p]])
                carry = carry_ref[rows[p], :]
                acc_ref[rows[p], :] += jnp.exp2(carry) * pv
                carry_ref[rows[p], :] = carry - jnp.sum(sp, axis=-1, keepdims=True)

            def live(p=p):
                return jnp.max(carry_ref[rows[p], :]) > SB_DEAD_LOG2

            def cond(st):
                return (st[0] >= 0) & st[1]

            def body(st, tile=tile, live=live):
                tile(st[0])
                return st[0] - 1, live()

            lax.while_loop(cond, body, (qi - SB_FIRST_TILES, live()))

    for p in pairs:
        r0 = p * 2 * tq
        o_ref[0, :, cols[p]] = jnp.where(lo, acc_ref[r0:r0 + tq], acc_ref[r0 + tq:r0 + 2 * tq]).astype(o_ref.dtype)


def _sb(qkvb, tri):
    bsz, s, n3 = qkvb.shape
    nblk = SB_HEADS // 2 // SB_PAIRS
    wide = SB_PAIRS * LANES
    tq = TQ_SB
    return pl.pallas_call(
        _sb_kernel,
        grid=(bsz, nblk, s // tq),
        in_specs=[pl.BlockSpec((1, tq, wide), lambda b, p, q: (b, q, p)),
                  pl.BlockSpec((1, s, wide), lambda b, p, q: (b, 0, nblk + p)),
                  pl.BlockSpec((1, s, wide), lambda b, p, q: (b, 0, 2 * nblk + p)),
                  pl.BlockSpec(tri.shape, lambda b, p, q: (0, 0))],
        out_specs=pl.BlockSpec((1, tq, wide), lambda b, p, q: (b, q, p)),
        out_shape=jax.ShapeDtypeStruct((bsz, s, nblk * wide), BF16),
        scratch_shapes=[pltpu.VMEM((SB_PAIRS * 2 * tq, 1), F32),
                        pltpu.VMEM((SB_PAIRS * 2 * tq, LANES), F32)],
        compiler_params=_cparams("parallel", "parallel", "arbitrary"),
        name="sb",
    )(qkvb, qkvb, qkvb, tri)


def _post_kernel(x_ref, oa_ref, ob_ref, mg_ref, mod_ref, wpa_ref, wpb_ref, wo_ref, g2_ref,
                 wg_ref, wu_ref, wd_ref, gf_ref, out_ref):
    d = x_ref.shape[2]
    part_rows = x_ref.shape[1] // POST_CHAINS
    for i in range(POST_CHAINS):
        rs = slice(i * part_rows, (i + 1) * part_rows)
        x = x_ref[0, rs, :]
        ya = _dot(oa_ref[0, rs, :], wpa_ref[...])
        yb = _dot(ob_ref[0, rs, :], wpb_ref[...])
        y = mg_ref[0, rs, 0:d].astype(F32) * ya + mg_ref[0, rs, d:2 * d].astype(F32) * yb
        x1 = x + mod_ref[0, 2:3, :] * _dot(y.astype(BF16), wo_ref[...])
        h2 = (_rmsnorm(x1, g2_ref[...]) * (1.0 + mod_ref[0, 4:5, :]) + mod_ref[0, 3:4, :]).astype(BF16)
        ffn = None
        for c0 in range(0, wg_ref.shape[1], FF_CHUNK):
            gte = _dot(h2, wg_ref[:, c0:c0 + FF_CHUNK])
            up = _dot(h2, wu_ref[:, c0:c0 + FF_CHUNK])
            act = (gte * jax.nn.sigmoid(gte) * up).astype(BF16)
            part = _dot(act, wd_ref[c0:c0 + FF_CHUNK, :])
            ffn = part if ffn is None else ffn + part
        x2 = x1 + mod_ref[0, 5:6, :] * ffn
        out_ref[0, rs, :] = _rmsnorm(x2, gf_ref[...])


def _post(x, oa, ob, mg, mod3, wpa, wpb, wo, g2, wg3, wu3, wd3, gf):
    bsz, s, d = x.shape
    tm = TM_POST
    tok = lambda n: pl.BlockSpec((1, tm, n), lambda b, i: (b, i, 0))
    return pl.pallas_call(
        _post_kernel,
        grid=(bsz, s // tm),
        in_specs=[tok(d), tok(oa.shape[2]), tok(ob.shape[2]), tok(mg.shape[2]),
                  pl.BlockSpec((1, 6, d), lambda b, i: (b, 0, 0)),
                  _resident(wpa.shape), _resident(wpb.shape), _resident(wo.shape),
                  pl.BlockSpec((1, d), lambda b, i: (0, 0)),
                  _resident(wg3.shape), _resident(wu3.shape), _resident(wd3.shape),
                  pl.BlockSpec((1, d), lambda b, i: (0, 0))],
        out_specs=tok(d),
        out_shape=jax.ShapeDtypeStruct((bsz, s, d), F32),
        compiler_params=_cparams("parallel", "parallel"),
        name="post",
    )(x, oa, ob, mg, mod3, wpa, wpb, wo, g2.reshape(1, d), wg3, wu3, wd3, gf.reshape(1, d))


def _bf16_pieces(x, n=3):
    out, rem = [], float(x)
    for _ in range(n):
        p = float(np.asarray(rem, np.float32).astype(jnp.bfloat16).astype(np.float32))
        out.append(p)
        rem -= p
    return out


N_PIECES = 3


def _key_extras(pos):
    lane = jnp.arange(LANES)[None, :]
    a = (pos // SEL_BLOCK)[:, None]
    b = (pos % SEL_BLOCK)[:, None]
    ext = jnp.where(lane < N_PIECES, a, jnp.where(lane < 2 * N_PIECES, b, 0)).astype(F32)
    return ext, a


def _query_extras():
    qx = np.zeros((NSA_HEADS, LANES), np.float32)
    for h in range(NSA_HEADS):
        slope = 2.0 ** (-(h + 1))
        for i, p in enumerate(_bf16_pieces(LOG2E, N_PIECES)):
            qx[h, i] = SEL_BLOCK * slope * p
            qx[h, N_PIECES + i] = slope * p
    return jnp.asarray(qx)


def kernel(x, c, w_ada, b_ada, norm_mix_g, w_in, cmp_pos_k, cmp_w1_k, cmp_w2_k, cmp_pos_v, cmp_w1_v, cmp_w2_v,
           w_proj_a, w_proj_b, w_out, norm_ffn_g, w_ffn_gate, w_ffn_up, w_ffn_down, norm_final_g):
    bsz, s, d = x.shape
    depth = w_ada.shape[0]
    assert s % SEL_CHUNK == 0 and s // SEL_BLOCK <= 32 and s % TM_IN == 0 and s >= WINDOW + TQ_NSA
    n_q = NSA_HEADS * HEAD_DIM
    n_kv = 6 * NSA_GROUPS * HEAD_DIM
    n_gate = 3 * NSA_HEADS
    n_b = 3 * SB_HEADS * HEAD_DIM
    n_m = 2 * d
    nc = s // CMP_STRIDE
    d_ff = w_ffn_gate.shape[2]
    assert d_ff % FF_CHUNK == 0

    pos = jnp.arange(s)
    ext, blk = _key_extras(pos)
    lane = jnp.arange(LANES)[None, :]
    kx = jnp.where(lane == SEL_LANE0 + blk, 1.0, ext).astype(BF16)
    kxc = _key_extras(jnp.arange(nc) * CMP_STRIDE + CMP_BLOCK - 1)[0].astype(BF16)
    jrow = jnp.arange(LANES)[:, None] - SEL_LANE0
    ncol = jnp.arange(nc)[None, :]
    ovl = ((ncol * CMP_STRIDE <= jrow * SEL_BLOCK + SEL_BLOCK - 1)
           & (ncol * CMP_STRIDE + CMP_BLOCK - 1 >= jrow * SEL_BLOCK)
           & (jrow >= 0) & (jrow < s // SEL_BLOCK) & (ncol < nc - 1)).astype(BF16)
    qx = _query_extras()
    cend = jnp.arange(nc)[:, None] * CMP_STRIDE + CMP_BLOCK - 1
    cmask = jnp.where((pos[None, :] >= cend) & (jnp.arange(nc)[:, None] < nc - 1), 0.0, NEG_INF).astype(F32)
    jj = jnp.arange(TK_SB)
    tri = -jnp.concatenate([jj[:, None] > jj[None, :]] + [jnp.ones((TK_SB, TK_SB), jnp.bool_)] * (SB_FIRST_TILES - 1),
                           axis=0).astype(BF16)

    for l in range(depth):
        w = w_in[l]
        wq = (w[:, :n_q] * (ATTN_SCALE * LOG2E)).reshape(d, NSA_GROUPS, NSA_REP, HEAD_DIM).transpose(0, 2, 1, 3).reshape(d, n_q)
        o1 = n_q + n_kv
        o2 = o1 + n_gate
        o3 = o2 + n_b
        wgate = jnp.pad(w[:, o1:o2], ((0, 0), (0, GATE_PAD - n_gate)))
        wb = jnp.concatenate([w[:, o2:o2 + n_b // 3] * (ATTN_SCALE * LOG2E), w[:, o2 + n_b // 3:o3]], axis=1)
        w_all = jnp.concatenate([wq, w[:, n_q:o1], wgate, wb, w[:, o3:]], axis=1).astype(BF16)
        splits = (n_q, n_kv, GATE_PAD, n_b, n_m)

        def cmp_weights(pos_emb, w1, w2):
            half = CMP_BLOCK // 2
            w1r = w1.reshape(2, half, HEAD_DIM, CMP_HIDDEN)
            z = jnp.zeros_like(w1r)
            g0 = jnp.concatenate([w1r, z], axis=3)
            g1 = jnp.concatenate([z, w1r], axis=3)
            w1p = jnp.stack([g0, g1], axis=2).reshape(2 * half * 2 * HEAD_DIM, 2 * CMP_HIDDEN)
            zz = jnp.zeros_like(w2)
            w2p = jnp.concatenate([jnp.concatenate([w2, zz], axis=1), jnp.concatenate([zz, w2], axis=1)], axis=0)
            pe = pos_emb.reshape(2, half, 1, HEAD_DIM)
            pe = jnp.broadcast_to(pe, (2, half, 2, HEAD_DIM)).reshape(2, half * 2 * HEAD_DIM)
            return pe, w1p.astype(BF16), w2p.astype(BF16)

        pk, w1k, w2k = cmp_weights(cmp_pos_k[l], cmp_w1_k[l], cmp_w2_k[l])
        pv, w1v, w2v = cmp_weights(cmp_pos_v[l], cmp_w1_v[l], cmp_w2_v[l])
        wpa = w_proj_a[l].reshape(NSA_GROUPS, NSA_REP, HEAD_DIM, d).transpose(1, 0, 2, 3).reshape(n_q, d).astype(BF16)
        wpb = w_proj_b[l].astype(BF16)
        wo = w_out[l].astype(BF16)
        wg3 = w_ffn_gate[l].astype(BF16)
        wu3 = w_ffn_up[l].astype(BF16)
        wd3 = w_ffn_down[l].astype(BF16)

        mod3 = _ada(c, w_ada[l], b_ada[l]).reshape(bsz, 6, d)
        qa, kva, qkvb, mg, ga = _in_proj(x, mod3, norm_mix_g[l], w_all, splits)
        kc, vc = _compress(kva, pk, pv, w1k, w1v, w2k, w2v)
        selbias = _nsa_select(qa, kc, kxc, ovl, qx, cmask)
        oa = _nsa(qa, kva, kc, vc, ga, kx, kxc, qx, selbias)
        ob = _sb(qkvb, tri)
        gf = norm_final_g if l == depth - 1 else jnp.ones_like(norm_final_g)
        x = _post(x, oa, ob, mg, mod3, wpa, wpb, wo, norm_ffn_g[l], wg3, wu3, wd3, gf)
        assert depth == 1, "final norm is fused into the last layer's post kernel"
    return x
```

```python
import functools
import math

import jax
import jax.numpy as jnp
import numpy as np
from jax import lax
from jax.experimental import pallas as pl
from jax.experimental.pallas import tpu as pltpu

F32 = jnp.float32
BF16 = jnp.bfloat16

HEAD_DIM = 64
NSA_HEADS = 8
NSA_GROUPS = 2
NSA_REP = NSA_HEADS // NSA_GROUPS
SB_HEADS = 8
CMP_BLOCK = 32
CMP_STRIDE = 16
CMP_HIDDEN = 2 * HEAD_DIM
SEL_BLOCK = 64
SEL_TOPK = 8
WINDOW = 512
RMS_EPS = 1e-6
NEG_INF = -1e30
FORCE_PRIORITY = 1e4
ATTN_SCALE = 1.0 / math.sqrt(HEAD_DIM)
LOG2E = math.log2(math.e)

LANES = 128
SEL_LANE0 = 32
GATE_PAD = 128

TM_IN = 1024
IN_CHUNK = 512
TM_POST = 512
POST_CHAINS = 1
TQ_NSA = 256
TQ_SELECT = 1024
NSA_CHAINS = 4
SEL_CHUNK = 512
TQ_SB = 256
TK_SB = 256
FF_CHUNK = 256
SB_DEAD_LOG2 = -150.0
SB_FIRST_TILES = 3
SB_PAIRS = 4

VMEM_LIMIT = 56 * 1024 * 1024


def _cparams(*sem):
    return pltpu.CompilerParams(dimension_semantics=sem, vmem_limit_bytes=VMEM_LIMIT)


def _resident(shape):
    nd = len(shape)
    return pl.BlockSpec(shape, lambda *_: (0,) * nd, pipeline_mode=pl.Buffered(1))


def _dot(a, b):
    return jnp.dot(a, b, preferred_element_type=F32)


def _dot_nt(a, b):
    return lax.dot_general(a, b, (((1,), (1,)), ((), ())), preferred_element_type=F32)


def _split_bf16(v):
    hi = v.astype(BF16)
    lo = (v - hi.astype(F32)).astype(BF16)
    return hi, lo


def _rmsnorm(x, g):
    return x * lax.rsqrt(jnp.mean(x * x, axis=-1, keepdims=True) + RMS_EPS) * g


def _ada_kernel(c_ref, w_ref, b_ref, o_ref):
    c = c_ref[...]
    a = (c * jax.nn.sigmoid(c)).astype(BF16)
    o_ref[...] = _dot(a, w_ref[...].astype(BF16)) + b_ref[...]


def _ada(c, w, b):
    bsz, d = c.shape
    n = w.shape[1]
    tn = d
    return pl.pallas_call(
        _ada_kernel,
        grid=(n // tn,),
        in_specs=[pl.BlockSpec((bsz, d), lambda j: (0, 0)),
                  pl.BlockSpec((d, tn), lambda j: (0, j)),
                  pl.BlockSpec((1, tn), lambda j: (0, j))],
        out_specs=pl.BlockSpec((bsz, tn), lambda j: (0, j)),
        out_shape=jax.ShapeDtypeStruct((bsz, n), F32),
        compiler_params=_cparams("arbitrary"),
        name="ada",
    )(c, w, b.reshape(1, n))


def _in_proj_kernel(x_ref, mod_ref, g_ref, w_ref, qa_ref, kva_ref, qkvb_ref, mg_ref, ga_ref, *, splits):
    n_q, n_kv, n_g, n_b, n_m = splits
    h = _rmsnorm(x_ref[0], g_ref[...]) * (1.0 + mod_ref[0, 1:2, :]) + mod_ref[0, 0:1, :]
    hb = h.astype(BF16)
    col = 0
    for ref, width, act in ((qa_ref, n_q, None), (kva_ref, n_kv, None), (ga_ref, n_g, "sig"),
                            (qkvb_ref, n_b, None), (mg_ref, n_m, "sig")):
        for c0 in range(0, width, IN_CHUNK):
            cw = min(IN_CHUNK, width - c0)
            r = _dot(hb, w_ref[:, col + c0:col + c0 + cw])
            if act == "sig":
                r = jax.nn.sigmoid(r)
            ref[0, :, c0:c0 + cw] = r.astype(ref.dtype)
        col += width


def _in_proj(x, mod3, g, w_all, splits):
    bsz, s, d = x.shape
    n_q, n_kv, n_g, n_b, n_m = splits
    tm = TM_IN
    out_shape = (jax.ShapeDtypeStruct((bsz, s, n_q), BF16),
                 jax.ShapeDtypeStruct((bsz, s, n_kv), BF16),
                 jax.ShapeDtypeStruct((bsz, s, n_b), BF16),
                 jax.ShapeDtypeStruct((bsz, s, n_m), BF16),
                 jax.ShapeDtypeStruct((bsz, s, n_g), F32))
    tok = lambda n: pl.BlockSpec((1, tm, n), lambda b, i: (b, i, 0))
    return pl.pallas_call(
        functools.partial(_in_proj_kernel, splits=splits),
        grid=(bsz, s // tm),
        in_specs=[tok(d),
                  pl.BlockSpec((1, 6, d), lambda b, i: (b, 0, 0)),
                  pl.BlockSpec((1, d), lambda b, i: (0, 0)),
                  _resident(w_all.shape)],
        out_specs=(tok(n_q), tok(n_kv), tok(n_b), tok(n_m), tok(n_g)),
        out_shape=out_shape,
        compiler_params=_cparams("parallel", "parallel"),
        name="in_proj",
    )(x, mod3, g.reshape(1, d), w_all)


def _gelu_tanh(x):
    return 0.5 * x * (1.0 + jnp.tanh(math.sqrt(2.0 / math.pi) * (x + 0.044715 * (x * x * x))))


def _compress_kernel(k_ref, v_ref, pk_ref, pv_ref, w1k_ref, w1v_ref, w2k_ref, w2v_ref, kc_ref, vc_ref, x32_ref):
    s = k_ref.shape[1]
    nc = s // CMP_STRIDE
    half = w1k_ref.shape[0] // 2

    def one(x_ref, p_ref, w1_ref, w2_ref, o_ref):
        x32_ref[...] = x_ref[0].astype(F32)
        a = b = None
        for l in range(CMP_STRIDE):
            xl = x32_ref[pl.ds(l, nc, stride=CMP_STRIDE), :]
            rows = slice(l * LANES, (l + 1) * LANES)
            top = (xl + p_ref[0:1, rows]).astype(BF16)
            bot = (xl + p_ref[1:2, rows]).astype(BF16)
            da = _dot(top, w1_ref[rows, :])
            db = _dot(bot, w1_ref[half + l * LANES:half + (l + 1) * LANES, :])
            a = da if a is None else a + da
            b = db if b is None else b + db
        hid = a + pltpu.roll(b, nc - 1, 0)
        o_ref[0] = _dot(_gelu_tanh(hid).astype(BF16), w2_ref[...]).astype(o_ref.dtype)

    one(k_ref, pk_ref, w1k_ref, w2k_ref, kc_ref)
    one(v_ref, pv_ref, w1v_ref, w2v_ref, vc_ref)


def _compress(kva, pk, pv, w1k, w1v, w2k, w2v):
    bsz, s, _ = kva.shape
    nc = s // CMP_STRIDE
    blk = lambda i: pl.BlockSpec((1, s, LANES), lambda b, i=i: (b, 0, i))
    out = pl.BlockSpec((1, nc, LANES), lambda b: (b, 0, 0))
    full = lambda a: pl.BlockSpec(a.shape, lambda b: (0,) * a.ndim)
    return pl.pallas_call(
        _compress_kernel,
        grid=(bsz,),
        in_specs=[blk(0), blk(1), full(pk), full(pv), full(w1k), full(w1v), full(w2k), full(w2v)],
        out_specs=(out, out),
        out_shape=(jax.ShapeDtypeStruct((bsz, nc, LANES), BF16),) * 2,
        scratch_shapes=[pltpu.VMEM((s, LANES), F32)],
        compiler_params=_cparams("parallel"),
        name="compress",
    )(kva, kva, pk, pv, w1k, w1v, w2k, w2v)


def _build_qaug(qa_ref, qx_ref, qaug_ref, tq):
    lo = lax.broadcasted_iota(jnp.int32, (tq, LANES), 1) < HEAD_DIM
    for g in range(NSA_GROUPS):
        for r in range(NSA_REP):
            h = g * NSA_REP + r
            qr = qa_ref[0, :, r * LANES:(r + 1) * LANES].astype(F32)
            qm = jnp.where(lo if g == 0 else jnp.logical_not(lo), qr, 0.0)
            qaug_ref[h * tq:(h + 1) * tq, 0:LANES] = qm.astype(BF16)
            qaug_ref[h * tq:(h + 1) * tq, LANES:2 * LANES] = jnp.broadcast_to(qx_ref[h:h + 1, :], (tq, LANES)).astype(BF16)


def _nsa_select_kernel(qa_ref, kc_ref, kxc_ref, ovl_ref, qx_ref, cmask_ref, bias_ref, qaug_ref):
    tq = TQ_SELECT
    q0 = pl.program_id(1) * tq
    _build_qaug(qa_ref, qx_ref, qaug_ref, tq)
    kc_aug = jnp.concatenate([kc_ref[0], kxc_ref[...]], axis=1)
    st = _dot_nt(kc_aug, qaug_ref[...])
    cmask = cmask_ref[...]
    sm = st + jnp.concatenate([cmask] * NSA_HEADS, axis=1)
    e = jnp.exp2(sm - jnp.max(sm, axis=0, keepdims=True))
    has_any = jnp.concatenate([cmask[0:1, :] == 0.0] * NSA_HEADS, axis=1)
    p_t = e * jnp.where(has_any, 1.0 / jnp.sum(e, axis=0, keepdims=True), 0.0)
    nsel_rows = 32
    j_s = lax.broadcasted_iota(jnp.int32, (nsel_rows, tq), 0)
    cur = (q0 + lax.broadcasted_iota(jnp.int32, (nsel_rows, tq), 1)) // SEL_BLOCK
    valid_j = j_s <= cur
    forced = ((j_s == 0) | (j_s == cur) | (j_s == cur - 1)) & valid_j
    for g in range(NSA_GROUPS):
        base = g * NSA_REP * tq
        psum = (p_t[:, base:base + tq] + p_t[:, base + tq:base + 2 * tq]
                + p_t[:, base + 2 * tq:base + 3 * tq] + p_t[:, base + 3 * tq:base + 4 * tq])
        p_hi, p_lo = _split_bf16(psum)
        p_slc = _dot(ovl_ref[...], p_hi) + _dot(ovl_ref[...], p_lo)
        prio = p_slc[SEL_LANE0:SEL_LANE0 + nsel_rows]
        prio = jnp.where(valid_j, prio, NEG_INF)
        prio = jnp.where(forced, FORCE_PRIORITY, prio)
        chosen = jnp.zeros((nsel_rows, tq), jnp.bool_)
        for _ in range(SEL_TOPK):
            best = jnp.max(prio, axis=0, keepdims=True)
            first = jnp.min(jnp.where(prio == best, j_s, nsel_rows), axis=0, keepdims=True)
            pick = j_s == first
            chosen = chosen | pick
            prio = jnp.where(pick, -3e38, prio)
        bias_t = jnp.where(chosen, 0.0, NEG_INF)
        bias_t = jnp.concatenate([jnp.zeros((SEL_LANE0, tq), F32), bias_t,
                                  jnp.zeros((LANES - SEL_LANE0 - nsel_rows, tq), F32)], axis=0)
        for c0 in range(0, tq, LANES):
            bias_ref[0, c0:c0 + LANES, g * LANES:(g + 1) * LANES] = bias_t[:, c0:c0 + LANES].T.astype(BF16)


def _nsa_select(qa, kc, kxc, ovl, qx, cmask):
    bsz, s, nq = qa.shape
    nc = kc.shape[1]
    tq = TQ_SELECT
    const = lambda a: pl.BlockSpec(a.shape, lambda b, q: (0,) * a.ndim)
    return pl.pallas_call(
        _nsa_select_kernel,
        grid=(bsz, s // tq),
        in_specs=[pl.BlockSpec((1, tq, nq), lambda b, q: (b, q, 0)),
                  pl.BlockSpec((1, nc, LANES), lambda b, q: (b, 0, 0)),
                  const(kxc), const(ovl), const(qx),
                  pl.BlockSpec((nc, tq), lambda b, q: (0, q))],
        out_specs=pl.BlockSpec((1, tq, NSA_GROUPS * LANES), lambda b, q: (b, q, 0)),
        out_shape=jax.ShapeDtypeStruct((bsz, s, NSA_GROUPS * LANES), BF16),
        scratch_shapes=[pltpu.VMEM((NSA_HEADS * tq, 2 * LANES), BF16)],
        compiler_params=_cparams("parallel", "parallel"),
        name="nsa_select",
    )(qa, kc, kxc, ovl, qx, cmask)


def _nsa_kernel(qa_ref, kc_ref, vc_ref, ksel_ref, vsel_ref, kwin_ref, vwin_ref, kx_ref, kxc_ref, qx_ref, bias_ref,
                ga_ref, o_ref, qaug_ref, qsel_ref, oacc_ref):
    tq = TQ_NSA
    nh = NSA_HEADS
    rows = nh * tq
    qi = pl.program_id(1)
    q0 = pl.multiple_of(qi * tq, tq)
    nc = kc_ref.shape[1]
    n_cmp = nc - 1

    lane = lax.broadcasted_iota(jnp.int32, (tq, LANES), 1)
    row = lax.broadcasted_iota(jnp.int32, (tq, LANES), 0)
    lo = lane < HEAD_DIM

    _build_qaug(qa_ref, qx_ref, qaug_ref, tq)
    for h in range(nh):
        g = h // NSA_REP
        qsel_ref[h * tq:(h + 1) * tq, 0:LANES] = qaug_ref[h * tq:(h + 1) * tq, 0:LANES]
        qsel_ref[h * tq:(h + 1) * tq, LANES:2 * LANES] = (
            bias_ref[0, :, g * LANES:(g + 1) * LANES].astype(F32) + qx_ref[h:h + 1, :]).astype(BF16)

    ga = ga_ref[0]

    def emit(o_heads, branch, first):
        for r in range(NSA_REP):
            c0 = r * 3 + branch
            c1 = NSA_REP * 3 + r * 3 + branch
            gt = jnp.where(lo, ga[:, c0:c0 + 1], ga[:, c1:c1 + 1])
            val = gt * jnp.where(lo, o_heads[r * tq:(r + 1) * tq], o_heads[(NSA_REP + r) * tq:(NSA_REP + r + 1) * tq])
            if first:
                oacc_ref[:, r * LANES:(r + 1) * LANES] = val
            else:
                oacc_ref[:, r * LANES:(r + 1) * LANES] += val

    def attend(q, k_aug, v, mask_fn):
        w = k_aug.shape[0]
        v_aug = jnp.concatenate([v, jnp.ones((w, LANES), BF16)], axis=1)
        part = rows // NSA_CHAINS
        outs = []
        for c in range(NSA_CHAINS):
            s3 = mask_fn(_dot_nt(q[c * part:(c + 1) * part], k_aug).reshape(part // tq, tq, w))
            p = jnp.exp2(s3 - jnp.max(s3, axis=-1, keepdims=True)).astype(BF16).reshape(part, w)
            pv = _dot(p, v_aug)
            outs.append(pv[:, 0:LANES] / pv[:, LANES:2 * LANES])
        return jnp.concatenate(outs, axis=0)

    qaug = qaug_ref[...]

    qsel = qsel_ref[...]
    last = q0 // SEL_CHUNK

    def branches(n_full):
        kc_aug = jnp.concatenate([kc_ref[0], kxc_ref[...]], axis=1)
        n_l = lax.broadcasted_iota(jnp.int32, (tq, nc), 1)
        t_l = q0 + lax.broadcasted_iota(jnp.int32, (tq, nc), 0)
        valid = (t_l >= n_l * CMP_STRIDE + (CMP_BLOCK - 1)) & (n_l < n_cmp)
        o_cmp = attend(qaug, kc_aug, vc_ref[0], lambda s3: jnp.where(valid[None], s3, NEG_INF))
        has_cmp = jnp.concatenate([q0 + row >= CMP_BLOCK - 1] * nh, axis=0)
        emit(jnp.where(has_cmp, o_cmp, 0.0), 0, True)

        span = WINDOW + tq
        w_start = pl.multiple_of(jnp.maximum(q0 - WINDOW, 0), tq)
        k_aug = jnp.concatenate([kwin_ref[0, pl.ds(w_start, span), :], kx_ref[pl.ds(w_start, span), :]], axis=1)
        dist = (q0 + lax.broadcasted_iota(jnp.int32, (tq, span), 0)) - (w_start + lax.broadcasted_iota(jnp.int32, (tq, span), 1))
        band = (dist >= 0) & (dist < WINDOW)
        emit(attend(qaug, k_aug, vwin_ref[0, pl.ds(w_start, span), :], lambda s3: jnp.where(band[None], s3, NEG_INF)), 2, False)

        nk = (n_full + 1) * SEL_CHUNK
        k_aug = jnp.concatenate([ksel_ref[0, 0:nk, :], kx_ref[0:nk, :]], axis=1)
        kpos = n_full * SEL_CHUNK + lax.broadcasted_iota(jnp.int32, (tq, SEL_CHUNK), 1)
        tpos = q0 + lax.broadcasted_iota(jnp.int32, (tq, SEL_CHUNK), 0)
        causal = (kpos <= tpos)[None]

        def mask_fn(s3):
            diag = jnp.where(causal, s3[:, :, n_full * SEL_CHUNK:nk], NEG_INF)
            return jnp.concatenate([s3[:, :, 0:n_full * SEL_CHUNK], diag], axis=2) if n_full else diag

        emit(attend(qsel, k_aug, vsel_ref[0, 0:nk, :], mask_fn), 1, False)

    for n_full in range(ksel_ref.shape[1] // SEL_CHUNK):
        pl.when(last == n_full)(functools.partial(branches, n_full))

    o_ref[0] = oacc_ref[...].astype(o_ref.dtype)


def _nsa(qa, kva, kc, vc, ga, kx, kxc, qx, selbias):
    bsz, s, nq = qa.shape
    nc = kc.shape[1]
    tq = TQ_NSA
    rows = NSA_HEADS * tq
    kv_blk = lambda i: pl.BlockSpec((1, s, LANES), lambda b, q, i=i: (b, 0, i))
    cmp_blk = pl.BlockSpec((1, nc, LANES), lambda b, q: (b, 0, 0))
    const = lambda a: pl.BlockSpec(a.shape, lambda b, q: (0,) * a.ndim)
    return pl.pallas_call(
        _nsa_kernel,
        grid=(bsz, s // tq),
        in_specs=[pl.BlockSpec((1, tq, nq), lambda b, q: (b, q, 0)),
                  cmp_blk, cmp_blk, kv_blk(2), kv_blk(3), kv_blk(4), kv_blk(5),
                  const(kx), const(kxc), const(qx),
                  pl.BlockSpec((1, tq, NSA_GROUPS * LANES), lambda b, q: (b, q, 0)),
                  pl.BlockSpec((1, tq, GATE_PAD), lambda b, q: (b, q, 0))],
        out_specs=pl.BlockSpec((1, tq, nq), lambda b, q: (b, q, 0)),
        out_shape=jax.ShapeDtypeStruct((bsz, s, nq), BF16),
        scratch_shapes=[pltpu.VMEM((rows, 2 * LANES), BF16),
                        pltpu.VMEM((rows, 2 * LANES), BF16),
                        pltpu.VMEM((tq, nq), F32)],
        compiler_params=_cparams("parallel", "arbitrary"),
        name="nsa",
    )(qa, kc, vc, kva, kva, kva, kva, kx, kxc, qx, selbias, ga)


def _sb_kernel(q_ref, k_ref, v_ref, tri_ref, o_ref, carry_ref, acc_ref):
    tq, tk = TQ_SB, TK_SB
    qi = pl.program_id(2)
    lane = lax.broadcasted_iota(jnp.int32, (tq, LANES), 1)
    lo = lane < HEAD_DIM
    ntri = tri_ref[0:tk, :]
    pairs = range(SB_PAIRS)
    cols = [slice(p * LANES, (p + 1) * LANES) for p in pairs]
    rows = [slice(p * 2 * tq, (p + 1) * 2 * tq) for p in pairs]
    qs = []
    for p in pairs:
        q2 = q_ref[0, :, cols[p]].astype(F32)
        qs.append(jnp.concatenate([jnp.where(lo, q2, 0.0), jnp.where(lo, 0.0, q2)], axis=0).astype(BF16))

    def stick(z):
        sp = jnp.maximum(z, 0.0) + jnp.log2(1.0 + jnp.exp2(-jnp.abs(z)))
        return sp, z - sp

    kpos = lax.broadcasted_iota(jnp.int32, (tq, tk), 1)
    tpos = lax.broadcasted_iota(jnp.int32, (tq, tk), 0)
    keep = jnp.concatenate([kpos < tpos] * 2, axis=0)

    def first_block(nt):
        w = nt * tk
        d0 = w - tk
        k_start = pl.multiple_of((qi - (nt - 1)) * tk, tk)
        for p in pairs:
            sp, ls = stick(_dot_nt(qs[p], k_ref[0, pl.ds(k_start, w), cols[p]]))
            sp_d = jnp.where(keep, sp[:, d0:w], 0.0)
            sp_b = (jnp.concatenate([sp[:, 0:d0], sp_d], axis=1) if nt > 1 else sp_d).astype(BF16)
            a_parts = []
            for i in range(nt):
                later = _dot(sp_b[:, i * tk:w], tri_ref[0:w - i * tk, :])
                a = jnp.exp2(ls[:, i * tk:(i + 1) * tk] + later)
                a_parts.append(jnp.where(keep, a, 0.0) if i == nt - 1 else a)
                if i == 0:
                    carry_ref[rows[p], :] = later[:, 0:1] - sp_b[:, 0:1].astype(F32)
            a = (jnp.concatenate(a_parts, axis=1) if nt > 1 else a_parts[0]).astype(BF16)
            acc_ref[rows[p], :] = _dot(a, v_ref[0, pl.ds(k_start, w), cols[p]])

    for nt in range(1, SB_FIRST_TILES):
        pl.when(qi == nt - 1)(functools.partial(first_block, nt))

    @pl.when(qi >= SB_FIRST_TILES - 1)
    def _():
        first_block(SB_FIRST_TILES)

        for p in pairs:
            def tile(kt, p=p):
                k0 = pl.multiple_of(kt * tk, tk)
                sp, ls = stick(_dot_nt(qs[p], k_ref[0, pl.ds(k0, tk), cols[p]]))
                later = _dot(sp.astype(BF16), ntri)
                pv = _dot(jnp.exp2(ls + later).astype(BF16), v_ref[0, pl.ds(k0, tk), cols[p]])
                carry = carry_ref[rows[p], :]
                acc_ref[rows[p], :] += jnp.exp2(carry) * pv
                carry_ref[rows[p], :] = carry - jnp.sum(sp, axis=-1, keepdims=True)

            def live(p=p):
                return jnp.max(carry_ref[rows[p], :]) > SB_DEAD_LOG2

            def cond(st):
                return (st[0] >= 0) & st[1]

            def body(st, tile=tile, live=live):
                tile(st[0])
                return st[0] - 1, live()

            lax.while_loop(cond, body, (qi - SB_FIRST_TILES, live()))

    for p in pairs:
        r0 = p * 2 * tq
        o_ref[0, :, cols[p]] = jnp.where(lo, acc_ref[r0:r0 + tq], acc_ref[r0 + tq:r0 + 2 * tq]).astype(o_ref.dtype)


def _sb(qkvb, tri):
    bsz, s, n3 = qkvb.shape
    nblk = SB_HEADS // 2 // SB_PAIRS
    wide = SB_PAIRS * LANES
    tq = TQ_SB
    return pl.pallas_call(
        _sb_kernel,
        grid=(bsz, nblk, s // tq),
        in_specs=[pl.BlockSpec((1, tq, wide), lambda b, p, q: (b, q, p)),
                  pl.BlockSpec((1, s, wide), lambda b, p, q: (b, 0, nblk + p)),
                  pl.BlockSpec((1, s, wide), lambda b, p, q: (b, 0, 2 * nblk + p)),
                  pl.BlockSpec(tri.shape, lambda b, p, q: (0, 0))],
        out_specs=pl.BlockSpec((1, tq, wide), lambda b, p, q: (b, q, p)),
        out_shape=jax.ShapeDtypeStruct((bsz, s, nblk * wide), BF16),
        scratch_shapes=[pltpu.VMEM((SB_PAIRS * 2 * tq, 1), F32),
                        pltpu.VMEM((SB_PAIRS * 2 * tq, LANES), F32)],
        compiler_params=_cparams("parallel", "parallel", "arbitrary"),
        name="sb",
    )(qkvb, qkvb, qkvb, tri)


def _post_kernel(x_ref, oa_ref, ob_ref, mg_ref, mod_ref, wpa_ref, wpb_ref, wo_ref, g2_ref,
                 wg_ref, wu_ref, wd_ref, gf_ref, out_ref):
    d = x_ref.shape[2]
    part_rows = x_ref.shape[1] // POST_CHAINS
    for i in range(POST_CHAINS):
        rs = slice(i * part_rows, (i + 1) * part_rows)
        x = x_ref[0, rs, :]
        ya = _dot(oa_ref[0, rs, :], wpa_ref[...])
        yb = _dot(ob_ref[0, rs, :], wpb_ref[...])
        y = mg_ref[0, rs, 0:d].astype(F32) * ya + mg_ref[0, rs, d:2 * d].astype(F32) * yb
        x1 = x + mod_ref[0, 2:3, :] * _dot(y.astype(BF16), wo_ref[...])
        h2 = (_rmsnorm(x1, g2_ref[...]) * (1.0 + mod_ref[0, 4:5, :]) + mod_ref[0, 3:4, :]).astype(BF16)
        acts = []
        for c0 in range(0, wg_ref.shape[1], FF_CHUNK):
            gte = _dot(h2, wg_ref[:, c0:c0 + FF_CHUNK])
            up = _dot(h2, wu_ref[:, c0:c0 + FF_CHUNK])
            acts.append((gte * jax.nn.sigmoid(gte) * up).astype(BF16))
        ffn = _dot(jnp.concatenate(acts, axis=1), wd_ref[...])
        x2 = x1 + mod_ref[0, 5:6, :] * ffn
        out_ref[0, rs, :] = _rmsnorm(x2, gf_ref[...])


def _post(x, oa, ob, mg, mod3, wpa, wpb, wo, g2, wg3, wu3, wd3, gf):
    bsz, s, d = x.shape
    tm = TM_POST
    tok = lambda n: pl.BlockSpec((1, tm, n), lambda b, i: (b, i, 0))
    return pl.pallas_call(
        _post_kernel,
        grid=(bsz, s // tm),
        in_specs=[tok(d), tok(oa.shape[2]), tok(ob.shape[2]), tok(mg.shape[2]),
                  pl.BlockSpec((1, 6, d), lambda b, i: (b, 0, 0)),
                  _resident(wpa.shape), _resident(wpb.shape), _resident(wo.shape),
                  pl.BlockSpec((1, d), lambda b, i: (0, 0)),
                  _resident(wg3.shape), _resident(wu3.shape), _resident(wd3.shape),
                  pl.BlockSpec((1, d), lambda b, i: (0, 0))],
        out_specs=tok(d),
        out_shape=jax.ShapeDtypeStruct((bsz, s, d), F32),
        compiler_params=_cparams("parallel", "parallel"),
        name="post",
    )(x, oa, ob, mg, mod3, wpa, wpb, wo, g2.reshape(1, d), wg3, wu3, wd3, gf.reshape(1, d))


def _bf16_pieces(x, n=3):
    out, rem = [], float(x)
    for _ in range(n):
        p = float(np.asarray(rem, np.float32).astype(jnp.bfloat16).astype(np.float32))
        out.append(p)
        rem -= p
    return out


N_PIECES = 3


def _key_extras(pos):
    lane = jnp.arange(LANES)[None, :]
    a = (pos // SEL_BLOCK)[:, None]
    b = (pos % SEL_BLOCK)[:, None]
    ext = jnp.where(lane < N_PIECES, a, jnp.where(lane < 2 * N_PIECES, b, 0)).astype(F32)
    return ext, a


def _query_extras():
    qx = np.zeros((NSA_HEADS, LANES), np.float32)
    for h in range(NSA_HEADS):
        slope = 2.0 ** (-(h + 1))
        for i, p in enumerate(_bf16_pieces(LOG2E, N_PIECES)):
            qx[h, i] = SEL_BLOCK * slope * p
            qx[h, N_PIECES + i] = slope * p
    return jnp.asarray(qx)


def kernel(x, c, w_ada, b_ada, norm_mix_g, w_in, cmp_pos_k, cmp_w1_k, cmp_w2_k, cmp_pos_v, cmp_w1_v, cmp_w2_v,
           w_proj_a, w_proj_b, w_out, norm_ffn_g, w_ffn_gate, w_ffn_up, w_ffn_down, norm_final_g):
    bsz, s, d = x.shape
    depth = w_ada.shape[0]
    assert s % SEL_CHUNK == 0 and s // SEL_BLOCK <= 32 and s % TM_IN == 0 and s >= WINDOW + TQ_NSA
    n_q = NSA_HEADS * HEAD_DIM
    n_kv = 6 * NSA_GROUPS * HEAD_DIM
    n_gate = 3 * NSA_HEADS
    n_b = 3 * SB_HEADS * HEAD_DIM
    n_m = 2 * d
    nc = s // CMP_STRIDE
    d_ff = w_ffn_gate.shape[2]
    assert d_ff % FF_CHUNK == 0

    pos = jnp.arange(s)
    ext, blk = _key_extras(pos)
    lane = jnp.arange(LANES)[None, :]
    kx = jnp.where(lane == SEL_LANE0 + blk, 1.0, ext).astype(BF16)
    kxc = _key_extras(jnp.arange(nc) * CMP_STRIDE + CMP_BLOCK - 1)[0].astype(BF16)
    jrow = jnp.arange(LANES)[:, None] - SEL_LANE0
    ncol = jnp.arange(nc)[None, :]
    ovl = ((ncol * CMP_STRIDE <= jrow * SEL_BLOCK + SEL_BLOCK - 1)
           & (ncol * CMP_STRIDE + CMP_BLOCK - 1 >= jrow * SEL_BLOCK)
           & (jrow >= 0) & (jrow < s // SEL_BLOCK) & (ncol < nc - 1)).astype(BF16)
    qx = _query_extras()
    cend = jnp.arange(nc)[:, None] * CMP_STRIDE + CMP_BLOCK - 1
    cmask = jnp.where((pos[None, :] >= cend) & (jnp.arange(nc)[:, None] < nc - 1), 0.0, NEG_INF).astype(F32)
    jj = jnp.arange(TK_SB)
    tri = -jnp.concatenate([jj[:, None] > jj[None, :]] + [jnp.ones((TK_SB, TK_SB), jnp.bool_)] * (SB_FIRST_TILES - 1),
                           axis=0).astype(BF16)

    for l in range(depth):
        w = w_in[l]
        wq = (w[:, :n_q] * (ATTN_SCALE * LOG2E)).reshape(d, NSA_GROUPS, NSA_REP, HEAD_DIM).transpose(0, 2, 1, 3).reshape(d, n_q)
        o1 = n_q + n_kv
        o2 = o1 + n_gate
        o3 = o2 + n_b
        wgate = jnp.pad(w[:, o1:o2], ((0, 0), (0, GATE_PAD - n_gate)))
        wb = jnp.concatenate([w[:, o2:o2 + n_b // 3] * (ATTN_SCALE * LOG2E), w[:, o2 + n_b // 3:o3]], axis=1)
        w_all = jnp.concatenate([wq, w[:, n_q:o1], wgate, wb, w[:, o3:]], axis=1).astype(BF16)
        splits = (n_q, n_kv, GATE_PAD, n_b, n_m)

        def cmp_weights(pos_emb, w1, w2):
            half = CMP_BLOCK // 2
            w1r = w1.reshape(2, half, HEAD_DIM, CMP_HIDDEN)
            z = jnp.zeros_like(w1r)
            g0 = jnp.concatenate([w1r, z], axis=3)
            g1 = jnp.concatenate([z, w1r], axis=3)
            w1p = jnp.stack([g0, g1], axis=2).reshape(2 * half * 2 * HEAD_DIM, 2 * CMP_HIDDEN)
            zz = jnp.zeros_like(w2)
            w2p = jnp.concatenate([jnp.concatenate([w2, zz], axis=1), jnp.concatenate([zz, w2], axis=1)], axis=0)
            pe = pos_emb.reshape(2, half, 1, HEAD_DIM)
            pe = jnp.broadcast_to(pe, (2, half, 2, HEAD_DIM)).reshape(2, half * 2 * HEAD_DIM)
            return pe, w1p.astype(BF16), w2p.astype(BF16)

        pk, w1k, w2k = cmp_weights(cmp_pos_k[l], cmp_w1_k[l], cmp_w2_k[l])
        pv, w1v, w2v = cmp_weights(cmp_pos_v[l], cmp_w1_v[l], cmp_w2_v[l])
        wpa = w_proj_a[l].reshape(NSA_GROUPS, NSA_REP, HEAD_DIM, d).transpose(1, 0, 2, 3).reshape(n_q, d).astype(BF16)
        wpb = w_proj_b[l].astype(BF16)
        wo = w_out[l].astype(BF16)
        wg3 = w_ffn_gate[l].astype(BF16)
        wu3 = w_ffn_up[l].astype(BF16)
        wd3 = w_ffn_down[l].astype(BF16)

        mod3 = _ada(c, w_ada[l], b_ada[l]).reshape(bsz, 6, d)
        qa, kva, qkvb, mg, ga = _in_proj(x, mod3, norm_mix_g[l], w_all, splits)
        kc, vc = _compress(kva, pk, pv, w1k, w1v, w2k, w2v)
        selbias = _nsa_select(qa, kc, kxc, ovl, qx, cmask)
        oa = _nsa(qa, kva, kc, vc, ga, kx, kxc, qx, selbias)
        ob = _sb(qkvb, tri)
        gf = norm_final_g if l == depth - 1 else jnp.ones_like(norm_final_g)
        x = _post(x, oa, ob, mg, mod3, wpa, wpb, wo, norm_ffn_g[l], wg3, wu3, wd3, gf)
        assert depth == 1, "final norm is fused into the last layer's post kernel"
    return x
```

```python
import functools
import math

import jax
import jax.numpy as jnp
import numpy as np
from jax import lax
from jax.experimental import pallas as pl
from jax.experimental.pallas import tpu as pltpu

F32 = jnp.float32
BF16 = jnp.bfloat16

HEAD_DIM = 64
NSA_HEADS = 8
NSA_GROUPS = 2
NSA_REP = NSA_HEADS // NSA_GROUPS
SB_HEADS = 8
CMP_BLOCK = 32
CMP_STRIDE = 16
CMP_HIDDEN = 2 * HEAD_DIM
SEL_BLOCK = 64
SEL_TOPK = 8
WINDOW = 512
RMS_EPS = 1e-6
NEG_INF = -1e30
FORCE_PRIORITY = 1e4
ATTN_SCALE = 1.0 / math.sqrt(HEAD_DIM)
LOG2E = math.log2(math.e)

LANES = 128
SEL_LANE0 = 32
GATE_PAD = 128

TM_IN = 1024
IN_CHUNK = 512
TM_POST = 512
POST_CHAINS = 1
TQ_NSA = 256
TQ_SELECT = 1024
NSA_CHAINS = 4
SEL_CHUNK = 512
TQ_SB = 256
TK_SB = 256
FF_CHUNK = 256
SB_DEAD_LOG2 = -150.0
SB_FIRST_TILES = 3
SB_PAIRS = 4

VMEM_LIMIT = 56 * 1024 * 1024


def _cparams(*sem):
    return pltpu.CompilerParams(dimension_semantics=sem, vmem_limit_bytes=VMEM_LIMIT)


def _resident(shape):
    nd = len(shape)
    return pl.BlockSpec(shape, lambda *_: (0,) * nd, pipeline_mode=pl.Buffered(1))


def _dot(a, b):
    return jnp.dot(a, b, preferred_element_type=F32)


def _dot_nt(a, b):
    return lax.dot_general(a, b, (((1,), (1,)), ((), ())), preferred_element_type=F32)


def _split_bf16(v):
    hi = v.astype(BF16)
    lo = (v - hi.astype(F32)).astype(BF16)
    return hi, lo


def _rmsnorm(x, g):
    return x * lax.rsqrt(jnp.mean(x * x, axis=-1, keepdims=True) + RMS_EPS) * g


def _ada_kernel(c_ref, w_ref, b_ref, o_ref):
    c = c_ref[...]
    a = (c * jax.nn.sigmoid(c)).astype(BF16)
    o_ref[...] = _dot(a, w_ref[...].astype(BF16)) + b_ref[...]


def _ada(c, w, b):
    bsz, d = c.shape
    n = w.shape[1]
    tn = d
    return pl.pallas_call(
        _ada_kernel,
        grid=(n // tn,),
        in_specs=[pl.BlockSpec((bsz, d), lambda j: (0, 0)),
                  pl.BlockSpec((d, tn), lambda j: (0, j)),
                  pl.BlockSpec((1, tn), lambda j: (0, j))],
        out_specs=pl.BlockSpec((bsz, tn), lambda j: (0, j)),
        out_shape=jax.ShapeDtypeStruct((bsz, n), F32),
        compiler_params=_cparams("arbitrary"),
        name="ada",
    )(c, w, b.reshape(1, n))


def _w_in_kernel(w_ref, o_ref, *, n_q, n_kv, n_gate, n_b):
    w = w_ref[0]
    rows = w.shape[0]
    qscale = ATTN_SCALE * LOG2E
    for r in range(NSA_REP):
        for g in range(NSA_GROUPS):
            src = (g * NSA_REP + r) * HEAD_DIM
            dst = (r * NSA_GROUPS + g) * HEAD_DIM
            o_ref[:, dst:dst + HEAD_DIM] = (w[:, src:src + HEAD_DIM] * qscale).astype(BF16)
    o1 = n_q + n_kv
    o2 = o1 + n_gate
    o_ref[:, n_q:o1] = w[:, n_q:o1].astype(BF16)
    gate = jnp.concatenate([w[:, o1:o2], jnp.zeros((rows, GATE_PAD - n_gate), F32)], axis=1)
    o_ref[:, o1:o1 + GATE_PAD] = gate.astype(BF16)
    p = o1 + GATE_PAD
    nbq = n_b // 3
    o_ref[:, p:p + nbq] = (w[:, o2:o2 + nbq] * qscale).astype(BF16)
    rest = w.shape[1] - (o2 + nbq)
    o_ref[:, p + nbq:p + nbq + rest] = w[:, o2 + nbq:].astype(BF16)


def _w_in_layout(w_in_l, n_q, n_kv, n_gate, n_b):
    d, d_in = w_in_l.shape[1], w_in_l.shape[2]
    tr = 128
    n_out = d_in - n_gate + GATE_PAD
    return pl.pallas_call(
        functools.partial(_w_in_kernel, n_q=n_q, n_kv=n_kv, n_gate=n_gate, n_b=n_b),
        grid=(d // tr,),
        in_specs=[pl.BlockSpec((1, tr, d_in), lambda i: (0, i, 0))],
        out_specs=pl.BlockSpec((tr, n_out), lambda i: (i, 0)),
        out_shape=jax.ShapeDtypeStruct((d, n_out), BF16),
        compiler_params=_cparams("parallel"),
        name="w_in_layout",
    )(w_in_l)


def _in_proj_kernel(x_ref, mod_ref, g_ref, w_ref, qa_ref, kva_ref, qkvb_ref, mg_ref, ga_ref, *, splits):
    n_q, n_kv, n_g, n_b, n_m = splits
    h = _rmsnorm(x_ref[0], g_ref[...]) * (1.0 + mod_ref[0, 1:2, :]) + mod_ref[0, 0:1, :]
    hb = h.astype(BF16)
    col = 0
    for ref, width, act in ((qa_ref, n_q, None), (kva_ref, n_kv, None), (ga_ref, n_g, "sig"),
                            (qkvb_ref, n_b, None), (mg_ref, n_m, "sig")):
        for c0 in range(0, width, IN_CHUNK):
            cw = min(IN_CHUNK, width - c0)
            r = _dot(hb, w_ref[:, col + c0:col + c0 + cw])
            if act == "sig":
                r = jax.nn.sigmoid(r)
            ref[0, :, c0:c0 + cw] = r.astype(ref.dtype)
        col += width


def _in_proj(x, mod3, g, w_all, splits):
    bsz, s, d = x.shape
    n_q, n_kv, n_g, n_b, n_m = splits
    tm = TM_IN
    out_shape = (jax.ShapeDtypeStruct((bsz, s, n_q), BF16),
                 jax.ShapeDtypeStruct((bsz, s, n_kv), BF16),
                 jax.ShapeDtypeStruct((bsz, s, n_b), BF16),
                 jax.ShapeDtypeStruct((bsz, s, n_m), BF16),
                 jax.ShapeDtypeStruct((bsz, s, n_g), F32))
    tok = lambda n: pl.BlockSpec((1, tm, n), lambda b, i: (b, i, 0))
    return pl.pallas_call(
        functools.partial(_in_proj_kernel, splits=splits),
        grid=(bsz, s // tm),
        in_specs=[tok(d),
                  pl.BlockSpec((1, 6, d), lambda b, i: (b, 0, 0)),
                  pl.BlockSpec((1, d), lambda b, i: (0, 0)),
                  _resident(w_all.shape)],
        out_specs=(tok(n_q), tok(n_kv), tok(n_b), tok(n_m), tok(n_g)),
        out_shape=out_shape,
        compiler_params=_cparams("parallel", "parallel"),
        name="in_proj",
    )(x, mod3, g.reshape(1, d), w_all)


def _gelu_tanh(x):
    return 0.5 * x * (1.0 + jnp.tanh(math.sqrt(2.0 / math.pi) * (x + 0.044715 * (x * x * x))))


def _compress_kernel(k_ref, v_ref, pk_ref, pv_ref, w1k_ref, w1v_ref, w2k_ref, w2v_ref, kc_ref, vc_ref, x32_ref):
    s = k_ref.shape[1]
    nc = s // CMP_STRIDE
    half = w1k_ref.shape[0] // 2

    def one(x_ref, p_ref, w1_ref, w2_ref, o_ref):
        x32_ref[...] = x_ref[0].astype(F32)
        a = b = None
        for l in range(CMP_STRIDE):
            xl = x32_ref[pl.ds(l, nc, stride=CMP_STRIDE), :]
            rows = slice(l * LANES, (l + 1) * LANES)
            top = (xl + p_ref[0:1, rows]).astype(BF16)
            bot = (xl + p_ref[1:2, rows]).astype(BF16)
            da = _dot(top, w1_ref[rows, :])
            db = _dot(bot, w1_ref[half + l * LANES:half + (l + 1) * LANES, :])
            a = da if a is None else a + da
            b = db if b is None else b + db
        hid = a + pltpu.roll(b, nc - 1, 0)
        o_ref[0] = _dot(_gelu_tanh(hid).astype(BF16), w2_ref[...]).astype(o_ref.dtype)

    one(k_ref, pk_ref, w1k_ref, w2k_ref, kc_ref)
    one(v_ref, pv_ref, w1v_ref, w2v_ref, vc_ref)


def _compress(kva, pk, pv, w1k, w1v, w2k, w2v):
    bsz, s, _ = kva.shape
    nc = s // CMP_STRIDE
    blk = lambda i: pl.BlockSpec((1, s, LANES), lambda b, i=i: (b, 0, i))
    out = pl.BlockSpec((1, nc, LANES), lambda b: (b, 0, 0))
    full = lambda a: pl.BlockSpec(a.shape, lambda b: (0,) * a.ndim)
    return pl.pallas_call(
        _compress_kernel,
        grid=(bsz,),
        in_specs=[blk(0), blk(1), full(pk), full(pv), full(w1k), full(w1v), full(w2k), full(w2v)],
        out_specs=(out, out),
        out_shape=(jax.ShapeDtypeStruct((bsz, nc, LANES), BF16),) * 2,
        scratch_shapes=[pltpu.VMEM((s, LANES), F32)],
        compiler_params=_cparams("parallel"),
        name="compress",
    )(kva, kva, pk, pv, w1k, w1v, w2k, w2v)


def _build_qaug(qa_ref, qx_ref, qaug_ref, tq):
    lo = lax.broadcasted_iota(jnp.int32, (tq, LANES), 1) < HEAD_DIM
    for g in range(NSA_GROUPS):
        for r in range(NSA_REP):
            h = g * NSA_REP + r
            qr = qa_ref[0, :, r * LANES:(r + 1) * LANES].astype(F32)
            qm = jnp.where(lo if g == 0 else jnp.logical_not(lo), qr, 0.0)
            qaug_ref[h * tq:(h + 1) * tq, 0:LANES] = qm.astype(BF16)
            qaug_ref[h * tq:(h + 1) * tq, LANES:2 * LANES] = jnp.broadcast_to(qx_ref[h:h + 1, :], (tq, LANES)).astype(BF16)


def _nsa_select_kernel(qa_ref, kc_ref, kxc_ref, ovl_ref, qx_ref, cmask_ref, bias_ref, qaug_ref):
    tq = TQ_SELECT
    q0 = pl.program_id(1) * tq
    _build_qaug(qa_ref, qx_ref, qaug_ref, tq)
    kc_aug = jnp.concatenate([kc_ref[0], kxc_ref[...]], axis=1)
    st = _dot_nt(kc_aug, qaug_ref[...])
    cmask = cmask_ref[...]
    sm = st + jnp.concatenate([cmask] * NSA_HEADS, axis=1)
    e = jnp.exp2(sm - jnp.max(sm, axis=0, keepdims=True))
    has_any = jnp.concatenate([cmask[0:1, :] == 0.0] * NSA_HEADS, axis=1)
    p_t = e * jnp.where(has_any, 1.0 / jnp.sum(e, axis=0, keepdims=True), 0.0)
    nsel_rows = 32
    j_s = lax.broadcasted_iota(jnp.int32, (nsel_rows, tq), 0)
    cur = (q0 + lax.broadcasted_iota(jnp.int32, (nsel_rows, tq), 1)) // SEL_BLOCK
    valid_j = j_s <= cur
    forced = ((j_s == 0) | (j_s == cur) | (j_s == cur - 1)) & valid_j
    for g in range(NSA_GROUPS):
        base = g * NSA_REP * tq
        psum = (p_t[:, base:base + tq] + p_t[:, base + tq:base + 2 * tq]
                + p_t[:, base + 2 * tq:base + 3 * tq] + p_t[:, base + 3 * tq:base + 4 * tq])
        p_hi, p_lo = _split_bf16(psum)
        p_slc = _dot(ovl_ref[...], p_hi) + _dot(ovl_ref[...], p_lo)
        prio = p_slc[SEL_LANE0:SEL_LANE0 + nsel_rows]
        prio = jnp.where(valid_j, prio, NEG_INF)
        prio = jnp.where(forced, FORCE_PRIORITY, prio)
        chosen = jnp.zeros((nsel_rows, tq), jnp.bool_)
        for _ in range(SEL_TOPK):
            best = jnp.max(prio, axis=0, keepdims=True)
            first = jnp.min(jnp.where(prio == best, j_s, nsel_rows), axis=0, keepdims=True)
            pick = j_s == first
            chosen = chosen | pick
            prio = jnp.where(pick, -3e38, prio)
        bias_t = jnp.where(chosen, 0.0, NEG_INF)
        bias_t = jnp.concatenate([jnp.zeros((SEL_LANE0, tq), F32), bias_t,
                                  jnp.zeros((LANES - SEL_LANE0 - nsel_rows, tq), F32)], axis=0)
        for c0 in range(0, tq, LANES):
            bias_ref[0, c0:c0 + LANES, g * LANES:(g + 1) * LANES] = bias_t[:, c0:c0 + LANES].T.astype(BF16)


def _nsa_select(qa, kc, kxc, ovl, qx, cmask):
    bsz, s, nq = qa.shape
    nc = kc.shape[1]
    tq = TQ_SELECT
    const = lambda a: pl.BlockSpec(a.shape, lambda b, q: (0,) * a.ndim)
    return pl.pallas_call(
        _nsa_select_kernel,
        grid=(bsz, s // tq),
        in_specs=[pl.BlockSpec((1, tq, nq), lambda b, q: (b, q, 0)),
                  pl.BlockSpec((1, nc, LANES), lambda b, q: (b, 0, 0)),
                  const(kxc), const(ovl), const(qx),
                  pl.BlockSpec((nc, tq), lambda b, q: (0, q))],
        out_specs=pl.BlockSpec((1, tq, NSA_GROUPS * LANES), lambda b, q: (b, q, 0)),
        out_shape=jax.ShapeDtypeStruct((bsz, s, NSA_GROUPS * LANES), BF16),
        scratch_shapes=[pltpu.VMEM((NSA_HEADS * tq, 2 * LANES), BF16)],
        compiler_params=_cparams("parallel", "parallel"),
        name="nsa_select",
    )(qa, kc, kxc, ovl, qx, cmask)


def _nsa_kernel(qa_ref, kc_ref, vc_ref, ksel_ref, vsel_ref, kwin_ref, vwin_ref, kx_ref, kxc_ref, qx_ref, bias_ref,
                ga_ref, o_ref, qaug_ref, qsel_ref, oacc_ref):
    tq = TQ_NSA
    nh = NSA_HEADS
    rows = nh * tq
    qi = pl.program_id(1)
    q0 = pl.multiple_of(qi * tq, tq)
    nc = kc_ref.shape[1]
    n_cmp = nc - 1

    lane = lax.broadcasted_iota(jnp.int32, (tq, LANES), 1)
    row = lax.broadcasted_iota(jnp.int32, (tq, LANES), 0)
    lo = lane < HEAD_DIM

    _build_qaug(qa_ref, qx_ref, qaug_ref, tq)
    for h in range(nh):
        g = h // NSA_REP
        qsel_ref[h * tq:(h + 1) * tq, 0:LANES] = qaug_ref[h * tq:(h + 1) * tq, 0:LANES]
        qsel_ref[h * tq:(h + 1) * tq, LANES:2 * LANES] = (
            bias_ref[0, :, g * LANES:(g + 1) * LANES].astype(F32) + qx_ref[h:h + 1, :]).astype(BF16)

    ga = ga_ref[0]

    def emit(o_heads, branch, first):
        for r in range(NSA_REP):
            c0 = r * 3 + branch
            c1 = NSA_REP * 3 + r * 3 + branch
            gt = jnp.where(lo, ga[:, c0:c0 + 1], ga[:, c1:c1 + 1])
            val = gt * jnp.where(lo, o_heads[r * tq:(r + 1) * tq], o_heads[(NSA_REP + r) * tq:(NSA_REP + r + 1) * tq])
            if first:
                oacc_ref[:, r * LANES:(r + 1) * LANES] = val
            else:
                oacc_ref[:, r * LANES:(r + 1) * LANES] += val

    def attend(q, k_aug, v, mask_fn):
        w = k_aug.shape[0]
        v_aug = jnp.concatenate([v, jnp.ones((w, LANES), BF16)], axis=1)
        part = rows // NSA_CHAINS
        outs = []
        for c in range(NSA_CHAINS):
            s3 = mask_fn(_dot_nt(q[c * part:(c + 1) * part], k_aug).reshape(part // tq, tq, w))
            p = jnp.exp2(s3 - jnp.max(s3, axis=-1, keepdims=True)).astype(BF16).reshape(part, w)
            pv = _dot(p, v_aug)
            outs.append(pv[:, 0:LANES] / pv[:, LANES:2 * LANES])
        return jnp.concatenate(outs, axis=0)

    qaug = qaug_ref[...]

    qsel = qsel_ref[...]
    last = q0 // SEL_CHUNK

    def branches(n_full):
        kc_aug = jnp.concatenate([kc_ref[0], kxc_ref[...]], axis=1)
        n_l = lax.broadcasted_iota(jnp.int32, (tq, nc), 1)
        t_l = q0 + lax.broadcasted_iota(jnp.int32, (tq, nc), 0)
        valid = (t_l >= n_l * CMP_STRIDE + (CMP_BLOCK - 1)) & (n_l < n_cmp)
        o_cmp = attend(qaug, kc_aug, vc_ref[0], lambda s3: jnp.where(valid[None], s3, NEG_INF))
        has_cmp = jnp.concatenate([q0 + row >= CMP_BLOCK - 1] * nh, axis=0)
        emit(jnp.where(has_cmp, o_cmp, 0.0), 0, True)

        span = WINDOW + tq
        w_start = pl.multiple_of(jnp.maximum(q0 - WINDOW, 0), tq)
        k_aug = jnp.concatenate([kwin_ref[0, pl.ds(w_start, span), :], kx_ref[pl.ds(w_start, span), :]], axis=1)
        dist = (q0 + lax.broadcasted_iota(jnp.int32, (tq, span), 0)) - (w_start + lax.broadcasted_iota(jnp.int32, (tq, span), 1))
        band = (dist >= 0) & (dist < WINDOW)
        emit(attend(qaug, k_aug, vwin_ref[0, pl.ds(w_start, span), :], lambda s3: jnp.where(band[None], s3, NEG_INF)), 2, False)

        nk = (n_full + 1) * SEL_CHUNK
        k_aug = jnp.concatenate([ksel_ref[0, 0:nk, :], kx_ref[0:nk, :]], axis=1)
        kpos = n_full * SEL_CHUNK + lax.broadcasted_iota(jnp.int32, (tq, SEL_CHUNK), 1)
        tpos = q0 + lax.broadcasted_iota(jnp.int32, (tq, SEL_CHUNK), 0)
        causal = (kpos <= tpos)[None]

        def mask_fn(s3):
            diag = jnp.where(causal, s3[:, :, n_full * SEL_CHUNK:nk], NEG_INF)
            return jnp.concatenate([s3[:, :, 0:n_full * SEL_CHUNK], diag], axis=2) if n_full else diag

        emit(attend(qsel, k_aug, vsel_ref[0, 0:nk, :], mask_fn), 1, False)

    for n_full in range(ksel_ref.shape[1] // SEL_CHUNK):
        pl.when(last == n_full)(functools.partial(branches, n_full))

    o_ref[0] = oacc_ref[...].astype(o_ref.dtype)


def _nsa(qa, kva, kc, vc, ga, kx, kxc, qx, selbias):
    bsz, s, nq = qa.shape
    nc = kc.shape[1]
    tq = TQ_NSA
    rows = NSA_HEADS * tq
    kv_blk = lambda i: pl.BlockSpec((1, s, LANES), lambda b, q, i=i: (b, 0, i))
    cmp_blk = pl.BlockSpec((1, nc, LANES), lambda b, q: (b, 0, 0))
    const = lambda a: pl.BlockSpec(a.shape, lambda b, q: (0,) * a.ndim)
    return pl.pallas_call(
        _nsa_kernel,
        grid=(bsz, s // tq),
        in_specs=[pl.BlockSpec((1, tq, nq), lambda b, q: (b, q, 0)),
                  cmp_blk, cmp_blk, kv_blk(2), kv_blk(3), kv_blk(4), kv_blk(5),
                  const(kx), const(kxc), const(qx),
                  pl.BlockSpec((1, tq, NSA_GROUPS * LANES), lambda b, q: (b, q, 0)),
                  pl.BlockSpec((1, tq, GATE_PAD), lambda b, q: (b, q, 0))],
        out_specs=pl.BlockSpec((1, tq, nq), lambda b, q: (b, q, 0)),
        out_shape=jax.ShapeDtypeStruct((bsz, s, nq), BF16),
        scratch_shapes=[pltpu.VMEM((rows, 2 * LANES), BF16),
                        pltpu.VMEM((rows, 2 * LANES), BF16),
                        pltpu.VMEM((tq, nq), F32)],
        compiler_params=_cparams("parallel", "arbitrary"),
        name="nsa",
    )(qa, kc, vc, kva, kva, kva, kva, kx, kxc, qx, selbias, ga)


def _sb_kernel(q_ref, k_ref, v_ref, tri_ref, o_ref, carry_ref, acc_ref):
    tq, tk = TQ_SB, TK_SB
    qi = pl.program_id(2)
    lane = lax.broadcasted_iota(jnp.int32, (tq, LANES), 1)
    lo = lane < HEAD_DIM
    ntri = tri_ref[0:tk, :]
    pairs = range(SB_PAIRS)
    cols = [slice(p * LANES, (p + 1) * LANES) for p in pairs]
    rows = [slice(p * 2 * tq, (p + 1) * 2 * tq) for p in pairs]
    qs = []
    for p in pairs:
        q2 = q_ref[0, :, cols[p]].astype(F32)
        qs.append(jnp.concatenate([jnp.where(lo, q2, 0.0), jnp.where(lo, 0.0, q2)], axis=0).astype(BF16))

    def stick(z):
        sp = jnp.maximum(z, 0.0) + jnp.log2(1.0 + jnp.exp2(-jnp.abs(z)))
        return sp, z - sp

    kpos = lax.broadcasted_iota(jnp.int32, (tq, tk), 1)
    tpos = lax.broadcasted_iota(jnp.int32, (tq, tk), 0)
    keep = jnp.concatenate([kpos < tpos] * 2, axis=0)

    def first_block(nt):
        w = nt * tk
        d0 = w - tk
        k_start = pl.multiple_of((qi - (nt - 1)) * tk, tk)
        for p in pairs:
            sp, ls = stick(_dot_nt(qs[p], k_ref[0, pl.ds(k_start, w), cols[p]]))
            sp_d = jnp.where(keep, sp[:, d0:w], 0.0)
            sp_b = (jnp.concatenate([sp[:, 0:d0], sp_d], axis=1) if nt > 1 else sp_d).astype(BF16)
            a_parts = []
            for i in range(nt):
                later = _dot(sp_b[:, i * tk:w], tri_ref[0:w - i * tk, :])
                a = jnp.exp2(ls[:, i * tk:(i + 1) * tk] + later)
                a_parts.append(jnp.where(keep, a, 0.0) if i == nt - 1 else a)
                if i == 0:
                    carry_ref[rows[p], :] = later[:, 0:1] - sp_b[:, 0:1].astype(F32)
            a = (jnp.concatenate(a_parts, axis=1) if nt > 1 else a_parts[0]).astype(BF16)
            acc_ref[rows[p], :] = _dot(a, v_ref[0, pl.ds(k_start, w), cols[p]])

    for nt in range(1, SB_FIRST_TILES):
        pl.when(qi == nt - 1)(functools.partial(first_block, nt))

    @pl.when(qi >= SB_FIRST_TILES - 1)
    def _():
        first_block(SB_FIRST_TILES)

        for p in pairs:
            def tile(kt, p=p):
                k0 = pl.multiple_of(kt * tk, tk)
                sp, ls = stick(_dot_nt(qs[p], k_ref[0, pl.ds(k0, tk), cols[p]]))
                later = _dot(sp.astype(BF16), ntri)
                pv = _dot(jnp.exp2(ls + later).astype(BF16), v_ref[0, pl.ds(k0, tk), cols[p]])
                carry = carry_ref[rows[p], :]
                acc_ref[rows[p], :] += jnp.exp2(carry) * pv
                carry_ref[rows[p], :] = carry - jnp.sum(sp, axis=-1, keepdims=True)

            def live(p=p):
                return jnp.max(carry_ref[rows[p], :]) > SB_DEAD_LOG2

            def cond(st):
                return (st[0] >= 0) & st[1]

            def body(st, tile=tile, live=live):
                tile(st[0])
                return st[0] - 1, live()

            lax.while_loop(cond, body, (qi - SB_FIRST_TILES, live()))

    for p in pairs:
        r0 = p * 2 * tq
        o_ref[0, :, cols[p]] = jnp.where(lo, acc_ref[r0:r0 + tq], acc_ref[r0 + tq:r0 + 2 * tq]).astype(o_ref.dtype)


def _sb(qkvb, tri):
    bsz, s, n3 = qkvb.shape
    nblk = SB_HEADS // 2 // SB_PAIRS
    wide = SB_PAIRS * LANES
    tq = TQ_SB
    return pl.pallas_call(
        _sb_kernel,
        grid=(bsz, nblk, s // tq),
        in_specs=[pl.BlockSpec((1, tq, wide), lambda b, p, q: (b, q, p)),
                  pl.BlockSpec((1, s, wide), lambda b, p, q: (b, 0, nblk + p)),
                  pl.BlockSpec((1, s, wide), lambda b, p, q: (b, 0, 2 * nblk + p)),
                  pl.BlockSpec(tri.shape, lambda b, p, q: (0, 0))],
        out_specs=pl.BlockSpec((1, tq, wide), lambda b, p, q: (b, q, p)),
        out_shape=jax.ShapeDtypeStruct((bsz, s, nblk * wide), BF16),
        scratch_shapes=[pltpu.VMEM((SB_PAIRS * 2 * tq, 1), F32),
                        pltpu.VMEM((SB_PAIRS * 2 * tq, LANES), F32)],
        compiler_params=_cparams("parallel", "parallel", "arbitrary"),
        name="sb",
    )(qkvb, qkvb, qkvb, tri)


def _post_kernel(x_ref, oa_ref, ob_ref, mg_ref, mod_ref, wpa_ref, wpb_ref, wo_ref, g2_ref,
                 wg_ref, wu_ref, wd_ref, gf_ref, out_ref):
    d = x_ref.shape[2]
    part_rows = x_ref.shape[1] // POST_CHAINS
    for i in range(POST_CHAINS):
        rs = slice(i * part_rows, (i + 1) * part_rows)
        x = x_ref[0, rs, :]
        ya = _dot(oa_ref[0, rs, :], wpa_ref[...])
        yb = _dot(ob_ref[0, rs, :], wpb_ref[...])
        y = mg_ref[0, rs, 0:d].astype(F32) * ya + mg_ref[0, rs, d:2 * d].astype(F32) * yb
        x1 = x + mod_ref[0, 2:3, :] * _dot(y.astype(BF16), wo_ref[...])
        h2 = (_rmsnorm(x1, g2_ref[...]) * (1.0 + mod_ref[0, 4:5, :]) + mod_ref[0, 3:4, :]).astype(BF16)
        acts = []
        for c0 in range(0, wg_ref.shape[1], FF_CHUNK):
            gte = _dot(h2, wg_ref[:, c0:c0 + FF_CHUNK])
            up = _dot(h2, wu_ref[:, c0:c0 + FF_CHUNK])
            acts.append((gte * jax.nn.sigmoid(gte) * up).astype(BF16))
        ffn = _dot(jnp.concatenate(acts, axis=1), wd_ref[...])
        x2 = x1 + mod_ref[0, 5:6, :] * ffn
        out_ref[0, rs, :] = _rmsnorm(x2, gf_ref[...])


def _post(x, oa, ob, mg, mod3, wpa, wpb, wo, g2, wg3, wu3, wd3, gf):
    bsz, s, d = x.shape
    tm = TM_POST
    tok = lambda n: pl.BlockSpec((1, tm, n), lambda b, i: (b, i, 0))
    return pl.pallas_call(
        _post_kernel,
        grid=(bsz, s // tm),
        in_specs=[tok(d), tok(oa.shape[2]), tok(ob.shape[2]), tok(mg.shape[2]),
                  pl.BlockSpec((1, 6, d), lambda b, i: (b, 0, 0)),
                  _resident(wpa.shape), _resident(wpb.shape), _resident(wo.shape),
                  pl.BlockSpec((1, d), lambda b, i: (0, 0)),
                  _resident(wg3.shape), _resident(wu3.shape), _resident(wd3.shape),
                  pl.BlockSpec((1, d), lambda b, i: (0, 0))],
        out_specs=tok(d),
        out_shape=jax.ShapeDtypeStruct((bsz, s, d), F32),
        compiler_params=_cparams("parallel", "parallel"),
        name="post",
    )(x, oa, ob, mg, mod3, wpa, wpb, wo, g2.reshape(1, d), wg3, wu3, wd3, gf.reshape(1, d))


def _bf16_pieces(x, n=3):
    out, rem = [], float(x)
    for _ in range(n):
        p = float(np.asarray(rem, np.float32).astype(jnp.bfloat16).astype(np.float32))
        out.append(p)
        rem -= p
    return out


N_PIECES = 3


def _key_extras(pos):
    lane = jnp.arange(LANES)[None, :]
    a = (pos // SEL_BLOCK)[:, None]
    b = (pos % SEL_BLOCK)[:, None]
    ext = jnp.where(lane < N_PIECES, a, jnp.where(lane < 2 * N_PIECES, b, 0)).astype(F32)
    return ext, a


def _query_extras():
    qx = np.zeros((NSA_HEADS, LANES), np.float32)
    for h in range(NSA_HEADS):
        slope = 2.0 ** (-(h + 1))
        for i, p in enumerate(_bf16_pieces(LOG2E, N_PIECES)):
            qx[h, i] = SEL_BLOCK * slope * p
            qx[h, N_PIECES + i] = slope * p
    return jnp.asarray(qx)


def kernel(x, c, w_ada, b_ada, norm_mix_g, w_in, cmp_pos_k, cmp_w1_k, cmp_w2_k, cmp_pos_v, cmp_w1_v, cmp_w2_v,
           w_proj_a, w_proj_b, w_out, norm_ffn_g, w_ffn_gate, w_ffn_up, w_ffn_down, norm_final_g):
    bsz, s, d = x.shape
    depth = w_ada.shape[0]
    assert s % SEL_CHUNK == 0 and s // SEL_BLOCK <= 32 and s >= max(WINDOW + TQ_NSA, SB_FIRST_TILES * TK_SB)
    assert all(s % t == 0 for t in (TM_IN, TM_POST, TQ_NSA, TQ_SELECT, TQ_SB))
    n_q = NSA_HEADS * HEAD_DIM
    n_kv = 6 * NSA_GROUPS * HEAD_DIM
    n_gate = 3 * NSA_HEADS
    n_b = 3 * SB_HEADS * HEAD_DIM
    n_m = 2 * d
    nc = s // CMP_STRIDE
    d_ff = w_ffn_gate.shape[2]
    assert d_ff % FF_CHUNK == 0

    pos = jnp.arange(s)
    ext, blk = _key_extras(pos)
    lane = jnp.arange(LANES)[None, :]
    kx = jnp.where(lane == SEL_LANE0 + blk, 1.0, ext).astype(BF16)
    kxc = _key_extras(jnp.arange(nc) * CMP_STRIDE + CMP_BLOCK - 1)[0].astype(BF16)
    jrow = jnp.arange(LANES)[:, None] - SEL_LANE0
    ncol = jnp.arange(nc)[None, :]
    ovl = ((ncol * CMP_STRIDE <= jrow * SEL_BLOCK + SEL_BLOCK - 1)
           & (ncol * CMP_STRIDE + CMP_BLOCK - 1 >= jrow * SEL_BLOCK)
           & (jrow >= 0) & (jrow < s // SEL_BLOCK) & (ncol < nc - 1)).astype(BF16)
    qx = _query_extras()
    cend = jnp.arange(nc)[:, None] * CMP_STRIDE + CMP_BLOCK - 1
    cmask = jnp.where((pos[None, :] >= cend) & (jnp.arange(nc)[:, None] < nc - 1), 0.0, NEG_INF).astype(F32)
    jj = jnp.arange(TK_SB)
    tri = -jnp.concatenate([jj[:, None] > jj[None, :]] + [jnp.ones((TK_SB, TK_SB), jnp.bool_)] * (SB_FIRST_TILES - 1),
                           axis=0).astype(BF16)

    for l in range(depth):
        w_all = _w_in_layout(w_in[l:l + 1], n_q, n_kv, n_gate, n_b)
        splits = (n_q, n_kv, GATE_PAD, n_b, n_m)

        def cmp_weights(pos_emb, w1, w2):
            half = CMP_BLOCK // 2
            w1r = w1.reshape(2, half, HEAD_DIM, CMP_HIDDEN)
            z = jnp.zeros_like(w1r)
            g0 = jnp.concatenate([w1r, z], axis=3)
            g1 = jnp.concatenate([z, w1r], axis=3)
            w1p = jnp.stack([g0, g1], axis=2).reshape(2 * half * 2 * HEAD_DIM, 2 * CMP_HIDDEN)
            zz = jnp.zeros_like(w2)
            w2p = jnp.concatenate([jnp.concatenate([w2, zz], axis=1), jnp.concatenate([zz, w2], axis=1)], axis=0)
            pe = pos_emb.reshape(2, half, 1, HEAD_DIM)
            pe = jnp.broadcast_to(pe, (2, half, 2, HEAD_DIM)).reshape(2, half * 2 * HEAD_DIM)
            return pe, w1p.astype(BF16), w2p.astype(BF16)

        pk, w1k, w2k = cmp_weights(cmp_pos_k[l], cmp_w1_k[l], cmp_w2_k[l])
        pv, w1v, w2v = cmp_weights(cmp_pos_v[l], cmp_w1_v[l], cmp_w2_v[l])
        wpa = w_proj_a[l].reshape(NSA_GROUPS, NSA_REP, HEAD_DIM, d).transpose(1, 0, 2, 3).reshape(n_q, d).astype(BF16)
        wpb = w_proj_b[l].astype(BF16)
        wo = w_out[l].astype(BF16)
        wg3 = w_ffn_gate[l].astype(BF16)
        wu3 = w_ffn_up[l].astype(BF16)
        wd3 = w_ffn_down[l].astype(BF16)

        mod3 = _ada(c, w_ada[l], b_ada[l]).reshape(bsz, 6, d)
        qa, kva, qkvb, mg, ga = _in_proj(x, mod3, norm_mix_g[l], w_all, splits)
        kc, vc = _compress(kva, pk, pv, w1k, w1v, w2k, w2v)
        selbias = _nsa_select(qa, kc, kxc, ovl, qx, cmask)
        oa = _nsa(qa, kva, kc, vc, ga, kx, kxc, qx, selbias)
        ob = _sb(qkvb, tri)
        gf = norm_final_g if l == depth - 1 else jnp.ones_like(norm_final_g)
        x = _post(x, oa, ob, mg, mod3, wpa, wpb, wo, norm_ffn_g[l], wg3, wu3, wd3, gf)
        assert depth == 1, "final norm is fused into the last layer's post kernel"
    return x
```

```python
import functools
import math

import jax
import jax.numpy as jnp
import numpy as np
from jax import lax
from jax.experimental import pallas as pl
from jax.experimental.pallas import tpu as pltpu

F32 = jnp.float32
BF16 = jnp.bfloat16

HEAD_DIM = 64
NSA_HEADS = 8
NSA_GROUPS = 2
NSA_REP = NSA_HEADS // NSA_GROUPS
SB_HEADS = 8
CMP_BLOCK = 32
CMP_STRIDE = 16
CMP_HIDDEN = 2 * HEAD_DIM
SEL_BLOCK = 64
SEL_TOPK = 8
WINDOW = 512
RMS_EPS = 1e-6
NEG_INF = -1e30
FORCE_PRIORITY = 1e4
ATTN_SCALE = 1.0 / math.sqrt(HEAD_DIM)
LOG2E = math.log2(math.e)

LANES = 128
SEL_LANE0 = 32
GATE_PAD = 128

TM_IN = 1024
IN_CHUNK = 512
TM_POST = 512
POST_CHAINS = 1
TQ_NSA = 256
TQ_SELECT = 2048
NSA_CHAINS = 4
SEL_CHUNK = 512
TQ_SB = 256
TK_SB = 256
FF_CHUNK = 256
SB_DEAD_LOG2 = -150.0
SB_FIRST_TILES = 3
SB_PAIRS = 4

VMEM_LIMIT = 56 * 1024 * 1024


def _cparams(*sem):
    return pltpu.CompilerParams(dimension_semantics=sem, vmem_limit_bytes=VMEM_LIMIT)


def _resident(shape):
    nd = len(shape)
    return pl.BlockSpec(shape, lambda *_: (0,) * nd, pipeline_mode=pl.Buffered(1))


def _dot(a, b):
    return jnp.dot(a, b, preferred_element_type=F32)


def _dot_nt(a, b):
    return lax.dot_general(a, b, (((1,), (1,)), ((), ())), preferred_element_type=F32)


def _split_bf16(v):
    hi = v.astype(BF16)
    lo = (v - hi.astype(F32)).astype(BF16)
    return hi, lo


def _rmsnorm(x, g):
    return x * lax.rsqrt(jnp.mean(x * x, axis=-1, keepdims=True) + RMS_EPS) * g


def _ada_kernel(c_ref, w_ref, b_ref, o_ref):
    c = c_ref[...]
    a = (c * jax.nn.sigmoid(c)).astype(BF16)
    o_ref[...] = _dot(a, w_ref[...].astype(BF16)) + b_ref[...]


def _ada(c, w, b):
    bsz, d = c.shape
    n = w.shape[1]
    tn = d
    return pl.pallas_call(
        _ada_kernel,
        grid=(n // tn,),
        in_specs=[pl.BlockSpec((bsz, d), lambda j: (0, 0)),
                  pl.BlockSpec((d, tn), lambda j: (0, j)),
                  pl.BlockSpec((1, tn), lambda j: (0, j))],
        out_specs=pl.BlockSpec((bsz, tn), lambda j: (0, j)),
        out_shape=jax.ShapeDtypeStruct((bsz, n), F32),
        compiler_params=_cparams("arbitrary"),
        name="ada",
    )(c, w, b.reshape(1, n))


def _w_in_kernel(w_ref, o_ref, *, n_q, n_kv, n_gate, n_b):
    w = w_ref[0]
    rows = w.shape[0]
    qscale = ATTN_SCALE * LOG2E
    for r in range(NSA_REP):
        for g in range(NSA_GROUPS):
            src = (g * NSA_REP + r) * HEAD_DIM
            dst = (r * NSA_GROUPS + g) * HEAD_DIM
            o_ref[:, dst:dst + HEAD_DIM] = (w[:, src:src + HEAD_DIM] * qscale).astype(BF16)
    o1 = n_q + n_kv
    o2 = o1 + n_gate
    o_ref[:, n_q:o1] = w[:, n_q:o1].astype(BF16)
    gate = jnp.concatenate([w[:, o1:o2], jnp.zeros((rows, GATE_PAD - n_gate), F32)], axis=1)
    o_ref[:, o1:o1 + GATE_PAD] = gate.astype(BF16)
    p = o1 + GATE_PAD
    nbq = n_b // 3
    o_ref[:, p:p + nbq] = (w[:, o2:o2 + nbq] * qscale).astype(BF16)
    rest = w.shape[1] - (o2 + nbq)
    o_ref[:, p + nbq:p + nbq + rest] = w[:, o2 + nbq:].astype(BF16)


def _w_in_layout(w_in_l, n_q, n_kv, n_gate, n_b):
    d, d_in = w_in_l.shape[1], w_in_l.shape[2]
    tr = 128
    n_out = d_in - n_gate + GATE_PAD
    return pl.pallas_call(
        functools.partial(_w_in_kernel, n_q=n_q, n_kv=n_kv, n_gate=n_gate, n_b=n_b),
        grid=(d // tr,),
        in_specs=[pl.BlockSpec((1, tr, d_in), lambda i: (0, i, 0))],
        out_specs=pl.BlockSpec((tr, n_out), lambda i: (i, 0)),
        out_shape=jax.ShapeDtypeStruct((d, n_out), BF16),
        compiler_params=_cparams("parallel"),
        name="w_in_layout",
    )(w_in_l)


def _in_proj_kernel(x_ref, mod_ref, g_ref, w_ref, qa_ref, kva_ref, qkvb_ref, mg_ref, ga_ref, *, splits):
    n_q, n_kv, n_g, n_b, n_m = splits
    h = _rmsnorm(x_ref[0], g_ref[...]) * (1.0 + mod_ref[0, 1:2, :]) + mod_ref[0, 0:1, :]
    hb = h.astype(BF16)
    col = 0
    for ref, width, act in ((qa_ref, n_q, None), (kva_ref, n_kv, None), (ga_ref, n_g, "sig"),
                            (qkvb_ref, n_b, None), (mg_ref, n_m, "sig")):
        for c0 in range(0, width, IN_CHUNK):
            cw = min(IN_CHUNK, width - c0)
            r = _dot(hb, w_ref[:, col + c0:col + c0 + cw])
            if act == "sig":
                r = jax.nn.sigmoid(r)
            ref[0, :, c0:c0 + cw] = r.astype(ref.dtype)
        col += width


def _in_proj(x, mod3, g, w_all, splits):
    bsz, s, d = x.shape
    n_q, n_kv, n_g, n_b, n_m = splits
    tm = TM_IN
    out_shape = (jax.ShapeDtypeStruct((bsz, s, n_q), BF16),
                 jax.ShapeDtypeStruct((bsz, s, n_kv), BF16),
                 jax.ShapeDtypeStruct((bsz, s, n_b), BF16),
                 jax.ShapeDtypeStruct((bsz, s, n_m), BF16),
                 jax.ShapeDtypeStruct((bsz, s, n_g), F32))
    tok = lambda n: pl.BlockSpec((1, tm, n), lambda b, i: (b, i, 0))
    return pl.pallas_call(
        functools.partial(_in_proj_kernel, splits=splits),
        grid=(bsz, s // tm),
        in_specs=[tok(d),
                  pl.BlockSpec((1, 6, d), lambda b, i: (b, 0, 0)),
                  pl.BlockSpec((1, d), lambda b, i: (0, 0)),
                  _resident(w_all.shape)],
        out_specs=(tok(n_q), tok(n_kv), tok(n_b), tok(n_m), tok(n_g)),
        out_shape=out_shape,
        compiler_params=_cparams("parallel", "parallel"),
        name="in_proj",
    )(x, mod3, g.reshape(1, d), w_all)


def _gelu_tanh(x):
    return 0.5 * x * (1.0 + jnp.tanh(math.sqrt(2.0 / math.pi) * (x + 0.044715 * (x * x * x))))


def _compress_kernel(k_ref, v_ref, pk_ref, pv_ref, w1k_ref, w1v_ref, w2k_ref, w2v_ref, kc_ref, vc_ref, x32_ref):
    s = k_ref.shape[1]
    nc = s // CMP_STRIDE
    half = w1k_ref.shape[0] // 2

    def one(x_ref, p_ref, w1_ref, w2_ref, o_ref):
        x32_ref[...] = x_ref[0].astype(F32)
        a = b = None
        for l in range(CMP_STRIDE):
            xl = x32_ref[pl.ds(l, nc, stride=CMP_STRIDE), :]
            rows = slice(l * LANES, (l + 1) * LANES)
            top = (xl + p_ref[0:1, rows]).astype(BF16)
            bot = (xl + p_ref[1:2, rows]).astype(BF16)
            da = _dot(top, w1_ref[rows, :])
            db = _dot(bot, w1_ref[half + l * LANES:half + (l + 1) * LANES, :])
            a = da if a is None else a + da
            b = db if b is None else b + db
        hid = a + pltpu.roll(b, nc - 1, 0)
        o_ref[0] = _dot(_gelu_tanh(hid).astype(BF16), w2_ref[...]).astype(o_ref.dtype)

    one(k_ref, pk_ref, w1k_ref, w2k_ref, kc_ref)
    one(v_ref, pv_ref, w1v_ref, w2v_ref, vc_ref)


def _compress(kva, pk, pv, w1k, w1v, w2k, w2v):
    bsz, s, _ = kva.shape
    nc = s // CMP_STRIDE
    blk = lambda i: pl.BlockSpec((1, s, LANES), lambda b, i=i: (b, 0, i))
    out = pl.BlockSpec((1, nc, LANES), lambda b: (b, 0, 0))
    full = lambda a: pl.BlockSpec(a.shape, lambda b: (0,) * a.ndim)
    return pl.pallas_call(
        _compress_kernel,
        grid=(bsz,),
        in_specs=[blk(0), blk(1), full(pk), full(pv), full(w1k), full(w1v), full(w2k), full(w2v)],
        out_specs=(out, out),
        out_shape=(jax.ShapeDtypeStruct((bsz, nc, LANES), BF16),) * 2,
        scratch_shapes=[pltpu.VMEM((s, LANES), F32)],
        compiler_params=_cparams("parallel"),
        name="compress",
    )(kva, kva, pk, pv, w1k, w1v, w2k, w2v)


def _build_qaug(qa_ref, qx_ref, qaug_ref, tq):
    lo = lax.broadcasted_iota(jnp.int32, (tq, LANES), 1) < HEAD_DIM
    for g in range(NSA_GROUPS):
        for r in range(NSA_REP):
            h = g * NSA_REP + r
            qr = qa_ref[0, :, r * LANES:(r + 1) * LANES].astype(F32)
            qm = jnp.where(lo if g == 0 else jnp.logical_not(lo), qr, 0.0)
            qaug_ref[h * tq:(h + 1) * tq, 0:LANES] = qm.astype(BF16)
            qaug_ref[h * tq:(h + 1) * tq, LANES:2 * LANES] = jnp.broadcast_to(qx_ref[h:h + 1, :], (tq, LANES)).astype(BF16)


def _nsa_select_kernel(qa_ref, kc_ref, kxc_ref, ovl_ref, qx_ref, cmask_ref, bias_ref, qaug_ref):
    tq = TQ_SELECT
    q0 = pl.program_id(1) * tq
    _build_qaug(qa_ref, qx_ref, qaug_ref, tq)
    kc_aug = jnp.concatenate([kc_ref[0], kxc_ref[...]], axis=1)
    st = _dot_nt(kc_aug, qaug_ref[...])
    cmask = cmask_ref[...]
    sm = st + jnp.concatenate([cmask] * NSA_HEADS, axis=1)
    e = jnp.exp2(sm - jnp.max(sm, axis=0, keepdims=True))
    has_any = jnp.concatenate([cmask[0:1, :] == 0.0] * NSA_HEADS, axis=1)
    p_t = e * jnp.where(has_any, 1.0 / jnp.sum(e, axis=0, keepdims=True), 0.0)
    nsel_rows = 32
    j_s = lax.broadcasted_iota(jnp.int32, (nsel_rows, tq), 0)
    cur = (q0 + lax.broadcasted_iota(jnp.int32, (nsel_rows, tq), 1)) // SEL_BLOCK
    valid_j = j_s <= cur
    forced = ((j_s == 0) | (j_s == cur) | (j_s == cur - 1)) & valid_j
    for g in range(NSA_GROUPS):
        base = g * NSA_REP * tq
        psum = (p_t[:, base:base + tq] + p_t[:, base + tq:base + 2 * tq]
                + p_t[:, base + 2 * tq:base + 3 * tq] + p_t[:, base + 3 * tq:base + 4 * tq])
        p_hi, p_lo = _split_bf16(psum)
        p_slc = _dot(ovl_ref[...], p_hi) + _dot(ovl_ref[...], p_lo)
        prio = p_slc[SEL_LANE0:SEL_LANE0 + nsel_rows]
        prio = jnp.where(valid_j, prio, NEG_INF)
        prio = jnp.where(forced, FORCE_PRIORITY, prio)
        chosen = jnp.zeros((nsel_rows, tq), jnp.bool_)
        for _ in range(SEL_TOPK):
            best = jnp.max(prio, axis=0, keepdims=True)
            first = jnp.min(jnp.where(prio == best, j_s, nsel_rows), axis=0, keepdims=True)
            pick = j_s == first
            chosen = chosen | pick
            prio = jnp.where(pick, -3e38, prio)
        bias_t = jnp.where(chosen, 0.0, NEG_INF)
        bias_t = jnp.concatenate([jnp.zeros((SEL_LANE0, tq), F32), bias_t,
                                  jnp.zeros((LANES - SEL_LANE0 - nsel_rows, tq), F32)], axis=0)
        for c0 in range(0, tq, LANES):
            bias_ref[0, c0:c0 + LANES, g * LANES:(g + 1) * LANES] = bias_t[:, c0:c0 + LANES].T.astype(BF16)


def _nsa_select(qa, kc, kxc, ovl, qx, cmask):
    bsz, s, nq = qa.shape
    nc = kc.shape[1]
    tq = TQ_SELECT
    const = lambda a: pl.BlockSpec(a.shape, lambda b, q: (0,) * a.ndim)
    return pl.pallas_call(
        _nsa_select_kernel,
        grid=(bsz, s // tq),
        in_specs=[pl.BlockSpec((1, tq, nq), lambda b, q: (b, q, 0)),
                  pl.BlockSpec((1, nc, LANES), lambda b, q: (b, 0, 0)),
                  const(kxc), const(ovl), const(qx),
                  pl.BlockSpec((nc, tq), lambda b, q: (0, q))],
        out_specs=pl.BlockSpec((1, tq, NSA_GROUPS * LANES), lambda b, q: (b, q, 0)),
        out_shape=jax.ShapeDtypeStruct((bsz, s, NSA_GROUPS * LANES), BF16),
        scratch_shapes=[pltpu.VMEM((NSA_HEADS * tq, 2 * LANES), BF16)],
        compiler_params=_cparams("parallel", "parallel"),
        name="nsa_select",
    )(qa, kc, kxc, ovl, qx, cmask)


def _nsa_kernel(qa_ref, kc_ref, vc_ref, ksel_ref, vsel_ref, kwin_ref, vwin_ref, kx_ref, kxc_ref, qx_ref, bias_ref,
                ga_ref, o_ref, qaug_ref, qsel_ref, oacc_ref):
    tq = TQ_NSA
    nh = NSA_HEADS
    rows = nh * tq
    qi = pl.program_id(1)
    q0 = pl.multiple_of(qi * tq, tq)
    nc = kc_ref.shape[1]
    n_cmp = nc - 1

    lane = lax.broadcasted_iota(jnp.int32, (tq, LANES), 1)
    row = lax.broadcasted_iota(jnp.int32, (tq, LANES), 0)
    lo = lane < HEAD_DIM

    _build_qaug(qa_ref, qx_ref, qaug_ref, tq)
    for h in range(nh):
        g = h // NSA_REP
        qsel_ref[h * tq:(h + 1) * tq, 0:LANES] = qaug_ref[h * tq:(h + 1) * tq, 0:LANES]
        qsel_ref[h * tq:(h + 1) * tq, LANES:2 * LANES] = (
            bias_ref[0, :, g * LANES:(g + 1) * LANES].astype(F32) + qx_ref[h:h + 1, :]).astype(BF16)

    ga = ga_ref[0]

    def emit(o_heads, branch, first):
        for r in range(NSA_REP):
            c0 = r * 3 + branch
            c1 = NSA_REP * 3 + r * 3 + branch
            gt = jnp.where(lo, ga[:, c0:c0 + 1], ga[:, c1:c1 + 1])
            val = gt * jnp.where(lo, o_heads[r * tq:(r + 1) * tq], o_heads[(NSA_REP + r) * tq:(NSA_REP + r + 1) * tq])
            if first:
                oacc_ref[:, r * LANES:(r + 1) * LANES] = val
            else:
                oacc_ref[:, r * LANES:(r + 1) * LANES] += val

    def attend(q, k_aug, v, mask_fn):
        w = k_aug.shape[0]
        v_aug = jnp.concatenate([v, jnp.ones((w, LANES), BF16)], axis=1)
        part = rows // NSA_CHAINS
        outs = []
        for c in range(NSA_CHAINS):
            s3 = mask_fn(_dot_nt(q[c * part:(c + 1) * part], k_aug).reshape(part // tq, tq, w))
            p = jnp.exp2(s3 - jnp.max(s3, axis=-1, keepdims=True)).astype(BF16).reshape(part, w)
            pv = _dot(p, v_aug)
            outs.append(pv[:, 0:LANES] / pv[:, LANES:2 * LANES])
        return jnp.concatenate(outs, axis=0)

    qaug = qaug_ref[...]

    qsel = qsel_ref[...]
    last = q0 // SEL_CHUNK

    def branches(n_full):
        kc_aug = jnp.concatenate([kc_ref[0], kxc_ref[...]], axis=1)
        n_l = lax.broadcasted_iota(jnp.int32, (tq, nc), 1)
        t_l = q0 + lax.broadcasted_iota(jnp.int32, (tq, nc), 0)
        valid = (t_l >= n_l * CMP_STRIDE + (CMP_BLOCK - 1)) & (n_l < n_cmp)
        o_cmp = attend(qaug, kc_aug, vc_ref[0], lambda s3: jnp.where(valid[None], s3, NEG_INF))
        has_cmp = jnp.concatenate([q0 + row >= CMP_BLOCK - 1] * nh, axis=0)
        emit(jnp.where(has_cmp, o_cmp, 0.0), 0, True)

        span = WINDOW + tq
        w_start = pl.multiple_of(jnp.maximum(q0 - WINDOW, 0), tq)
        k_aug = jnp.concatenate([kwin_ref[0, pl.ds(w_start, span), :], kx_ref[pl.ds(w_start, span), :]], axis=1)
        dist = (q0 + lax.broadcasted_iota(jnp.int32, (tq, span), 0)) - (w_start + lax.broadcasted_iota(jnp.int32, (tq, span), 1))
        band = (dist >= 0) & (dist < WINDOW)
        emit(attend(qaug, k_aug, vwin_ref[0, pl.ds(w_start, span), :], lambda s3: jnp.where(band[None], s3, NEG_INF)), 2, False)

        nk = (n_full + 1) * SEL_CHUNK
        k_aug = jnp.concatenate([ksel_ref[0, 0:nk, :], kx_ref[0:nk, :]], axis=1)
        kpos = n_full * SEL_CHUNK + lax.broadcasted_iota(jnp.int32, (tq, SEL_CHUNK), 1)
        tpos = q0 + lax.broadcasted_iota(jnp.int32, (tq, SEL_CHUNK), 0)
        causal = (kpos <= tpos)[None]

        def mask_fn(s3):
            diag = jnp.where(causal, s3[:, :, n_full * SEL_CHUNK:nk], NEG_INF)
            return jnp.concatenate([s3[:, :, 0:n_full * SEL_CHUNK], diag], axis=2) if n_full else diag

        emit(attend(qsel, k_aug, vsel_ref[0, 0:nk, :], mask_fn), 1, False)

    for n_full in range(ksel_ref.shape[1] // SEL_CHUNK):
        pl.when(last == n_full)(functools.partial(branches, n_full))

    o_ref[0] = oacc_ref[...].astype(o_ref.dtype)


def _nsa(qa, kva, kc, vc, ga, kx, kxc, qx, selbias):
    bsz, s, nq = qa.shape
    nc = kc.shape[1]
    tq = TQ_NSA
    rows = NSA_HEADS * tq
    kv_blk = lambda i: pl.BlockSpec((1, s, LANES), lambda b, q, i=i: (b, 0, i))
    cmp_blk = pl.BlockSpec((1, nc, LANES), lambda b, q: (b, 0, 0))
    const = lambda a: pl.BlockSpec(a.shape, lambda b, q: (0,) * a.ndim)
    return pl.pallas_call(
        _nsa_kernel,
        grid=(bsz, s // tq),
        in_specs=[pl.BlockSpec((1, tq, nq), lambda b, q: (b, q, 0)),
                  cmp_blk, cmp_blk, kv_blk(2), kv_blk(3), kv_blk(4), kv_blk(5),
                  const(kx), const(kxc), const(qx),
                  pl.BlockSpec((1, tq, NSA_GROUPS * LANES), lambda b, q: (b, q, 0)),
                  pl.BlockSpec((1, tq, GATE_PAD), lambda b, q: (b, q, 0))],
        out_specs=pl.BlockSpec((1, tq, nq), lambda b, q: (b, q, 0)),
        out_shape=jax.ShapeDtypeStruct((bsz, s, nq), BF16),
        scratch_shapes=[pltpu.VMEM((rows, 2 * LANES), BF16),
                        pltpu.VMEM((rows, 2 * LANES), BF16),
                        pltpu.VMEM((tq, nq), F32)],
        compiler_params=_cparams("parallel", "arbitrary"),
        name="nsa",
    )(qa, kc, vc, kva, kva, kva, kva, kx, kxc, qx, selbias, ga)


def _sb_kernel(q_ref, k_ref, v_ref, tri_ref, o_ref, carry_ref, acc_ref):
    tq, tk = TQ_SB, TK_SB
    qi = pl.program_id(2)
    lane = lax.broadcasted_iota(jnp.int32, (tq, LANES), 1)
    lo = lane < HEAD_DIM
    ntri = tri_ref[0:tk, :]
    pairs = range(SB_PAIRS)
    cols = [slice(p * LANES, (p + 1) * LANES) for p in pairs]
    rows = [slice(p * 2 * tq, (p + 1) * 2 * tq) for p in pairs]
    qs = []
    for p in pairs:
        q2 = q_ref[0, :, cols[p]].astype(F32)
        qs.append(jnp.concatenate([jnp.where(lo, q2, 0.0), jnp.where(lo, 0.0, q2)], axis=0).astype(BF16))

    def stick(z):
        sp = jnp.maximum(z, 0.0) + jnp.log2(1.0 + jnp.exp2(-jnp.abs(z)))
        return sp, z - sp

    kpos = lax.broadcasted_iota(jnp.int32, (tq, tk), 1)
    tpos = lax.broadcasted_iota(jnp.int32, (tq, tk), 0)
    keep = jnp.concatenate([kpos < tpos] * 2, axis=0)

    def first_block(nt):
        w = nt * tk
        d0 = w - tk
        k_start = pl.multiple_of((qi - (nt - 1)) * tk, tk)
        for p in pairs:
            sp, ls = stick(_dot_nt(qs[p], k_ref[0, pl.ds(k_start, w), cols[p]]))
            sp_d = jnp.where(keep, sp[:, d0:w], 0.0)
            sp_b = (jnp.concatenate([sp[:, 0:d0], sp_d], axis=1) if nt > 1 else sp_d).astype(BF16)
            a_parts = []
            for i in range(nt):
                later = _dot(sp_b[:, i * tk:w], tri_ref[0:w - i * tk, :])
                a = jnp.exp2(ls[:, i * tk:(i + 1) * tk] + later)
                a_parts.append(jnp.where(keep, a, 0.0) if i == nt - 1 else a)
                if i == 0:
                    carry_ref[rows[p], :] = later[:, 0:1] - sp_b[:, 0:1].astype(F32)
            a = (jnp.concatenate(a_parts, axis=1) if nt > 1 else a_parts[0]).astype(BF16)
            acc_ref[rows[p], :] = _dot(a, v_ref[0, pl.ds(k_start, w), cols[p]])

    for nt in range(1, SB_FIRST_TILES):
        pl.when(qi == nt - 1)(functools.partial(first_block, nt))

    @pl.when(qi >= SB_FIRST_TILES - 1)
    def _():
        first_block(SB_FIRST_TILES)

        for p in pairs:
            def tile(kt, p=p):
                k0 = pl.multiple_of(kt * tk, tk)
                sp, ls = stick(_dot_nt(qs[p], k_ref[0, pl.ds(k0, tk), cols[p]]))
                later = _dot(sp.astype(BF16), ntri)
                pv = _dot(jnp.exp2(ls + later).astype(BF16), v_ref[0, pl.ds(k0, tk), cols[p]])
                carry = carry_ref[rows[p], :]
                acc_ref[rows[p], :] += jnp.exp2(carry) * pv
                carry_ref[rows[p], :] = carry - jnp.sum(sp, axis=-1, keepdims=True)

            def live(p=p):
                return jnp.max(carry_ref[rows[p], :]) > SB_DEAD_LOG2

            def cond(st):
                return (st[0] >= 0) & st[1]

            def body(st, tile=tile, live=live):
                tile(st[0])
                return st[0] - 1, live()

            lax.while_loop(cond, body, (qi - SB_FIRST_TILES, live()))

    for p in pairs:
        r0 = p * 2 * tq
        o_ref[0, :, cols[p]] = jnp.where(lo, acc_ref[r0:r0 + tq], acc_ref[r0 + tq:r0 + 2 * tq]).astype(o_ref.dtype)


def _sb(qkvb, tri):
    bsz, s, n3 = qkvb.shape
    nblk = SB_HEADS // 2 // SB_PAIRS
    wide = SB_PAIRS * LANES
    tq = TQ_SB
    return pl.pallas_call(
        _sb_kernel,
        grid=(bsz, nblk, s // tq),
        in_specs=[pl.BlockSpec((1, tq, wide), lambda b, p, q: (b, q, p)),
                  pl.BlockSpec((1, s, wide), lambda b, p, q: (b, 0, nblk + p)),
                  pl.BlockSpec((1, s, wide), lambda b, p, q: (b, 0, 2 * nblk + p)),
                  pl.BlockSpec(tri.shape, lambda b, p, q: (0, 0))],
        out_specs=pl.BlockSpec((1, tq, wide), lambda b, p, q: (b, q, p)),
        out_shape=jax.ShapeDtypeStruct((bsz, s, nblk * wide), BF16),
        scratch_shapes=[pltpu.VMEM((SB_PAIRS * 2 * tq, 1), F32),
                        pltpu.VMEM((SB_PAIRS * 2 * tq, LANES), F32)],
        compiler_params=_cparams("parallel", "parallel", "arbitrary"),
        name="sb",
    )(qkvb, qkvb, qkvb, tri)


def _post_kernel(x_ref, oa_ref, ob_ref, mg_ref, mod_ref, wpa_ref, wpb_ref, wo_ref, g2_ref,
                 wg_ref, wu_ref, wd_ref, gf_ref, out_ref):
    d = x_ref.shape[2]
    part_rows = x_ref.shape[1] // POST_CHAINS
    for i in range(POST_CHAINS):
        rs = slice(i * part_rows, (i + 1) * part_rows)
        x = x_ref[0, rs, :]
        ya = _dot(oa_ref[0, rs, :], wpa_ref[...])
        yb = _dot(ob_ref[0, rs, :], wpb_ref[...])
        y = mg_ref[0, rs, 0:d].astype(F32) * ya + mg_ref[0, rs, d:2 * d].astype(F32) * yb
        x1 = x + mod_ref[0, 2:3, :] * _dot(y.astype(BF16), wo_ref[...])
        h2 = (_rmsnorm(x1, g2_ref[...]) * (1.0 + mod_ref[0, 4:5, :]) + mod_ref[0, 3:4, :]).astype(BF16)
        acts = []
        for c0 in range(0, wg_ref.shape[1], FF_CHUNK):
            gte = _dot(h2, wg_ref[:, c0:c0 + FF_CHUNK])
            up = _dot(h2, wu_ref[:, c0:c0 + FF_CHUNK])
            acts.append((gte * jax.nn.sigmoid(gte) * up).astype(BF16))
        ffn = _dot(jnp.concatenate(acts, axis=1), wd_ref[...])
        x2 = x1 + mod_ref[0, 5:6, :] * ffn
        out_ref[0, rs, :] = _rmsnorm(x2, gf_ref[...])


def _post(x, oa, ob, mg, mod3, wpa, wpb, wo, g2, wg3, wu3, wd3, gf):
    bsz, s, d = x.shape
    tm = TM_POST
    tok = lambda n: pl.BlockSpec((1, tm, n), lambda b, i: (b, i, 0))
    return pl.pallas_call(
        _post_kernel,
        grid=(bsz, s // tm),
        in_specs=[tok(d), tok(oa.shape[2]), tok(ob.shape[2]), tok(mg.shape[2]),
                  pl.BlockSpec((1, 6, d), lambda b, i: (b, 0, 0)),
                  _resident(wpa.shape), _resident(wpb.shape), _resident(wo.shape),
                  pl.BlockSpec((1, d), lambda b, i: (0, 0)),
                  _resident(wg3.shape), _resident(wu3.shape), _resident(wd3.shape),
                  pl.BlockSpec((1, d), lambda b, i: (0, 0))],
        out_specs=tok(d),
        out_shape=jax.ShapeDtypeStruct((bsz, s, d), F32),
        compiler_params=_cparams("parallel", "parallel"),
        name="post",
    )(x, oa, ob, mg, mod3, wpa, wpb, wo, g2.reshape(1, d), wg3, wu3, wd3, gf.reshape(1, d))


def _bf16_pieces(x, n=3):
    out, rem = [], float(x)
    for _ in range(n):
        p = float(np.asarray(rem, np.float32).astype(jnp.bfloat16).astype(np.float32))
        out.append(p)
        rem -= p
    return out


N_PIECES = 3


def _key_extras(pos):
    lane = jnp.arange(LANES)[None, :]
    a = (pos // SEL_BLOCK)[:, None]
    b = (pos % SEL_BLOCK)[:, None]
    ext = jnp.where(lane < N_PIECES, a, jnp.where(lane < 2 * N_PIECES, b, 0)).astype(F32)
    return ext, a


def _query_extras():
    qx = np.zeros((NSA_HEADS, LANES), np.float32)
    for h in range(NSA_HEADS):
        slope = 2.0 ** (-(h + 1))
        for i, p in enumerate(_bf16_pieces(LOG2E, N_PIECES)):
            qx[h, i] = SEL_BLOCK * slope * p
            qx[h, N_PIECES + i] = slope * p
    return jnp.asarray(qx)


def kernel(x, c, w_ada, b_ada, norm_mix_g, w_in, cmp_pos_k, cmp_w1_k, cmp_w2_k, cmp_pos_v, cmp_w1_v, cmp_w2_v,
           w_proj_a, w_proj_b, w_out, norm_ffn_g, w_ffn_gate, w_ffn_up, w_ffn_down, norm_final_g):
    bsz, s, d = x.shape
    depth = w_ada.shape[0]
    assert s % SEL_CHUNK == 0 and s // SEL_BLOCK <= 32 and s >= max(WINDOW + TQ_NSA, SB_FIRST_TILES * TK_SB)
    assert all(s % t == 0 for t in (TM_IN, TM_POST, TQ_NSA, TQ_SELECT, TQ_SB))
    n_q = NSA_HEADS * HEAD_DIM
    n_kv = 6 * NSA_GROUPS * HEAD_DIM
    n_gate = 3 * NSA_HEADS
    n_b = 3 * SB_HEADS * HEAD_DIM
    n_m = 2 * d
    nc = s // CMP_STRIDE
    d_ff = w_ffn_gate.shape[2]
    assert d_ff % FF_CHUNK == 0

    pos = jnp.arange(s)
    ext, blk = _key_extras(pos)
    lane = jnp.arange(LANES)[None, :]
    kx = jnp.where(lane == SEL_LANE0 + blk, 1.0, ext).astype(BF16)
    kxc = _key_extras(jnp.arange(nc) * CMP_STRIDE + CMP_BLOCK - 1)[0].astype(BF16)
    jrow = jnp.arange(LANES)[:, None] - SEL_LANE0
    ncol = jnp.arange(nc)[None, :]
    ovl = ((ncol * CMP_STRIDE <= jrow * SEL_BLOCK + SEL_BLOCK - 1)
           & (ncol * CMP_STRIDE + CMP_BLOCK - 1 >= jrow * SEL_BLOCK)
           & (jrow >= 0) & (jrow < s // SEL_BLOCK) & (ncol < nc - 1)).astype(BF16)
    qx = _query_extras()
    cend = jnp.arange(nc)[:, None] * CMP_STRIDE + CMP_BLOCK - 1
    cmask = jnp.where((pos[None, :] >= cend) & (jnp.arange(nc)[:, None] < nc - 1), 0.0, NEG_INF).astype(F32)
    jj = jnp.arange(TK_SB)
    tri = -jnp.concatenate([jj[:, None] > jj[None, :]] + [jnp.ones((TK_SB, TK_SB), jnp.bool_)] * (SB_FIRST_TILES - 1),
                           axis=0).astype(BF16)

    for l in range(depth):
        w_all = _w_in_layout(w_in[l:l + 1], n_q, n_kv, n_gate, n_b)
        splits = (n_q, n_kv, GATE_PAD, n_b, n_m)

        def cmp_weights(pos_emb, w1, w2):
            half = CMP_BLOCK // 2
            w1r = w1.reshape(2, half, HEAD_DIM, CMP_HIDDEN)
            z = jnp.zeros_like(w1r)
            g0 = jnp.concatenate([w1r, z], axis=3)
            g1 = jnp.concatenate([z, w1r], axis=3)
            w1p = jnp.stack([g0, g1], axis=2).reshape(2 * half * 2 * HEAD_DIM, 2 * CMP_HIDDEN)
            zz = jnp.zeros_like(w2)
            w2p = jnp.concatenate([jnp.concatenate([w2, zz], axis=1), jnp.concatenate([zz, w2], axis=1)], axis=0)
            pe = pos_emb.reshape(2, half, 1, HEAD_DIM)
            pe = jnp.broadcast_to(pe, (2, half, 2, HEAD_DIM)).reshape(2, half * 2 * HEAD_DIM)
            return pe, w1p.astype(BF16), w2p.astype(BF16)

        pk, w1k, w2k = cmp_weights(cmp_pos_k[l], cmp_w1_k[l], cmp_w2_k[l])
        pv, w1v, w2v = cmp_weights(cmp_pos_v[l], cmp_w1_v[l], cmp_w2_v[l])
        wpa = w_proj_a[l].reshape(NSA_GROUPS, NSA_REP, HEAD_DIM, d).transpose(1, 0, 2, 3).reshape(n_q, d).astype(BF16)
        wpb = w_proj_b[l].astype(BF16)
        wo = w_out[l].astype(BF16)
        wg3 = w_ffn_gate[l].astype(BF16)
        wu3 = w_ffn_up[l].astype(BF16)
        wd3 = w_ffn_down[l].astype(BF16)

        mod3 = _ada(c, w_ada[l], b_ada[l]).reshape(bsz, 6, d)
        qa, kva, qkvb, mg, ga = _in_proj(x, mod3, norm_mix_g[l], w_all, splits)
        kc, vc = _compress(kva, pk, pv, w1k, w1v, w2k, w2v)
        selbias = _nsa_select(qa, kc, kxc, ovl, qx, cmask)
        oa = _nsa(qa, kva, kc, vc, ga, kx, kxc, qx, selbias)
        ob = _sb(qkvb, tri)
        gf = norm_final_g if l == depth - 1 else jnp.ones_like(norm_final_g)
        x = _post(x, oa, ob, mg, mod3, wpa, wpb, wo, norm_ffn_g[l], wg3, wu3, wd3, gf)
        assert depth == 1, "final norm is fused into the last layer's post kernel"
    return x
```
